```python
import math
import jax
import jax.numpy as jnp
from jax import lax
import numpy as np

D_MODEL = 2048
BATCH = 1
SEQ = 16384
DEPTH = 4

GRID_W = 64
CTX_LEN = 256
N_MIXERS = 4
HEAD_DIM = 128
N_HEADS = D_MODEL // HEAD_DIM
N_KV_HEADS = 4
GROUP = N_HEADS // N_KV_HEADS
QKV_WIDTH = (N_HEADS + 2 * N_KV_HEADS) * HEAD_DIM
WINDOW = 128
WIN_BLOCK = 128
Q_BLOCK = 128
ROPE_BASE = 10000.0
ROPE_PAIRS_PER_AXIS = HEAD_DIM // 4
HY_FILTER_WIDTH = 64
HY_BANDS = 8
HY_POS_DIM = 1 + 2 * HY_BANDS
HY_SHORT = 3
HY_MOD_SHIFT = 0.05
S5_GROUP_SIZE = 16
S5_GROUPS = D_MODEL // S5_GROUP_SIZE
S5_STATE = 64
SCAN_CHUNK = 128
N_EXPERTS = 32
TOP_K = 4
EXPERT_FF = 768
SWIGLU_LIMIT = 7.0
SWIGLU_ALPHA = 1.702
MOE_BLOCK = 128
EPS = 1e-6
NEG_INF = -1e30
N_WIN_LAYERS = (DEPTH + 3) // 4
N_FULL_LAYERS = (DEPTH + 2) // 4
N_HYENA_LAYERS = (DEPTH + 1) // 4
N_S5_LAYERS = DEPTH // 4

kernel_name = 'hybrid_interleaved_diffusion_trunk'


def rms_norm(x, gain):
    xf = x.astype(jnp.float32)
    y = xf * lax.rsqrt(jnp.mean(xf * xf, axis=-1, keepdims=True) + EPS)
    return y.astype(x.dtype) * gain


def axial_rope(n_tokens):
    n_rows = n_tokens // GRID_W
    row = jnp.broadcast_to(jnp.arange(n_rows, dtype=jnp.float32)[:, None], (n_rows, GRID_W)).reshape(-1)
    col = jnp.broadcast_to(jnp.arange(GRID_W, dtype=jnp.float32)[None, :], (n_rows, GRID_W)).reshape(-1)
    freqs = ROPE_BASE ** (-jnp.arange(ROPE_PAIRS_PER_AXIS, dtype=jnp.float32) / ROPE_PAIRS_PER_AXIS)
    ang = jnp.concatenate([row[:, None] * freqs, col[:, None] * freqs], axis=-1)
    return jnp.cos(ang)[:, None, :], jnp.sin(ang)[:, None, :]


def apply_rope(t, cos, sin):
    t1, t2 = jnp.split(t.astype(jnp.float32), 2, axis=-1)
    return jnp.concatenate([t1 * cos - t2 * sin, t2 * cos + t1 * sin], axis=-1).astype(t.dtype)


def project_qkv(h, w_qkv, q_gain, k_gain):
    b, n, _ = h.shape
    qkv = h @ w_qkv
    q, k, v = jnp.split(qkv, [N_HEADS * HEAD_DIM, (N_HEADS + N_KV_HEADS) * HEAD_DIM], axis=-1)
    q = rms_norm(q.reshape(b, n, N_HEADS, HEAD_DIM), q_gain)
    k = rms_norm(k.reshape(b, n, N_KV_HEADS, HEAD_DIM), k_gain)
    v = v.reshape(b, n, N_KV_HEADS, HEAD_DIM)
    return q, k, v


def context_attention(qc, kc, vc, sinks):
    b, nc = qc.shape[:2]
    q = qc.reshape(b, nc, N_KV_HEADS, GROUP, HEAD_DIM)
    s = jnp.einsum('bqkgd,bskd->bkgqs', q, kc).astype(jnp.float32) * HEAD_DIM ** -0.5
    if sinks is not None:
        sink = jnp.broadcast_to(sinks.astype(jnp.float32).reshape(1, N_KV_HEADS, GROUP, 1, 1), s.shape[:-1] + (1,))
        p = jax.nn.softmax(jnp.concatenate([s, sink], axis=-1), axis=-1)[..., :-1]
    else:
        p = jax.nn.softmax(s, axis=-1)
    o = jnp.einsum('bkgqs,bskd->bqkgd', p.astype(vc.dtype), vc)
    return o.reshape(b, nc, N_HEADS * HEAD_DIM)


def banded_blocks(t):
    b, n = t.shape[:2]
    tb = t.reshape(b, n // WIN_BLOCK, WIN_BLOCK, N_KV_HEADS, HEAD_DIM)
    tp = jnp.pad(tb, ((0, 0), (1, 1), (0, 0), (0, 0), (0, 0)))
    return jnp.concatenate([tp[:, :-2], tp[:, 1:-1], tp[:, 2:]], axis=2)


def window_attention(q, k, v, kc, vc, sinks):
    b, n = q.shape[:2]
    nb = n // WIN_BLOCK
    n_loc = 3 * WIN_BLOCK
    n_ctx = kc.shape[1]
    qb = jnp.moveaxis(q.reshape(b, nb, WIN_BLOCK, N_KV_HEADS, GROUP, HEAD_DIM), 1, 0)
    kb = jnp.moveaxis(banded_blocks(k), 1, 0)
    vb = jnp.moveaxis(banded_blocks(v), 1, 0)
    blk = jnp.arange(nb, dtype=jnp.int32)
    sink = sinks.astype(jnp.float32).reshape(1, N_KV_HEADS, GROUP, 1, 1)
    scale = HEAD_DIM ** -0.5

    def one_block(args):
        qi, ki, vi, bi = args
        q_pos = bi * WIN_BLOCK + jnp.arange(WIN_BLOCK)
        k_pos = (bi - 1) * WIN_BLOCK + jnp.arange(n_loc)
        valid = (k_pos >= 0)[None, :] & (k_pos < n)[None, :] & (jnp.abs(q_pos[:, None] - k_pos[None, :]) <= WINDOW)
        s_loc = jnp.einsum('bqkgd,bskd->bkgqs', qi, ki).astype(jnp.float32) * scale
        s_loc = jnp.where(valid, s_loc, NEG_INF)
        s_ctx = jnp.einsum('bqkgd,bskd->bkgqs', qi, kc).astype(jnp.float32) * scale
        s_sink = jnp.broadcast_to(sink, s_loc.shape[:-1] + (1,))
        p = jax.nn.softmax(jnp.concatenate([s_loc, s_ctx, s_sink], axis=-1), axis=-1).astype(vi.dtype)
        return (jnp.einsum('bkgqs,bskd->bqkgd', p[..., :n_loc], vi)
                + jnp.einsum('bkgqs,bskd->bqkgd', p[..., n_loc:n_loc + n_ctx], vc))

    o = lax.map(one_block, (qb, kb, vb, blk))
    return jnp.moveaxis(o, 0, 1).reshape(b, n, N_HEADS * HEAD_DIM)


def full_attention(q, k, v, kc, vc):
    b, n = q.shape[:2]
    k_all = jnp.concatenate([k, kc], axis=1)
    v_all = jnp.concatenate([v, vc], axis=1)
    qb = jnp.moveaxis(q.reshape(b, n // Q_BLOCK, Q_BLOCK, N_KV_HEADS, GROUP, HEAD_DIM), 1, 0)

    def one_block(qi):
        s = jnp.einsum('bqkgd,bskd->bkgqs', qi, k_all).astype(jnp.float32) * HEAD_DIM ** -0.5
        p = jax.nn.softmax(s, axis=-1).astype(v_all.dtype)
        return jnp.einsum('bkgqs,bskd->bqkgd', p, v_all)

    o = lax.map(one_block, qb)
    return jnp.moveaxis(o, 0, 1).reshape(b, n, N_HEADS * HEAD_DIM)


def window_mixer(hx, hc, w_qkv, w_o, q_gain, k_gain, sinks, with_ctx):
    cos, sin = axial_rope(hx.shape[1])
    qx, kx, vx = project_qkv(hx, w_qkv, q_gain, k_gain)
    qc, kc, vc = project_qkv(hc, w_qkv, q_gain, k_gain)
    qx, kx = apply_rope(qx, cos, sin), apply_rope(kx, cos, sin)
    yx = window_attention(qx, kx, vx, kc, vc, sinks) @ w_o
    yc = context_attention(qc, kc, vc, sinks) @ w_o if with_ctx else None
    return yx, yc


def full_mixer(hx, hc, w_qkv, w_o, q_gain, k_gain, with_ctx):
    cos, sin = axial_rope(hx.shape[1])
    qx, kx, vx = project_qkv(hx, w_qkv, q_gain, k_gain)
    qc, kc, vc = project_qkv(hc, w_qkv, q_gain, k_gain)
    qx, kx = apply_rope(qx, cos, sin), apply_rope(kx, cos, sin)
    yx = full_attention(qx, kx, vx, kc, vc) @ w_o
    yc = context_attention(qc, kc, vc, None) @ w_o if with_ctx else None
    return yx, yc


def short_conv(u, w, bias):
    ch = u.shape[-1]
    y = lax.conv_general_dilated(u, w[:, None, :].astype(u.dtype), window_strides=(1,),
                                 padding=((HY_SHORT // 2, HY_SHORT // 2),),
                                 dimension_numbers=('NWC', 'WIO', 'NWC'), feature_group_count=ch)
    return y + bias


def hyena_filters(n, w1, b1, w2, b2, w3, b3, sin_freq, decay):
    f32 = jnp.float32
    t = jnp.arange(n, dtype=f32) / n
    bands = jnp.arange(1, HY_BANDS + 1, dtype=f32)
    ang = 2.0 * jnp.pi * t[:, None] * bands[None, :]
    z = jnp.concatenate([t[:, None], jnp.cos(ang), -jnp.sin(ang)], axis=-1)
    sf = sin_freq.astype(f32)
    hid = jnp.sin(sf[0] * (z @ w1.astype(f32) + b1.astype(f32)))
    hid = jnp.sin(sf[1] * (hid @ w2.astype(f32) + b2.astype(f32)))
    filt = hid @ w3.astype(f32) + b3.astype(f32)
    window = jnp.exp(-t[:, None] * jnp.abs(decay.astype(f32))[None, :]) + HY_MOD_SHIFT
    h_fwd, h_bwd = jnp.split(filt, 2, axis=-1)
    return h_fwd * window, h_bwd * window


def bidir_long_conv(u, h_fwd, h_bwd, skip):
    n, d = h_fwd.shape
    g = jnp.concatenate([h_fwd, jnp.zeros((1, d), jnp.float32), h_bwd[:0:-1]], axis=0)
    u32 = u.astype(jnp.float32)
    spec = jnp.fft.rfft(u32, n=2 * n, axis=1) * jnp.fft.rfft(g, axis=0)[None]
    y = jnp.fft.irfft(spec, n=2 * n, axis=1)[:, :n]
    return (y + u32 * skip.astype(jnp.float32)).astype(u.dtype)


def hyena_mixer(hx, hc, w_in, b_in, w_short, b_short, fw1, fb1, fw2, fb2, fw3, fb3, sin_freq, decay, skip,
                w_out, b_out, with_ctx):
    def run(h):
        n = h.shape[1]
        z = short_conv(h @ w_in + b_in, w_short, b_short)
        x0, x1, v = jnp.split(z, 3, axis=-1)
        h_fwd, h_bwd = hyena_filters(n, fw1, fb1, fw2, fb2, fw3, fb3, sin_freq, decay)
        y = x0 * bidir_long_conv(x1 * v, h_fwd, h_bwd, skip)
        return y @ w_out + b_out
    return run(hx), (run(hc) if with_ctx else None)


def s5_discretize(lam_re, lam_im, log_step, b_re, b_im):
    f32 = jnp.float32
    lam = lax.complex(lam_re.astype(f32), lam_im.astype(f32))
    step = jnp.exp(log_step.astype(f32))[:, None]
    a_bar = jnp.exp(lam * step)
    b_mat = lax.complex(b_re.astype(f32), b_im.astype(f32))
    b_bar = ((a_bar - 1.0) / lam)[..., None] * b_mat
    return a_bar, b_bar


def linear_combine(e1, e2):
    a1, b1 = e1
    a2, b2 = e2
    return a1 * a2, a2 * b1 + b2


def s5_scan(u, h0, a_bar, b_bar, c_mat):
    b, n = u.shape[:2]
    chunks = jnp.moveaxis(u.reshape(b, n // SCAN_CHUNK, SCAN_CHUNK, S5_GROUPS, S5_GROUP_SIZE), 1, 0)

    def step(h, u_blk):
        bu = jnp.einsum('gpc,btgc->btgp', b_bar, u_blk.astype(jnp.complex64))
        bu = bu.at[:, 0].add(a_bar[None] * h)
        a = jnp.broadcast_to(a_bar, bu.shape)
        _, states = lax.associative_scan(linear_combine, (a, bu), axis=1)
        y = None if c_mat is None else jnp.einsum('gcp,btgp->btgc', c_mat, states).real
        return states[:, -1], y

    h_last, ys = lax.scan(step, h0, chunks)
    if ys is not None:
        ys = jnp.moveaxis(ys, 0, 1).reshape(b, n, S5_GROUPS, S5_GROUP_SIZE)
    return ys, h_last


def s5_mixer(hx, hc, lam_re, lam_im, log_step, b_re, b_im, c_re, c_im, skip, w_glu, b_glu, with_ctx):
    f32 = jnp.float32
    b, n, d = hx.shape
    nc = hc.shape[1]
    ux = hx.astype(f32).reshape(b, n, S5_GROUPS, S5_GROUP_SIZE)
    uc = hc.astype(f32).reshape(b, nc, S5_GROUPS, S5_GROUP_SIZE)
    yx = skip.astype(f32) * hx.astype(f32)
    yc = skip.astype(f32) * hc.astype(f32) if with_ctx else None
    h_zero = jnp.zeros((b, S5_GROUPS, S5_STATE), jnp.complex64)
    for direction in range(2):
        a_bar, b_bar = s5_discretize(lam_re[direction], lam_im[direction], log_step[direction],
                                     b_re[direction], b_im[direction])
        c_mat = lax.complex(c_re[direction].astype(f32), c_im[direction].astype(f32))
        rev = (lambda t: t[:, ::-1]) if direction == 1 else (lambda t: t)
        y_ctx, h_ctx = s5_scan(rev(uc), h_zero, a_bar, b_bar, c_mat if with_ctx else None)
        y_lat, _ = s5_scan(rev(ux), h_ctx, a_bar, b_bar, c_mat)
        yx = yx + rev(y_lat).reshape(b, n, d)
        if with_ctx:
            yc = yc + rev(y_ctx).reshape(b, nc, d)

    def glu(y):
        z = jax.nn.gelu(y).astype(hx.dtype)
        val, gate = jnp.split(z @ w_glu + b_glu, 2, axis=-1)
        return val * jax.nn.sigmoid(gate)

    return glu(yx), (glu(yc) if with_ctx else None)


def clamped_swiglu(gu):
    glu, lin = jnp.split(gu, 2, axis=-1)
    glu = jnp.minimum(glu, SWIGLU_LIMIT)
    lin = jnp.clip(lin, -SWIGLU_LIMIT, SWIGLU_LIMIT)
    return glu * jax.nn.sigmoid(SWIGLU_ALPHA * glu) * (lin + 1.0)


def moe_ffn(h, router_w, router_b, w_gu, b_gu, w_dn, b_dn):
    n, d = h.shape
    logits = (h @ router_w + router_b).astype(jnp.float32)
    top_logit, top_idx = lax.top_k(logits, TOP_K)
    gates = jax.nn.softmax(top_logit, axis=-1)
    flat_e = top_idx.reshape(-1)
    flat_tok = jnp.repeat(jnp.arange(n, dtype=jnp.int32), TOP_K)
    order = jnp.argsort(flat_e)
    e_sorted = flat_e[order]
    tok_sorted = flat_tok[order]
    w_sorted = gates.reshape(-1)[order]
    counts = jnp.bincount(flat_e, length=N_EXPERTS)
    padded = (counts + MOE_BLOCK - 1) // MOE_BLOCK * MOE_BLOCK
    pad_end = jnp.cumsum(padded)
    pad_start = pad_end - padded
    grp_start = jnp.cumsum(counts) - counts
    dest = pad_start[e_sorted] + jnp.arange(n * TOP_K, dtype=jnp.int32) - grp_start[e_sorted]
    n_blocks = -(-(n * TOP_K + N_EXPERTS * (MOE_BLOCK - 1)) // MOE_BLOCK)
    src = jnp.full((n_blocks * MOE_BLOCK,), n, jnp.int32).at[dest].set(tok_sorted)
    xb = jnp.concatenate([h, jnp.zeros((1, d), h.dtype)], axis=0)[src].reshape(n_blocks, MOE_BLOCK, d)
    blk_expert = jnp.minimum(jnp.searchsorted(pad_end, jnp.arange(n_blocks) * MOE_BLOCK, side='right'),
                             N_EXPERTS - 1)

    def expert_block(args):
        xe, e = args
        return clamped_swiglu(xe @ w_gu[e] + b_gu[e]) @ w_dn[e] + b_dn[e]

    yb = lax.map(expert_block, (xb, blk_expert)).reshape(-1, d)
    y = yb[dest] * w_sorted[:, None].astype(yb.dtype)
    return jax.ops.segment_sum(y, tok_sorted, num_segments=n)


def setup_inputs(seed: int = 0) -> dict:
    key = jax.random.key(seed)
    ks = iter(jax.random.split(key, 64))
    f32 = jnp.float32

    def nrm(shape, scale=1.0):
        return jax.random.normal(next(ks), shape, f32) * scale

    def gain(shape):
        return 1.0 + nrm(shape, 0.02)

    d, f, e = D_MODEL, EXPERT_FF, N_EXPERTS
    hd_all = N_HEADS * HEAD_DIM
    inp = {}
    inp['x'] = nrm((BATCH, SEQ, d))
    inp['c'] = nrm((BATCH, d))
    inp['ctx'] = nrm((BATCH, CTX_LEN, d))
    inp['c_ctx'] = nrm((d,))
    inp['w_mod'] = nrm((DEPTH, d, 6 * d), 0.5 * d ** -0.5)
    inp['b_mod'] = nrm((DEPTH, 6 * d), 0.02)
    inp['norm_mix'] = gain((DEPTH, d))
    inp['norm_ffn'] = gain((DEPTH, d))
    inp['router_w'] = nrm((DEPTH, d, e), d ** -0.5)
    inp['router_b'] = nrm((DEPTH, e), 0.01)
    inp['w_gate_up'] = nrm((DEPTH, e, d, 2 * f), d ** -0.5)
    inp['b_gate_up'] = nrm((DEPTH, e, 2 * f), 0.02)
    inp['w_down'] = nrm((DEPTH, e, f, d), f ** -0.5)
    inp['b_down'] = nrm((DEPTH, e, d), 0.02)
    inp['win_w_qkv'] = nrm((N_WIN_LAYERS, d, QKV_WIDTH), d ** -0.5)
    inp['win_w_o'] = nrm((N_WIN_LAYERS, hd_all, d), hd_all ** -0.5)
    inp['win_q_gain'] = gain((N_WIN_LAYERS, HEAD_DIM))
    inp['win_k_gain'] = gain((N_WIN_LAYERS, HEAD_DIM))
    inp['win_sinks'] = nrm((N_WIN_LAYERS, N_HEADS), 0.5)
    inp['full_w_qkv'] = nrm((N_FULL_LAYERS, d, QKV_WIDTH), d ** -0.5)
    inp['full_w_o'] = nrm((N_FULL_LAYERS, hd_all, d), hd_all ** -0.5)
    inp['full_q_gain'] = gain((N_FULL_LAYERS, HEAD_DIM))
    inp['full_k_gain'] = gain((N_FULL_LAYERS, HEAD_DIM))
    inp['hy_w_in'] = nrm((N_HYENA_LAYERS, d, 3 * d), d ** -0.5)
    inp['hy_b_in'] = nrm((N_HYENA_LAYERS, 3 * d), 0.02)
    inp['hy_w_short'] = nrm((N_HYENA_LAYERS, HY_SHORT, 3 * d), HY_SHORT ** -0.5)
    inp['hy_b_short'] = nrm((N_HYENA_LAYERS, 3 * d), 0.02)
    inp['hy_filt_w1'] = nrm((N_HYENA_LAYERS, HY_POS_DIM, HY_FILTER_WIDTH), HY_POS_DIM ** -0.5)
    inp['hy_filt_b1'] = nrm((N_HYENA_LAYERS, HY_FILTER_WIDTH), 0.1)
    inp['hy_filt_w2'] = nrm((N_HYENA_LAYERS, HY_FILTER_WIDTH, HY_FILTER_WIDTH), HY_FILTER_WIDTH ** -0.5)
    inp['hy_filt_b2'] = nrm((N_HYENA_LAYERS, HY_FILTER_WIDTH), 0.1)
    inp['hy_filt_w3'] = nrm((N_HYENA_LAYERS, HY_FILTER_WIDTH, 2 * d), 0.015 * HY_FILTER_WIDTH ** -0.5)
    inp['hy_filt_b3'] = nrm((N_HYENA_LAYERS, 2 * d), 0.002)
    inp['hy_sin_freq'] = 1.0 + nrm((N_HYENA_LAYERS, 2, HY_FILTER_WIDTH), 0.1)
    decay0 = jnp.linspace(math.log(100.0) / 1.5, math.log(100.0) / 0.3, d, dtype=f32)
    inp['hy_decay'] = decay0[None, :] * (1.0 + nrm((N_HYENA_LAYERS, d), 0.05))
    inp['hy_skip'] = nrm((N_HYENA_LAYERS, d))
    inp['hy_w_out'] = nrm((N_HYENA_LAYERS, d, d), d ** -0.5)
    inp['hy_b_out'] = nrm((N_HYENA_LAYERS, d), 0.02)
    s5_shape = (N_S5_LAYERS, 2, S5_GROUPS, S5_STATE)
    inp['s5_lam_re'] = -0.5 + nrm(s5_shape, 0.01)
    inp['s5_lam_im'] = jnp.pi * jnp.arange(S5_STATE, dtype=f32) + nrm(s5_shape, 0.01)
    inp['s5_log_step'] = jax.random.uniform(next(ks), (N_S5_LAYERS, 2, S5_GROUPS), f32,
                                            math.log(1e-3), math.log(1e-1))
    inp['s5_b_re'] = nrm(s5_shape + (S5_GROUP_SIZE,), (2 * S5_GROUP_SIZE) ** -0.5)
    inp['s5_b_im'] = nrm(s5_shape + (S5_GROUP_SIZE,), (2 * S5_GROUP_SIZE) ** -0.5)
    inp['s5_c_re'] = nrm((N_S5_LAYERS, 2, S5_GROUPS, S5_GROUP_SIZE, S5_STATE), S5_STATE ** -0.25)
    inp['s5_c_im'] = nrm((N_S5_LAYERS, 2, S5_GROUPS, S5_GROUP_SIZE, S5_STATE), S5_STATE ** -0.25)
    inp['s5_skip'] = nrm((N_S5_LAYERS, d))
    inp['s5_w_glu'] = nrm((N_S5_LAYERS, d, 2 * d), d ** -0.5)
    inp['s5_b_glu'] = nrm((N_S5_LAYERS, 2 * d), 0.02)
    return inp


def reference(x, c, ctx, c_ctx, w_mod, b_mod, norm_mix, norm_ffn, router_w, router_b, w_gate_up, b_gate_up,
              w_down, b_down, win_w_qkv, win_w_o, win_q_gain, win_k_gain, win_sinks, full_w_qkv, full_w_o,
              full_q_gain, full_k_gain, hy_w_in, hy_b_in, hy_w_short, hy_b_short, hy_filt_w1, hy_filt_b1,
              hy_filt_w2, hy_filt_b2, hy_filt_w3, hy_filt_b3, hy_sin_freq, hy_decay, hy_skip, hy_w_out, hy_b_out,
              s5_lam_re, s5_lam_im, s5_log_step, s5_b_re, s5_b_im, s5_c_re, s5_c_im, s5_skip, s5_w_glu, s5_b_glu):
    cx = ctx
    for i in range(DEPTH):
        kind, j = i % N_MIXERS, i // N_MIXERS
        with_ctx = i < DEPTH - 1
        mod_x = (jax.nn.silu(c) @ w_mod[i] + b_mod[i])[:, None, :]
        mod_c = jax.nn.silu(c_ctx) @ w_mod[i] + b_mod[i]
        sh1, sc1, gt1, sh2, sc2, gt2 = jnp.split(mod_x, 6, axis=-1)
        csh1, csc1, cgt1, csh2, csc2, cgt2 = jnp.split(mod_c, 6, axis=-1)
        hx = rms_norm(x, norm_mix[i]) * (1.0 + sc1) + sh1
        hc = rms_norm(cx, norm_mix[i]) * (1.0 + csc1) + csh1
        if kind == 0:
            yx, yc = window_mixer(hx, hc, win_w_qkv[j], win_w_o[j], win_q_gain[j], win_k_gain[j], win_sinks[j],
                                  with_ctx)
        elif kind == 1:
            yx, yc = full_mixer(hx, hc, full_w_qkv[j], full_w_o[j], full_q_gain[j], full_k_gain[j], with_ctx)
        elif kind == 2:
            yx, yc = hyena_mixer(hx, hc, hy_w_in[j], hy_b_in[j], hy_w_short[j], hy_b_short[j], hy_filt_w1[j],
                                 hy_filt_b1[j], hy_filt_w2[j], hy_filt_b2[j], hy_filt_w3[j], hy_filt_b3[j],
                                 hy_sin_freq[j], hy_decay[j], hy_skip[j], hy_w_out[j], hy_b_out[j], with_ctx)
        else:
            yx, yc = s5_mixer(hx, hc, s5_lam_re[j], s5_lam_im[j], s5_log_step[j], s5_b_re[j], s5_b_im[j],
                              s5_c_re[j], s5_c_im[j], s5_skip[j], s5_w_glu[j], s5_b_glu[j], with_ctx)
        x = x + gt1 * yx
        hx2 = rms_norm(x, norm_ffn[i]) * (1.0 + sc2) + sh2
        if with_ctx:
            cx = cx + cgt1 * yc
            hc2 = rms_norm(cx, norm_ffn[i]) * (1.0 + csc2) + csh2
            tokens = jnp.concatenate([hx2.reshape(-1, D_MODEL), hc2.reshape(-1, D_MODEL)], axis=0)
            out = moe_ffn(tokens, router_w[i], router_b[i], w_gate_up[i], b_gate_up[i], w_down[i], b_down[i])
            n_lat = x.shape[0] * x.shape[1]
            x = x + gt2 * out[:n_lat].reshape(x.shape)
            cx = cx + cgt2 * out[n_lat:].reshape(cx.shape)
        else:
            out = moe_ffn(hx2.reshape(-1, D_MODEL), router_w[i], router_b[i], w_gate_up[i], b_gate_up[i],
                          w_down[i], b_down[i])
            x = x + gt2 * out.reshape(x.shape)
    return x
```

```python
import functools
import math

import numpy as np
import jax
import jax.numpy as jnp
from jax import lax
from jax.experimental import pallas as pl
from jax.experimental.pallas import tpu as pltpu

F32, BF16, I32 = jnp.float32, jnp.bfloat16, jnp.int32
SDS = jax.ShapeDtypeStruct
BS = pl.BlockSpec

HEAD_DIM = 128
N_KV_HEADS = 4
GROUP = 4
GRID_W = 64
WINDOW = 128
ROPE_BASE = 10000.0
TOP_K = 4
SWIGLU_LIMIT = 7.0
SWIGLU_ALPHA = 1.702
EPS = 1e-6
NEG_INF = -1e30

LANES = 128
TM = 512
TQ = 256
TK = 512
MOE_BLK = 256
TOK_DMA = 256
VMEM_LIMIT = 56 * 1024 * 1024


def _cp(*sem):
    return pltpu.CompilerParams(dimension_semantics=sem, vmem_limit_bytes=VMEM_LIMIT)


def _norm_mod(x, gain, scale, shift):
    ms = jnp.mean(x * x, axis=-1, keepdims=True)
    return (x * lax.rsqrt(ms + EPS)) * gain * (1.0 + scale) + shift


def _split3(a):
    a0 = a.astype(BF16)
    r1 = a - a0.astype(F32)
    a1 = r1.astype(BF16)
    a2 = (r1 - a1.astype(F32)).astype(BF16)
    return a0, a1, a2


def _dot_f32(a, b):
    a0, a1, a2 = _split3(a)
    b0, b1, b2 = _split3(b)
    d = lambda u, v: jnp.dot(u, v, preferred_element_type=F32)
    return (d(a0, b0) + (d(a0, b1) + d(a1, b0)) + (d(a0, b2) + d(a1, b1) + d(a2, b0)))


def _mod_kernel(c_ref, w_ref, b_ref, o_ref):
    cc = c_ref[...]
    s = cc * (1.0 / (1.0 + jnp.exp(-cc)))
    o_ref[...] = jnp.dot(s.astype(BF16), w_ref[...].astype(BF16), preferred_element_type=F32) + b_ref[...]


def _modulation(cc, w_mod, b_mod):
    depth, d, d6 = w_mod.shape
    tn = 1024
    return pl.pallas_call(
        _mod_kernel,
        out_shape=SDS((depth, 8, d6), F32),
        grid=(depth, d6 // tn),
        in_specs=[BS((8, d), lambda l, j: (0, 0)),
                  BS((None, d, tn), lambda l, j: (l, 0, j)),
                  BS((None, 1, tn), lambda l, j: (l, 0, j))],
        out_specs=BS((None, 8, tn), lambda l, j: (l, 0, j)),
        compiler_params=_cp("parallel", "parallel"),
        name="modulation",
    )(cc, w_mod, b_mod.reshape(depth, 1, d6))


def _mod_spec(n_lat_tiles, d, tn, chunk):
    per = d // tn
    return BS((None, 1, tn), lambda i, j: (jnp.where(i < n_lat_tiles, 0, 1), 0, chunk * per + j))


def _mod_spec_full(n_lat_tiles, d, chunk):
    return BS((None, 1, d), lambda i, j: (jnp.where(i < n_lat_tiles, 0, 1), 0, chunk))


def _qkv_kernel(x_ref, gain_ref, sc_ref, sh_ref, w_ref, qg_ref, kg_ref, cos_ref, sin_ref, o_ref, h_scr,
                *, n_lat_tiles):
    i = pl.program_id(0)
    j = pl.program_id(1)

    @pl.when(j == 0)
    def _():
        h_scr[...] = _norm_mod(x_ref[...], gain_ref[...], sc_ref[...], sh_ref[...]).astype(BF16)

    acc = jnp.dot(h_scr[...], w_ref[...], preferred_element_type=F32)

    @pl.when(j < 5)
    def _():
        gain = jnp.where(j < 4, qg_ref[...], kg_ref[...])
        post = jnp.where(j < 4, HEAD_DIM ** -0.5, 1.0)
        is_lat = i < n_lat_tiles
        for hh in range(4):
            t = acc[:, hh * HEAD_DIM:(hh + 1) * HEAD_DIM]
            t = t * lax.rsqrt(jnp.mean(t * t, axis=-1, keepdims=True) + EPS) * gain
            rot = pltpu.roll(t, HEAD_DIM // 2, axis=1)
            tr = t * cos_ref[...] + rot * sin_ref[...]
            t = jnp.where(is_lat, tr, t) * post
            o_ref[:, hh * HEAD_DIM:(hh + 1) * HEAD_DIM] = t.astype(BF16)

    @pl.when(j == 5)
    def _():
        o_ref[...] = acc.astype(BF16)


def _qkv(x, gain, mod3, w_bf, q_gain, k_gain, cos2, sin2, n_lat):
    np_, d = x.shape
    nt, nlt = np_ // TM, n_lat // TM
    width = w_bf.shape[1]
    tn = 4 * HEAD_DIM
    return pl.pallas_call(
        functools.partial(_qkv_kernel, n_lat_tiles=nlt),
        out_shape=SDS((np_, width), BF16),
        grid=(nt, width // tn),
        in_specs=[BS((TM, d), lambda i, j: (i, 0)),
                  BS((1, d), lambda i, j: (0, 0)),
                  _mod_spec_full(nlt, d, 1),
                  _mod_spec_full(nlt, d, 0),
                  BS((d, tn), lambda i, j: (0, j)),
                  BS((1, HEAD_DIM), lambda i, j: (0, 0)),
                  BS((1, HEAD_DIM), lambda i, j: (0, 0)),
                  BS((TM, HEAD_DIM), lambda i, j: (jnp.minimum(i, nlt - 1), 0)),
                  BS((TM, HEAD_DIM), lambda i, j: (jnp.minimum(i, nlt - 1), 0))],
        out_specs=BS((TM, tn), lambda i, j: (i, j)),
        scratch_shapes=[pltpu.VMEM((TM, d), BF16)],
        compiler_params=_cp("parallel", "arbitrary"),
        name="qkv_proj",
    )(x, gain.reshape(1, d), mod3, mod3, w_bf, q_gain.reshape(1, HEAD_DIM), k_gain.reshape(1, HEAD_DIM),
      cos2, sin2)


def _rope_tables(n_lat):
    pairs = HEAD_DIM // 4
    t = jnp.arange(n_lat, dtype=I32)
    row = (t // GRID_W).astype(F32)
    col = (t % GRID_W).astype(F32)
    freqs = ROPE_BASE ** (-jnp.arange(pairs, dtype=F32) / pairs)
    ang = jnp.concatenate([row[:, None] * freqs, col[:, None] * freqs], axis=-1)
    cos, sin = jnp.cos(ang), jnp.sin(ang)
    return jnp.concatenate([cos, cos], axis=-1), jnp.concatenate([-sin, sin], axis=-1)


def _attn_kernel(sink_ref, q_ref, k_ref, v_ref, o_ref, m_scr, l_scr, acc_scr,
                 *, windowed, has_sink, n_lat, n_ctx, n_ctx_pad):
    kvh = pl.program_id(0)
    i = pl.program_id(1)
    is_lat = i < n_lat // TQ
    rows = GROUP * TQ
    qs = jnp.concatenate([q_ref[:, h * HEAD_DIM:(h + 1) * HEAD_DIM] for h in range(GROUP)], axis=0)

    m_scr[...] = jnp.full((rows, 1), NEG_INF, F32)
    l_scr[...] = jnp.zeros((rows, 1), F32)
    acc_scr[...] = jnp.zeros((rows, HEAD_DIM), F32)

    def step(start, size, mask):
        k = k_ref[pl.ds(start, size), :]
        v = v_ref[pl.ds(start, size), :]
        s = lax.dot_general(qs, k, (((1,), (1,)), ((), ())), preferred_element_type=F32)
        if mask is not None:
            s = jnp.where(mask, s, NEG_INF)
        m_old = m_scr[...]
        m_new = jnp.maximum(m_old, jnp.max(s, axis=-1, keepdims=True))
        alpha = jnp.exp(m_old - m_new)
        p = jnp.exp(s - m_new)
        l_scr[...] = alpha * l_scr[...] + jnp.sum(p, axis=-1, keepdims=True)
        acc_scr[...] = alpha * acc_scr[...] + jnp.dot(p.astype(BF16), v, preferred_element_type=F32)
        m_scr[...] = m_new

    if windowed:
        wk = TQ + 2 * WINDOW

        @pl.when(is_lat)
        def _():
            start = pl.multiple_of(jnp.clip(i * TQ - WINDOW, 0, n_lat - wk), WINDOW)
            q_pos = i * TQ + lax.broadcasted_iota(I32, (rows, wk), 0) % TQ
            k_pos = start + lax.broadcasted_iota(I32, (rows, wk), 1)
            step(start, wk, jnp.abs(q_pos - k_pos) <= WINDOW)
    else:
        def body(c, carry):
            step(pl.multiple_of(c * TK, TK), TK, None)
            return carry

        lax.fori_loop(0, jnp.where(is_lat, n_lat // TK, 0), body, 0)

    ctx_mask = None
    if n_ctx != n_ctx_pad:
        ctx_mask = lax.broadcasted_iota(I32, (rows, n_ctx_pad), 1) < n_ctx
    step(n_lat, n_ctx_pad, ctx_mask)

    m = m_scr[...]
    l = l_scr[...]
    acc = acc_scr[...]
    if has_sink:
        sink = jnp.concatenate([jnp.full((TQ, 1), sink_ref[kvh * GROUP + h], F32) for h in range(GROUP)], axis=0)
        m_f = jnp.maximum(m, sink)
        scale = jnp.exp(m - m_f)
        l = l * scale + jnp.exp(sink - m_f)
        acc = acc * scale
    out = acc * (1.0 / l)
    for h in range(GROUP):
        o_ref[:, h * HEAD_DIM:(h + 1) * HEAD_DIM] = out[h * TQ:(h + 1) * TQ].astype(BF16)


def _attention(qkv, sinks, n_lat, n_ctx, windowed):
    np_ = qkv.shape[0]
    n_heads = N_KV_HEADS * GROUP
    has_sink = sinks is not None
    if sinks is None:
        sinks = jnp.zeros((n_heads,), F32)
    rows = GROUP * TQ
    kern = functools.partial(_attn_kernel, windowed=windowed, has_sink=has_sink, n_lat=n_lat, n_ctx=n_ctx,
                             n_ctx_pad=np_ - n_lat)
    return pl.pallas_call(
        kern,
        out_shape=SDS((np_, n_heads * HEAD_DIM), BF16),
        grid_spec=pltpu.PrefetchScalarGridSpec(
            num_scalar_prefetch=1,
            grid=(N_KV_HEADS, np_ // TQ),
            in_specs=[BS((TQ, GROUP * HEAD_DIM), lambda kv, i, s: (i, kv)),
                      BS((np_, HEAD_DIM), lambda kv, i, s: (0, n_heads + kv)),
                      BS((np_, HEAD_DIM), lambda kv, i, s: (0, n_heads + N_KV_HEADS + kv))],
            out_specs=BS((TQ, GROUP * HEAD_DIM), lambda kv, i, s: (i, kv)),
            scratch_shapes=[pltpu.VMEM((rows, 1), F32), pltpu.VMEM((rows, 1), F32),
                            pltpu.VMEM((rows, HEAD_DIM), F32)]),
        compiler_params=_cp("parallel", "arbitrary"),
        name="window_attention" if windowed else "full_attention",
    )(sinks.astype(F32), qkv, qkv, qkv)


def _proj_res_kernel(a_ref, w_ref, b_ref, x_ref, gt_ref, o_ref):
    y = jnp.dot(a_ref[...], w_ref[...], preferred_element_type=F32) + b_ref[...]
    o_ref[...] = x_ref[...] + gt_ref[...] * y


def _proj_res(a, w_bf, bias, x, mod3, n_lat, gate_chunk):
    np_, d = x.shape
    kdim = a.shape[1]
    tn = 1024
    nlt = n_lat // TM
    return pl.pallas_call(
        _proj_res_kernel,
        out_shape=SDS((np_, d), F32),
        grid=(np_ // TM, d // tn),
        in_specs=[BS((TM, kdim), lambda i, j: (i, 0)),
                  BS((kdim, tn), lambda i, j: (0, j)),
                  BS((1, tn), lambda i, j: (0, j)),
                  BS((TM, tn), lambda i, j: (i, j)),
                  _mod_spec(nlt, d, tn, gate_chunk)],
        out_specs=BS((TM, tn), lambda i, j: (i, j)),
        compiler_params=_cp("parallel", "parallel"),
        name="proj_residual",
    )(a, w_bf, bias.reshape(1, d), x, mod3)


def _router_kernel(x_ref, gain_ref, sc_ref, sh_ref, wr_ref, br_ref, h_ref, idx_ref, gate_ref):
    h = _norm_mod(x_ref[...], gain_ref[...], sc_ref[...], sh_ref[...])
    h_ref[...] = h
    vals = _dot_f32(h, wr_ref[...]) + br_ref[...]
    lane = lax.broadcasted_iota(I32, vals.shape, 1).astype(F32)
    tops, idxs = [], []
    for _ in range(TOP_K):
        m = jnp.max(vals, axis=-1, keepdims=True)
        sel = jnp.min(jnp.where(vals == m, lane, float(LANES)), axis=-1, keepdims=True)
        tops.append(m)
        idxs.append(sel)
        vals = jnp.where(lane == sel, 2.0 * NEG_INF, vals)
    es = [jnp.exp(t - tops[0]) for t in tops]
    inv = 1.0 / (es[0] + es[1] + es[2] + es[3])
    idx_out = jnp.zeros(vals.shape, F32)
    gate_out = jnp.zeros(vals.shape, F32)
    for k in range(TOP_K):
        idx_out = jnp.where(lane == float(k), idxs[k], idx_out)
        gate_out = jnp.where(lane == float(k), es[k] * inv, gate_out)
    idx_ref[...] = idx_out.astype(I32)
    gate_ref[...] = gate_out


def _router(x, gain, mod3, wr_pad, br_pad, n_lat):
    np_, d = x.shape
    nlt = n_lat // TM
    return pl.pallas_call(
        _router_kernel,
        out_shape=(SDS((np_, d), F32), SDS((np_, LANES), I32), SDS((np_, LANES), F32)),
        grid=(np_ // TM, 1),
        in_specs=[BS((TM, d), lambda i, j: (i, 0)),
                  BS((1, d), lambda i, j: (0, 0)),
                  _mod_spec_full(nlt, d, 4),
                  _mod_spec_full(nlt, d, 3),
                  BS((d, LANES), lambda i, j: (0, 0)),
                  BS((1, LANES), lambda i, j: (0, 0))],
        out_specs=(BS((TM, d), lambda i, j: (i, 0)),
                   BS((TM, LANES), lambda i, j: (i, 0)),
                   BS((TM, LANES), lambda i, j: (i, 0))),
        compiler_params=_cp("parallel", "arbitrary"),
        name="router_top4",
    )(x, gain.reshape(1, d), mod3, mod3, wr_pad, br_pad)


def _rank_kernel(idx_ref, rank_ref, cnt_ref, base_scr, *, n_tok):
    i = pl.program_id(0)

    @pl.when(i == 0)
    def _():
        base_scr[...] = jnp.zeros(base_scr.shape, F32)

    idx = idx_ref[...]
    lane = lax.broadcasted_iota(I32, idx.shape, 1)
    row = i * TM + lax.broadcasted_iota(I32, idx.shape, 0)
    valid = row < n_tok
    hots = [jnp.where(valid & (lane == idx[:, k:k + 1]), 1.0, 0.0) for k in range(TOP_K)]
    cnt = hots[0] + hots[1] + hots[2] + hots[3]
    tri = jnp.where(lax.broadcasted_iota(I32, (TM, TM), 0) > lax.broadcasted_iota(I32, (TM, TM), 1), 1.0, 0.0)
    before = base_scr[...] + jnp.dot(tri.astype(BF16), cnt.astype(BF16), preferred_element_type=F32)
    out = jnp.zeros(idx.shape, F32)
    for k in range(TOP_K):
        out = jnp.where(lane == k, jnp.sum(hots[k] * before, axis=-1, keepdims=True), out)
    rank_ref[...] = out.astype(I32)
    base_scr[...] = base_scr[...] + jnp.sum(cnt, axis=0, keepdims=True)
    cnt_ref[...] = base_scr[...].astype(I32)


def _expert_ranks(idx, n_tok):
    np_ = idx.shape[0]
    return pl.pallas_call(
        functools.partial(_rank_kernel, n_tok=n_tok),
        out_shape=(SDS((np_, LANES), I32), SDS((1, LANES), I32)),
        grid=(np_ // TM,),
        in_specs=[BS((TM, LANES), lambda i: (i, 0))],
        out_specs=(BS((TM, LANES), lambda i: (i, 0)), BS((1, LANES), lambda i: (0, 0))),
        scratch_shapes=[pltpu.VMEM((1, LANES), F32)],
        compiler_params=_cp("arbitrary"),
        name="expert_ranks",
    )(idx)


def _row_copy(src, dst, sem):
    return pltpu.make_async_copy(src, dst, sem)


def _dispatch_kernel(dest_ref, h_hbm, xs_init, xs_hbm, sem):
    del xs_init
    i = pl.program_id(0)

    def issue(t, carry):
        src = h_hbm.at[pl.ds(i * TOK_DMA + t, 1)]
        for k in range(TOP_K):
            _row_copy(src, xs_hbm.at[pl.ds(dest_ref[t * TOP_K + k], 1)], sem).start()
        return carry

    lax.fori_loop(0, TOK_DMA, issue, 0)

    def drain(t, carry):
        for k in range(TOP_K):
            _row_copy(h_hbm.at[pl.ds(0, 1)], xs_hbm.at[pl.ds(0, 1)], sem).wait()
        return carry

    lax.fori_loop(0, TOK_DMA, drain, 0)


def _dispatch(dest_flat, h, n_tok, n_rows):
    d = h.shape[1]
    xs0 = jnp.zeros((n_rows, d), F32)
    return pl.pallas_call(
        _dispatch_kernel,
        out_shape=SDS((n_rows, d), F32),
        grid=(n_tok // TOK_DMA,),
        in_specs=[BS((TOK_DMA * TOP_K,), lambda i: (i,), memory_space=pltpu.SMEM),
                  BS(memory_space=pl.ANY),
                  BS(memory_space=pl.ANY)],
        out_specs=BS(memory_space=pl.ANY),
        scratch_shapes=[pltpu.SemaphoreType.DMA(())],
        input_output_aliases={2: 0},
        compiler_params=pltpu.CompilerParams(dimension_semantics=("arbitrary",), has_side_effects=True),
        name="moe_dispatch",
    )(dest_flat, h, xs0)


def _ffn_kernel(be_ref, nu_ref, x_ref, wgu_ref, bgu_ref, wdn_ref, bdn_ref, o_ref, *, ff):
    b = pl.program_id(0)

    @pl.when(b < nu_ref[0])
    def _():
        gu = jnp.dot(x_ref[...].astype(BF16), wgu_ref[...], preferred_element_type=F32) + bgu_ref[...]
        glu = jnp.minimum(gu[:, :ff], SWIGLU_LIMIT)
        lin = jnp.clip(gu[:, ff:], -SWIGLU_LIMIT, SWIGLU_LIMIT)
        act = glu * (1.0 / (1.0 + jnp.exp(-SWIGLU_ALPHA * glu))) * (lin + 1.0)
        o_ref[...] = jnp.dot(act.astype(BF16), wdn_ref[...], preferred_element_type=F32) + bdn_ref[...]

    @pl.when(b >= nu_ref[0])
    def _():
        o_ref[...] = jnp.zeros(o_ref.shape, F32)


def _expert_ffn(blk_expert, n_used, xs, wgu_bf, bgu, wdn_bf, bdn):
    n_rows, d = xs.shape
    n_exp, _, ff2 = wgu_bf.shape
    ff = ff2 // 2
    return pl.pallas_call(
        functools.partial(_ffn_kernel, ff=ff),
        out_shape=SDS((n_rows, d), F32),
        grid_spec=pltpu.PrefetchScalarGridSpec(
            num_scalar_prefetch=2,
            grid=(n_rows // MOE_BLK,),
            in_specs=[BS((MOE_BLK, d), lambda b, be, nu: (b, 0)),
                      BS((None, d, ff2), lambda b, be, nu: (be[b], 0, 0)),
                      BS((None, 1, ff2), lambda b, be, nu: (be[b], 0, 0)),
                      BS((None, ff, d), lambda b, be, nu: (be[b], 0, 0)),
                      BS((None, 1, d), lambda b, be, nu: (be[b], 0, 0))],
            out_specs=BS((MOE_BLK, d), lambda b, be, nu: (b, 0))),
        compiler_params=_cp("arbitrary"),
        name="expert_ffn",
    )(blk_expert, n_used, xs, wgu_bf, bgu.reshape(n_exp, 1, ff2), wdn_bf, bdn.reshape(n_exp, 1, d))


def _combine_kernel(dest_ref, g_ref, x_ref, gt_ref, y_hbm, o_ref, buf, sem):
    def issue(t, carry):
        for k in range(TOP_K):
            _row_copy(y_hbm.at[pl.ds(dest_ref[t * TOP_K + k], 1)], buf.at[k, pl.ds(t, 1)], sem).start()
        return carry

    lax.fori_loop(0, TOK_DMA, issue, 0)

    def drain(t, carry):
        for k in range(TOP_K):
            _row_copy(y_hbm.at[pl.ds(0, 1)], buf.at[k, pl.ds(0, 1)], sem).wait()
        return carry

    lax.fori_loop(0, TOK_DMA, drain, 0)
    g = g_ref[...]
    acc = g[:, 0:1] * buf[0]
    for k in range(1, TOP_K):
        acc = acc + g[:, k:k + 1] * buf[k]
    o_ref[...] = x_ref[...] + gt_ref[...] * acc


def _combine(dest_flat, gates, x, mod3, y, n_lat):
    np_, d = x.shape
    nlt = n_lat // TOK_DMA
    return pl.pallas_call(
        _combine_kernel,
        out_shape=SDS((np_, d), F32),
        grid=(np_ // TOK_DMA,),
        in_specs=[BS((TOK_DMA * TOP_K,), lambda i: (i,), memory_space=pltpu.SMEM),
                  BS((TOK_DMA, LANES), lambda i: (i, 0)),
                  BS((TOK_DMA, d), lambda i: (i, 0)),
                  BS((None, 1, d), lambda i: (jnp.where(i < nlt, 0, 1), 0, 5)),
                  BS(memory_space=pl.ANY)],
        out_specs=BS((TOK_DMA, d), lambda i: (i, 0)),
        scratch_shapes=[pltpu.VMEM((TOP_K, TOK_DMA, d), F32), pltpu.SemaphoreType.DMA(())],
        compiler_params=_cp("arbitrary"),
        name="moe_combine",
    )(dest_flat, gates, x, mod3, y)


def _moe(x, gain, mod3, router_w, router_b, wgu_bf, bgu, wdn_bf, bdn, n_lat, n_tok):
    np_, d = x.shape
    n_exp = router_w.shape[1]
    wr_pad = jnp.zeros((d, LANES), F32).at[:, :n_exp].set(router_w)
    br_pad = jnp.full((1, LANES), NEG_INF, F32).at[0, :n_exp].set(router_b)
    h, idx, gates = _router(x, gain, mod3, wr_pad, br_pad, n_lat)
    rank, counts = _expert_ranks(idx, n_tok)
    counts = counts[0, :n_exp]
    padded = (counts + MOE_BLK - 1) // MOE_BLK * MOE_BLK
    pad_end = jnp.cumsum(padded)
    pad_start = pad_end - padded
    dest = pad_start[idx[:, :TOP_K]] + rank[:, :TOP_K]
    dest = jnp.where(jnp.arange(np_)[:, None] < n_tok, dest, 0).astype(I32).reshape(-1)
    n_blocks = -(-(n_tok * TOP_K + n_exp * (MOE_BLK - 1)) // MOE_BLK)
    blk_expert = jnp.minimum(jnp.searchsorted(pad_end, jnp.arange(n_blocks, dtype=I32) * MOE_BLK, side="right"),
                             n_exp - 1).astype(I32)
    n_used = (pad_end[-1:] // MOE_BLK).astype(I32)
    xs = _dispatch(dest, h, n_tok, n_blocks * MOE_BLK)
    y = _expert_ffn(blk_expert, n_used, xs, wgu_bf, bgu, wdn_bf, bdn)
    return _combine(dest, gates, x, mod3, y, n_lat)


def _attn_layer(x, mod3, norm_gain, w_qkv, w_o, q_gain, k_gain, sinks, cos2, sin2, n_lat, n_ctx, windowed):
    d = x.shape[1]
    qkv = _qkv(x, norm_gain, mod3, w_qkv.astype(BF16), q_gain, k_gain, cos2, sin2, n_lat)
    o = _attention(qkv, sinks, n_lat, n_ctx, windowed)
    return _proj_res(o, w_o.astype(BF16), jnp.zeros((d,), F32), x, mod3, n_lat, 2)


def _linear_norm_kernel(x_ref, gain_ref, sc_ref, sh_ref, w_ref, b_ref, o_ref, h_scr):
    @pl.when(pl.program_id(1) == 0)
    def _():
        h_scr[...] = _norm_mod(x_ref[...], gain_ref[...], sc_ref[...], sh_ref[...]).astype(BF16)

    o_ref[...] = (jnp.dot(h_scr[...], w_ref[...], preferred_element_type=F32) + b_ref[...]).astype(o_ref.dtype)


def _linear_norm(x, gain, mod3, w_bf, bias, n_lat, out_dtype):
    np_, d = x.shape
    width = w_bf.shape[1]
    tn = 1024
    nlt = n_lat // TM
    return pl.pallas_call(
        _linear_norm_kernel,
        out_shape=SDS((np_, width), out_dtype),
        grid=(np_ // TM, width // tn),
        in_specs=[BS((TM, d), lambda i, j: (i, 0)),
                  BS((1, d), lambda i, j: (0, 0)),
                  _mod_spec_full(nlt, d, 1),
                  _mod_spec_full(nlt, d, 0),
                  BS((d, tn), lambda i, j: (0, j)),
                  BS((1, tn), lambda i, j: (0, j))],
        out_specs=BS((TM, tn), lambda i, j: (i, j)),
        scratch_shapes=[pltpu.VMEM((TM, d), BF16)],
        compiler_params=_cp("parallel", "arbitrary"),
        name="linear_norm",
    )(x, gain.reshape(1, d), mod3, mod3, w_bf, bias.reshape(1, width))


HY_NB = 128
HY_BANDS = 8
HY_MOD_SHIFT = 0.05
HALO = 16


def _short_conv_kernel(z0c, z0p, z0n, z1c, z1p, z1n, z2c, z2p, z2n, w_ref, b_ref, x0_ref, u_ref,
                       *, n_lat_tiles, n_ctx):
    i = pl.program_id(0)
    row = lax.broadcasted_iota(I32, (TM, 1), 0)
    is_ctx = i >= n_lat_tiles
    has_prev = jnp.logical_and(i > 0, i != n_lat_tiles)
    has_next = i < n_lat_tiles - 1
    keep = jnp.logical_or(jnp.logical_not(is_ctx), row < n_ctx)

    def conv(cur, prv, nxt, part):
        zc = jnp.where(keep, cur[...].astype(F32), 0.0)
        first = jnp.where(has_prev, prv[HALO - 1:HALO, :].astype(F32), 0.0)
        last = jnp.where(has_next, nxt[0:1, :].astype(F32), 0.0)
        up = jnp.where(row == 0, first, pltpu.roll(zc, 1, axis=0))
        dn = jnp.where(row == TM - 1, last, pltpu.roll(zc, TM - 1, axis=0))
        w = w_ref[part]
        return up * w[0:1, :] + zc * w[1:2, :] + dn * w[2:3, :] + b_ref[part]

    x0 = conv(z0c, z0p, z0n, 0)
    x1 = conv(z1c, z1p, z1n, 1)
    v = conv(z2c, z2p, z2n, 2)
    x0_ref[...] = x0.astype(BF16)
    u_ref[...] = (x1 * v).astype(BF16)


def _short_conv(z, w_short, b_short, n_lat, n_ctx):
    np_, d3 = z.shape
    d = d3 // 3
    tn = 512
    per = d // tn
    nlt = n_lat // TM
    hb = TM // HALO
    last_halo = np_ // HALO - 1
    specs = []
    for part in range(3):
        specs.append(BS((TM, tn), lambda i, j, part=part: (i, part * per + j)))
        specs.append(BS((HALO, tn), lambda i, j, part=part: (jnp.maximum(i * hb - 1, 0), part * per + j)))
        specs.append(BS((HALO, tn), lambda i, j, part=part: (jnp.minimum((i + 1) * hb, last_halo), part * per + j)))
    w3 = w_short.reshape(3, 3, d).transpose(1, 0, 2)
    b3 = b_short.reshape(3, 1, d)
    return pl.pallas_call(
        functools.partial(_short_conv_kernel, n_lat_tiles=nlt, n_ctx=n_ctx),
        out_shape=(SDS((np_, d), BF16), SDS((np_, d), BF16)),
        grid=(np_ // TM, per),
        in_specs=specs + [BS((3, 3, tn), lambda i, j: (0, 0, j)), BS((3, 1, tn), lambda i, j: (0, 0, j))],
        out_specs=(BS((TM, tn), lambda i, j: (i, j)), BS((TM, tn), lambda i, j: (i, j))),
        compiler_params=_cp("parallel", "parallel"),
        name="hyena_short_conv",
    )(z, z, z, z, z, z, z, z, z, w3, b3)


def _filter_kernel(w1_ref, b1_ref, w2_ref, b2_ref, sf_ref, w3f_ref, b3f_ref, w3b_ref, b3b_ref, dec_ref, o_ref,
                   *, n, rows, s_mul, s_blk):
    blk = pl.program_id(0)
    a = lax.broadcasted_iota(I32, (rows, 1), 0)
    s = s_mul * a + s_blk * blk
    lag = jnp.where(s < n, s, 2 * n - s)
    t = lag.astype(F32) / float(n)
    lane = lax.broadcasted_iota(I32, (rows, LANES), 1)
    band = jnp.where(lane <= HY_BANDS, lane, lane - HY_BANDS).astype(F32)
    ang = 2.0 * jnp.pi * t * band
    feats = jnp.where(lane == 0, t,
                      jnp.where(lane <= HY_BANDS, jnp.cos(ang),
                                jnp.where(lane <= 2 * HY_BANDS, -jnp.sin(ang), 0.0)))
    hid = jnp.sin(sf_ref[0:1, :] * (_dot_f32(feats, w1_ref[...]) + b1_ref[...]))
    hid = jnp.sin(sf_ref[1:2, :] * (_dot_f32(hid, w2_ref[...]) + b2_ref[...]))
    fwd = _dot_f32(hid, w3f_ref[...]) + b3f_ref[...]
    bwd = _dot_f32(hid, w3b_ref[...]) + b3b_ref[...]
    window = jnp.exp(-t * jnp.abs(dec_ref[...])) + HY_MOD_SHIFT
    g = jnp.where(s < n, fwd, jnp.where(s == n, 0.0, bwd)) * window
    o_ref[...] = g.astype(o_ref.dtype)


def _hyena_filter(n, fw1, fb1, fw2, fb2, fw3, fb3, sin_freq, decay, two_d, out_dtype):
    width = fw1.shape[1]
    d = decay.shape[0]
    tn = 1024
    per = d // tn
    pad = lambda m, r, c: jnp.zeros((r, c), F32).at[:m.shape[0], :m.shape[1]].set(m)
    w1p = pad(fw1, LANES, LANES)
    b1p = pad(fb1.reshape(1, width), 1, LANES)
    w2p = pad(fw2, LANES, LANES)
    b2p = pad(fb2.reshape(1, width), 1, LANES)
    sfp = pad(sin_freq, 2, LANES)
    w3p = pad(fw3, LANES, 2 * d)
    b3 = fb3.reshape(1, 2 * d)
    if two_d:
        rows, nblk, s_mul, s_blk = 2 * n // HY_NB, HY_NB, HY_NB, 1
        out_shape = SDS((rows, HY_NB * d), out_dtype)
    else:
        rows = min(2 * n, 256)
        nblk, s_mul, s_blk = 2 * n // rows, 1, rows
        out_shape = SDS((2 * n, d), out_dtype)
    if two_d:
        out_spec = BS((rows, tn), lambda b, j: (0, b * per + j))
    else:
        out_spec = BS((rows, tn), lambda b, j: (b, j))
    const = lambda r, c: BS((r, c), lambda b, j: (0, 0))
    return pl.pallas_call(
        functools.partial(_filter_kernel, n=n, rows=rows, s_mul=s_mul, s_blk=s_blk),
        out_shape=out_shape,
        grid=(nblk, per),
        in_specs=[const(LANES, LANES), const(1, LANES), const(LANES, LANES), const(1, LANES), const(2, LANES),
                  BS((LANES, tn), lambda b, j: (0, j)), BS((1, tn), lambda b, j: (0, j)),
                  BS((LANES, tn), lambda b, j: (0, per + j)), BS((1, tn), lambda b, j: (0, per + j)),
                  BS((1, tn), lambda b, j: (0, j))],
        out_specs=out_spec,
        compiler_params=_cp("parallel", "parallel"),
        name="hyena_filter",
    )(w1p, b1p, w2p, b2p, sfp, w3p, b3, w3p, b3, decay.reshape(1, d))


def _dft_tables(n_total):
    nb = HY_NB
    na = n_total // nb
    ka = np.arange(na)[:, None]
    a = np.arange(na)[None, :]
    ang_a = 2.0 * np.pi * ((ka * a) % na) / na
    fa = np.concatenate([np.cos(ang_a), -np.sin(ang_a)], axis=0)
    kb = np.arange(nb)[:, None]
    b = np.arange(nb)[None, :]
    ang_b = 2.0 * np.pi * ((kb * b) % nb) / nb
    c, s = np.cos(ang_b), np.sin(ang_b)
    fwd_b = np.block([[c, s], [-s, c]])
    inv_b = np.block([[c, -s], [s, c]]) / float(n_total)
    ang_t = 2.0 * np.pi * ((np.arange(nb)[:, None] * np.arange(na)[None, :]) % n_total) / n_total
    tw = (np.cos(ang_t), -np.sin(ang_t))
    inv_a = np.concatenate([np.cos(ang_a[:na // 2]), -np.sin(ang_a[:na // 2])], axis=1)
    f = lambda m: jnp.asarray(m, F32)
    return dict(na=na, fa=f(fa), fwd_b=f(fwd_b), inv_b=f(inv_b), tw_re=f(tw[0]), tw_im=f(tw[1]), inv_a=f(inv_a))


def _dft_a_kernel(f_ref, x_ref, twr_ref, twi_ref, re_ref, im_ref, *, na):
    pq = jnp.dot(f_ref[...], x_ref[...], preferred_element_type=F32)
    p, q = pq[:na], pq[na:]
    twr, twi = twr_ref[...], twi_ref[...]
    re_ref[...] = (p * twr - q * twi).astype(BF16)
    im_ref[...] = (p * twi + q * twr).astype(BF16)


def _dft_a(x2d, f_bf, tw_re, tw_im, d):
    ka_rows, width = x2d.shape
    na = f_bf.shape[0] // 2
    tn = min(d, 2048)
    per = d // tn
    twr = tw_re.reshape(HY_NB, na, 1)
    twi = tw_im.reshape(HY_NB, na, 1)
    return pl.pallas_call(
        functools.partial(_dft_a_kernel, na=na),
        out_shape=(SDS((na, width), BF16), SDS((na, width), BF16)),
        grid=(width // tn,),
        in_specs=[BS((2 * na, ka_rows), lambda l: (0, 0)),
                  BS((ka_rows, tn), lambda l: (0, l)),
                  BS((None, na, 1), lambda l: (l // per, 0, 0)),
                  BS((None, na, 1), lambda l: (l // per, 0, 0))],
        out_specs=(BS((na, tn), lambda l: (0, l)), BS((na, tn), lambda l: (0, l))),
        compiler_params=_cp("parallel"),
        name="dft_stage_a",
    )(f_bf, x2d, twr, twi)


def _dft_b_kernel(fwd_ref, inv_ref, ur_ref, ui_ref, gr_ref, gi_ref, twr_ref, twi_ref, br_ref, bi_ref):
    nb = HY_NB
    fwd = fwd_ref[...]
    us = jnp.dot(fwd, jnp.concatenate([ur_ref[...], ui_ref[...]], axis=0), preferred_element_type=F32)
    gs = jnp.dot(fwd, jnp.concatenate([gr_ref[...], gi_ref[...]], axis=0), preferred_element_type=F32)
    ure, uim, gre, gim = us[:nb], us[nb:], gs[:nb], gs[nb:]
    yre = ure * gre - uim * gim
    yim = ure * gim + uim * gre
    ys = jnp.concatenate([yre, yim], axis=0).astype(BF16)
    bs = jnp.dot(inv_ref[...], ys, preferred_element_type=F32)
    bre, bim = bs[:nb], bs[nb:]
    twr, twi = twr_ref[...], twi_ref[...]
    br_ref[...] = (bre * twr + bim * twi).astype(BF16)
    bi_ref[...] = (bim * twr - bre * twi).astype(BF16)


def _dft_b(ur, ui, gr, gi, fwd_bf, inv_bf, tw_re, tw_im, na):
    rows, d = ur.shape
    tn = min(d, 2048)
    nb = HY_NB
    twr = tw_re.T.reshape(na, nb, 1)
    twi = tw_im.T.reshape(na, nb, 1)
    data = BS((nb, tn), lambda k, j: (k, j))
    tws = BS((None, nb, 1), lambda k, j: (k, 0, 0))
    return pl.pallas_call(
        _dft_b_kernel,
        out_shape=(SDS((rows, d), BF16), SDS((rows, d), BF16)),
        grid=(na, d // tn),
        in_specs=[BS((2 * nb, 2 * nb), lambda k, j: (0, 0)), BS((2 * nb, 2 * nb), lambda k, j: (0, 0)),
                  data, data, data, data, tws, tws],
        out_specs=(data, data),
        compiler_params=_cp("parallel", "parallel"),
        name="dft_stage_b",
    )(fwd_bf, inv_bf, ur, ui, gr, gi, twr, twi)


def _idft_a_kernel(f_ref, br_ref, bi_ref, u_ref, skip_ref, o_ref):
    bs = jnp.concatenate([br_ref[...], bi_ref[...]], axis=0)
    y = jnp.dot(f_ref[...], bs, preferred_element_type=F32)
    o_ref[...] = y + u_ref[...].astype(F32) * skip_ref[...]


def _idft_a(br2d, bi2d, u2d, skip, inv_a_bf, d):
    na, width = br2d.shape
    tn = min(d, 2048)
    per = d // tn
    return pl.pallas_call(
        _idft_a_kernel,
        out_shape=SDS((na // 2, width), F32),
        grid=(width // tn,),
        in_specs=[BS((na // 2, 2 * na), lambda l: (0, 0)),
                  BS((na, tn), lambda l: (0, l)),
                  BS((na, tn), lambda l: (0, l)),
                  BS((na // 2, tn), lambda l: (0, l)),
                  BS((1, tn), lambda l: (0, l % per))],
        out_specs=BS((na // 2, tn), lambda l: (0, l)),
        compiler_params=_cp("parallel"),
        name="idft_stage_a",
    )(inv_a_bf, br2d, bi2d, u2d, skip.reshape(1, d))


def _ctx_conv_kernel(g_ref, u_ref, skip_ref, o_ref, gg_scr, u_scr, *, n):
    g = g_ref[...]
    gg_scr[0:2 * n, :] = g
    gg_scr[2 * n:4 * n, :] = g
    u_scr[...] = u_ref[...].astype(F32)

    def body(s, acc):
        return acc + gg_scr[pl.ds(2 * n - s, n), :] * u_scr[pl.ds(s, 1), :]

    acc = lax.fori_loop(0, n, body, jnp.zeros((n, LANES), F32))
    o_ref[...] = acc + u_scr[...] * skip_ref[...]


def _ctx_conv(g, u, skip):
    n2, d = g.shape
    n = n2 // 2
    return pl.pallas_call(
        functools.partial(_ctx_conv_kernel, n=n),
        out_shape=SDS((n, d), F32),
        grid=(d // LANES,),
        in_specs=[BS((n2, LANES), lambda j: (0, j)), BS((n, LANES), lambda j: (0, j)),
                  BS((1, LANES), lambda j: (0, j))],
        out_specs=BS((n, LANES), lambda j: (0, j)),
        scratch_shapes=[pltpu.VMEM((2 * n2, LANES), F32), pltpu.VMEM((n, LANES), F32)],
        compiler_params=_cp("parallel"),
        name="hyena_ctx_conv",
    )(g, u, skip.reshape(1, d))


def _gated_proj_res_kernel(a_ref, y_ref, w_ref, b_ref, x_ref, gt_ref, o_ref):
    a = (a_ref[...].astype(F32) * y_ref[...]).astype(BF16)
    y = jnp.dot(a, w_ref[...], preferred_element_type=F32) + b_ref[...]
    o_ref[...] = x_ref[...] + gt_ref[...] * y


def _gated_proj_res(a, y, w_bf, bias, x, mod3, n_lat, gate_chunk):
    np_, d = x.shape
    kdim = a.shape[1]
    tn = 1024
    nlt = n_lat // TM
    return pl.pallas_call(
        _gated_proj_res_kernel,
        out_shape=SDS((np_, d), F32),
        grid=(np_ // TM, d // tn),
        in_specs=[BS((TM, kdim), lambda i, j: (i, 0)),
                  BS((TM, kdim), lambda i, j: (i, 0)),
                  BS((kdim, tn), lambda i, j: (0, j)),
                  BS((1, tn), lambda i, j: (0, j)),
                  BS((TM, tn), lambda i, j: (i, j)),
                  _mod_spec(nlt, d, tn, gate_chunk)],
        out_specs=BS((TM, tn), lambda i, j: (i, j)),
        compiler_params=_cp("parallel", "parallel"),
        name="gated_proj_residual",
    )(a, y, w_bf, bias.reshape(1, d), x, mod3)


def _hyena_layer(x, mod3, norm_gain, w_in, b_in, w_short, b_short, fw1, fb1, fw2, fb2, fw3, fb3, sin_freq, decay,
                 skip, w_out, b_out, n_lat, n_ctx):
    np_, d = x.shape
    z = _linear_norm(x, norm_gain, mod3, w_in.astype(BF16), b_in, n_lat, BF16)
    x0, u = _short_conv(z, w_short, b_short, n_lat, n_ctx)
    tabs = _dft_tables(2 * n_lat)
    na = tabs["na"]
    g2d = _hyena_filter(n_lat, fw1, fb1, fw2, fb2, fw3, fb3, sin_freq, decay, True, BF16)
    u2d = u[:n_lat].reshape(na // 2, HY_NB * d)
    fa_bf = tabs["fa"].astype(BF16)
    ur, ui = _dft_a(u2d, fa_bf[:, :na // 2], tabs["tw_re"], tabs["tw_im"], d)
    gr, gi = _dft_a(g2d, fa_bf, tabs["tw_re"], tabs["tw_im"], d)
    rs = lambda m: m.reshape(na * HY_NB, d)
    br, bi = _dft_b(rs(ur), rs(ui), rs(gr), rs(gi), tabs["fwd_b"].astype(BF16), tabs["inv_b"].astype(BF16),
                    tabs["tw_re"], tabs["tw_im"], na)
    rs2 = lambda m: m.reshape(na, HY_NB * d)
    y_lat = _idft_a(rs2(br), rs2(bi), u2d, skip, tabs["inv_a"].astype(BF16), d).reshape(n_lat, d)
    g_ctx = _hyena_filter(n_ctx, fw1, fb1, fw2, fb2, fw3, fb3, sin_freq, decay, False, F32)
    y_ctx = _ctx_conv(g_ctx, u[n_lat:n_lat + n_ctx], skip)
    y = jnp.concatenate([y_lat, y_ctx, jnp.zeros((np_ - n_lat - n_ctx, d), F32)], axis=0)
    return _gated_proj_res(x0, y, w_out.astype(BF16), b_out, x, mod3, n_lat, 2)


S5_T = 16
S5_GS = 16
S5_P = 64


def _s5_discretize(lr_row, li_row, dt):
    mag = jnp.exp(lr_row * dt)
    a_re, a_im = mag * jnp.cos(li_row * dt), mag * jnp.sin(li_row * dt)
    xr, xi = a_re - 1.0, a_im
    den = 1.0 / (lr_row * lr_row + li_row * li_row)
    return (xr * lr_row + xi * li_row) * den, (xi * lr_row - xr * li_row) * den


def _s5_power_c(lr_col, li_col, dt, cre_t, cim_t, shift):
    width = S5_T * S5_GS
    lane = lax.broadcasted_iota(I32, (S5_GS, width), 1)
    expand = jnp.where(lane % S5_GS == lax.broadcasted_iota(I32, (S5_GS, width), 0), 1.0, 0.0)
    c_re = _dot_f32(cre_t, expand)
    c_im = _dot_f32(cim_t, expand)
    tau = (lax.broadcasted_iota(I32, (1, width), 1) // S5_GS + shift).astype(F32)
    mag = jnp.exp(tau * (lr_col * dt))
    p_re, p_im = mag * jnp.cos(tau * (li_col * dt)), mag * jnp.sin(tau * (li_col * dt))
    z_re = p_re * c_re - p_im * c_im
    z_im = p_re * c_im + p_im * c_re
    return jnp.concatenate([z_re, -z_im], axis=0)


def _s5_chunk_kernel(u_ref, lr_row, li_row, lr_col, li_col, ls_ref, bre_t, bim_t, cre_t, cim_t,
                     y_ref, s_ref, mt_scr):
    width = S5_T * S5_GS
    dt = jnp.exp(ls_ref[...])
    q_re, q_im = _s5_discretize(lr_row[...], li_row[...], dt)
    bb_re = q_re * bre_t[...] - q_im * bim_t[...]
    bb_im = q_re * bim_t[...] + q_im * bre_t[...]
    z = _s5_power_c(lr_col[...], li_col[...], dt, cre_t[...], cim_t[...], 0)
    k0 = _dot_f32(jnp.concatenate([bb_re, bb_im], axis=1), z)
    lane = lax.broadcasted_iota(I32, (S5_GS, width), 1)
    for s in range(S5_T):
        blk = k0 if s == 0 else jnp.where(lane >= s * S5_GS, pltpu.roll(k0, s * S5_GS, axis=1), 0.0)
        mt_scr[s * S5_GS:(s + 1) * S5_GS, :] = blk.astype(BF16)
    u = u_ref[...]
    y_ref[...] = jnp.dot(u, mt_scr[...], preferred_element_type=F32)
    row = lax.broadcasted_iota(I32, (width, S5_GS), 0)
    expand_t = jnp.where(row % S5_GS == lax.broadcasted_iota(I32, (width, S5_GS), 1), 1.0, 0.0)
    be_re = _dot_f32(expand_t, bb_re)
    be_im = _dot_f32(expand_t, bb_im)
    tau = (S5_T - 1 - lax.broadcasted_iota(I32, (width, 1), 0) // S5_GS).astype(F32)
    mag = jnp.exp(tau * (lr_row[...] * dt))
    p_re, p_im = mag * jnp.cos(tau * (li_row[...] * dt)), mag * jnp.sin(tau * (li_row[...] * dt))
    g = jnp.concatenate([p_re * be_re - p_im * be_im, p_re * be_im + p_im * be_re], axis=1)
    s_ref[...] = jnp.dot(u, g.astype(BF16), preferred_element_type=F32)


def _s5_param_specs(groups):
    row = BS((None, None, 1, S5_P), lambda dd, g: (dd, g, 0, 0))
    col = BS((None, None, S5_P, 1), lambda dd, g: (dd, g, 0, 0))
    one = BS((None, None, 1, 1), lambda dd, g: (dd, g, 0, 0))
    bt = BS((None, None, S5_GS, S5_P), lambda dd, g: (dd, g, 0, 0))
    ct = BS((None, None, S5_P, S5_GS), lambda dd, g: (dd, g, 0, 0))
    return row, col, one, bt, ct


def _s5_chunks(ut, prm):
    _, groups, nc, width = ut.shape
    row, col, one, bt, ct = _s5_param_specs(groups)
    data = lambda w: BS((None, None, nc, w), lambda dd, g: (dd, g, 0, 0))
    return pl.pallas_call(
        _s5_chunk_kernel,
        out_shape=(SDS((2, groups, nc, width), F32), SDS((2, groups, nc, 2 * S5_P), F32)),
        grid=(2, groups),
        in_specs=[data(width), row, row, col, col, one, bt, bt, ct, ct],
        out_specs=(data(width), data(2 * S5_P)),
        scratch_shapes=[pltpu.VMEM((width, width), BF16)],
        compiler_params=_cp("parallel", "parallel"),
        name="s5_chunks",
    )(ut, prm["lr_row"], prm["li_row"], prm["lr_col"], prm["li_col"], prm["ls"], prm["bre_t"], prm["bim_t"],
      prm["cre_t"], prm["cim_t"])


def _s5_state_kernel(s_ref, lr_ref, li_ref, ls_ref, h_ref, h_scr, *, steps):
    c0 = pl.program_id(1)

    @pl.when(c0 == 0)
    def _():
        h_scr[...] = jnp.zeros(h_scr.shape, F32)

    dt = jnp.exp(ls_ref[...])
    mag = jnp.exp(float(S5_T) * lr_ref[...] * dt)
    ang = float(S5_T) * li_ref[...] * dt
    a_re, a_im = mag * jnp.cos(ang), mag * jnp.sin(ang)
    lane = lax.broadcasted_iota(I32, a_im.shape, 1)
    a_sw = jnp.where(lane < S5_P, -a_im, a_im)
    h = h_scr[...]
    for c in range(steps):
        h_ref[c] = h
        h = a_re * h + a_sw * pltpu.roll(h, S5_P, axis=1) + s_ref[c]
    h_scr[...] = h


def _s5_states(s_t, prm):
    _, nc, groups, w = s_t.shape
    steps = 8
    data = BS((None, steps, groups, w), lambda dd, c: (dd, c, 0, 0))
    par = BS((None, groups, w), lambda dd, c: (dd, 0, 0))
    return pl.pallas_call(
        functools.partial(_s5_state_kernel, steps=steps),
        out_shape=SDS(s_t.shape, F32),
        grid=(2, nc // steps),
        in_specs=[data, par, par, BS((None, groups, 1), lambda dd, c: (dd, 0, 0))],
        out_specs=data,
        scratch_shapes=[pltpu.VMEM((groups, w), F32)],
        compiler_params=_cp("parallel", "arbitrary"),
        name="s5_state_scan",
    )(s_t, prm["lr2"], prm["li2"], prm["ls_g"])


def _s5_output_kernel(y1_ref, h_ref, lr_col, li_col, ls_ref, cre_t, cim_t, y_ref):
    dt = jnp.exp(ls_ref[...])
    z = _s5_power_c(lr_col[...], li_col[...], dt, cre_t[...], cim_t[...], 1)
    y_ref[...] = y1_ref[...] + jnp.dot(h_ref[...].astype(BF16), z.astype(BF16), preferred_element_type=F32)


def _s5_outputs(y1, h_in, prm):
    _, groups, nc, width = y1.shape
    row, col, one, bt, ct = _s5_param_specs(groups)
    data = lambda w: BS((None, None, nc, w), lambda dd, g: (dd, g, 0, 0))
    return pl.pallas_call(
        _s5_output_kernel,
        out_shape=SDS(y1.shape, F32),
        grid=(2, groups),
        in_specs=[data(width), data(2 * S5_P), col, col, one, ct, ct],
        out_specs=data(width),
        compiler_params=_cp("parallel", "parallel"),
        name="s5_outputs",
    )(y1, h_in, prm["lr_col"], prm["li_col"], prm["ls"], prm["cre_t"], prm["cim_t"])


def _s5_glu_kernel(x_ref, gain_ref, sc_ref, sh_ref, ys_ref, skip_ref, wv_ref, wg_ref, bv_ref, bg_ref,
                   xo_ref, gt_ref, o_ref, z_scr):
    @pl.when(pl.program_id(1) == 0)
    def _():
        h = _norm_mod(x_ref[...], gain_ref[...], sc_ref[...], sh_ref[...])
        z_scr[...] = jax.nn.gelu(skip_ref[...] * h + ys_ref[...]).astype(BF16)

    z = z_scr[...]
    val = jnp.dot(z, wv_ref[...], preferred_element_type=F32) + bv_ref[...]
    gate = jnp.dot(z, wg_ref[...], preferred_element_type=F32) + bg_ref[...]
    o_ref[...] = xo_ref[...] + gt_ref[...] * (val * (1.0 / (1.0 + jnp.exp(-gate))))


def _s5_glu(x, gain, mod3, ys, skip, w_bf, bias, n_lat):
    np_, d = x.shape
    tn = 512
    per = d // tn
    nlt = n_lat // TM
    b2 = bias.reshape(1, 2 * d)
    return pl.pallas_call(
        _s5_glu_kernel,
        out_shape=SDS((np_, d), F32),
        grid=(np_ // TM, per),
        in_specs=[BS((TM, d), lambda i, j: (i, 0)),
                  BS((1, d), lambda i, j: (0, 0)),
                  _mod_spec_full(nlt, d, 1),
                  _mod_spec_full(nlt, d, 0),
                  BS((TM, d), lambda i, j: (i, 0)),
                  BS((1, d), lambda i, j: (0, 0)),
                  BS((d, tn), lambda i, j: (0, j)),
                  BS((d, tn), lambda i, j: (0, per + j)),
                  BS((1, tn), lambda i, j: (0, j)),
                  BS((1, tn), lambda i, j: (0, per + j)),
                  BS((TM, tn), lambda i, j: (i, j)),
                  _mod_spec(nlt, d, tn, 2)],
        out_specs=BS((TM, tn), lambda i, j: (i, j)),
        scratch_shapes=[pltpu.VMEM((TM, d), BF16)],
        compiler_params=_cp("parallel", "arbitrary"),
        name="s5_glu",
    )(x, gain.reshape(1, d), mod3, mod3, ys, skip.reshape(1, d), w_bf, w_bf, b2, b2, x, mod3)


def _norm_mod_rows_kernel(x_ref, gain_ref, sc_ref, sh_ref, o_ref):
    o_ref[...] = _norm_mod(x_ref[...], gain_ref[...], sc_ref[...], sh_ref[...]).astype(BF16)


def _norm_mod_rows(x, gain, mod3, n_lat):
    np_, d = x.shape
    nlt = n_lat // TM
    return pl.pallas_call(
        _norm_mod_rows_kernel,
        out_shape=SDS((np_, d), BF16),
        grid=(np_ // TM, 1),
        in_specs=[BS((TM, d), lambda i, j: (i, 0)),
                  BS((1, d), lambda i, j: (0, 0)),
                  _mod_spec_full(nlt, d, 1),
                  _mod_spec_full(nlt, d, 0)],
        out_specs=BS((TM, d), lambda i, j: (i, 0)),
        compiler_params=_cp("parallel", "arbitrary"),
        name="norm_mod_rows",
    )(x, gain.reshape(1, d), mod3, mod3)


def _s5_layer(x, mod3, norm_gain, lam_re, lam_im, log_step, b_re, b_im, c_re, c_im, skip, w_glu, b_glu,
              n_lat, n_ctx):
    np_, d = x.shape
    groups = d // S5_GS
    n_tok = n_lat + n_ctx
    nc = n_tok // S5_T
    h = _norm_mod_rows(x, norm_gain, mod3, n_lat)
    lat, ctx = h[:n_lat], h[n_lat:n_tok]
    seqs = jnp.stack([jnp.concatenate([ctx, lat], axis=0), jnp.concatenate([lat, ctx], axis=0)[::-1]], axis=0)
    ut = seqs.reshape(2, nc, S5_T, groups, S5_GS).transpose(0, 3, 1, 2, 4).reshape(2, groups, nc, S5_T * S5_GS)
    prm = dict(
        lr_row=lam_re.reshape(2, groups, 1, S5_P), li_row=lam_im.reshape(2, groups, 1, S5_P),
        lr_col=lam_re.reshape(2, groups, S5_P, 1), li_col=lam_im.reshape(2, groups, S5_P, 1),
        ls=log_step.reshape(2, groups, 1, 1), ls_g=log_step.reshape(2, groups, 1),
        bre_t=b_re.transpose(0, 1, 3, 2), bim_t=b_im.transpose(0, 1, 3, 2),
        cre_t=c_re.transpose(0, 1, 3, 2), cim_t=c_im.transpose(0, 1, 3, 2),
        lr2=jnp.concatenate([lam_re, lam_re], axis=-1), li2=jnp.concatenate([lam_im, lam_im], axis=-1))
    y1, s = _s5_chunks(ut, prm)
    h_in = _s5_states(s.transpose(0, 2, 1, 3), prm).transpose(0, 2, 1, 3)
    y = _s5_outputs(y1, h_in, prm)
    y = y.reshape(2, groups, nc, S5_T, S5_GS).transpose(0, 2, 3, 1, 4).reshape(2, n_tok, d)
    y_lat = y[0, n_ctx:] + y[1, ::-1][:n_lat]
    ys = jnp.concatenate([y_lat, jnp.zeros((np_ - n_lat, d), F32)], axis=0)
    return _s5_glu(x, norm_gain, mod3, ys, skip, w_glu.astype(BF16), b_glu, n_lat)


def kernel(x, c, ctx, c_ctx, w_mod, b_mod, norm_mix, norm_ffn, router_w, router_b, w_gate_up, b_gate_up, w_down, b_down, win_w_qkv, win_w_o, win_q_gain, win_k_gain, win_sinks, full_w_qkv, full_w_o, full_q_gain, full_k_gain, hy_w_in, hy_b_in, hy_w_short, hy_b_short, hy_filt_w1, hy_filt_b1, hy_filt_w2, hy_filt_b2, hy_filt_w3, hy_filt_b3, hy_sin_freq, hy_decay, hy_skip, hy_w_out, hy_b_out, s5_lam_re, s5_lam_im, s5_log_step, s5_b_re, s5_b_im, s5_c_re, s5_c_im, s5_skip, s5_w_glu, s5_b_glu):
    batch, n_lat, d = x.shape
    n_ctx = ctx.shape[1]
    assert batch == 1 and n_lat % TM == 0 and n_ctx % TOK_DMA == 0 and (n_lat + n_ctx) % S5_T == 0
    depth = w_mod.shape[0]
    n_tok = n_lat + n_ctx
    pad = (-n_ctx) % TM
    cc = jnp.zeros((8, d), F32).at[0].set(c[0]).at[1].set(c_ctx)
    mod = _modulation(cc, w_mod, b_mod)
    xs = jnp.concatenate([x[0], ctx[0], jnp.zeros((pad, d), F32)], axis=0)
    cos2, sin2 = _rope_tables(n_lat)
    for i in range(depth):
        kind, j = i % 4, i // 4
        mod3 = mod[i, :2].reshape(2, 1, 6 * d)
        if kind == 0:
            xs = _attn_layer(xs, mod3, norm_mix[i], win_w_qkv[j], win_w_o[j], win_q_gain[j], win_k_gain[j],
                             win_sinks[j], cos2, sin2, n_lat, n_ctx, True)
        elif kind == 1:
            xs = _attn_layer(xs, mod3, norm_mix[i], full_w_qkv[j], full_w_o[j], full_q_gain[j], full_k_gain[j],
                             None, cos2, sin2, n_lat, n_ctx, False)
        elif kind == 2:
            xs = _hyena_layer(xs, mod3, norm_mix[i], hy_w_in[j], hy_b_in[j], hy_w_short[j], hy_b_short[j],
                              hy_filt_w1[j], hy_filt_b1[j], hy_filt_w2[j], hy_filt_b2[j], hy_filt_w3[j],
                              hy_filt_b3[j], hy_sin_freq[j], hy_decay[j], hy_skip[j], hy_w_out[j], hy_b_out[j],
                              n_lat, n_ctx)
        else:
            xs = _s5_layer(xs, mod3, norm_mix[i], s5_lam_re[j], s5_lam_im[j], s5_log_step[j], s5_b_re[j],
                           s5_b_im[j], s5_c_re[j], s5_c_im[j], s5_skip[j], s5_w_glu[j], s5_b_glu[j], n_lat, n_ctx)
        xs = _moe(xs, norm_ffn[i], mod3, router_w[i], router_b[i], w_gate_up[i].astype(BF16), b_gate_up[i],
                  w_down[i].astype(BF16), b_down[i], n_lat, n_tok)
    return xs[:n_lat].reshape(x.shape)
```

```python
import functools
import math

import numpy as np
import jax
import jax.numpy as jnp
from jax import lax
from jax.experimental import pallas as pl
from jax.experimental.pallas import tpu as pltpu

F32, BF16, I32 = jnp.float32, jnp.bfloat16, jnp.int32
SDS = jax.ShapeDtypeStruct
BS = pl.BlockSpec

HEAD_DIM = 128
N_KV_HEADS = 4
GROUP = 4
GRID_W = 64
WINDOW = 128
ROPE_BASE = 10000.0
TOP_K = 4
SWIGLU_LIMIT = 7.0
SWIGLU_ALPHA = 1.702
EPS = 1e-6
NEG_INF = -1e30

LANES = 128
TM = 512
TQ = 256
TK = 1024
ATT_RB = 128
LOG2E = math.log2(math.e)
MOE_BLK = 256
TOK_DMA = 256
VMEM_LIMIT = 56 * 1024 * 1024


def _cp(*sem):
    return pltpu.CompilerParams(dimension_semantics=sem, vmem_limit_bytes=VMEM_LIMIT)


def _norm_mod(x, gain, scale, shift):
    ms = jnp.mean(x * x, axis=-1, keepdims=True)
    return (x * lax.rsqrt(ms + EPS)) * gain * (1.0 + scale) + shift


def _split3(a):
    a0 = a.astype(BF16)
    r1 = a - a0.astype(F32)
    a1 = r1.astype(BF16)
    a2 = (r1 - a1.astype(F32)).astype(BF16)
    return a0, a1, a2


def _dot_f32(a, b):
    a0, a1, a2 = _split3(a)
    b0, b1, b2 = _split3(b)
    d = lambda u, v: jnp.dot(u, v, preferred_element_type=F32)
    return (d(a0, b0) + (d(a0, b1) + d(a1, b0)) + (d(a0, b2) + d(a1, b1) + d(a2, b0)))


def _mod_kernel(c_ref, w_ref, b_ref, o_ref):
    cc = c_ref[...]
    s = cc * (1.0 / (1.0 + jnp.exp(-cc)))
    o_ref[...] = jnp.dot(s.astype(BF16), w_ref[...].astype(BF16), preferred_element_type=F32) + b_ref[...]


def _modulation(cc, w_mod, b_mod):
    depth, d, d6 = w_mod.shape
    tn = 1024
    return pl.pallas_call(
        _mod_kernel,
        out_shape=SDS((depth, 8, d6), F32),
        grid=(depth, d6 // tn),
        in_specs=[BS((8, d), lambda l, j: (0, 0)),
                  BS((None, d, tn), lambda l, j: (l, 0, j)),
                  BS((None, 1, tn), lambda l, j: (l, 0, j))],
        out_specs=BS((None, 8, tn), lambda l, j: (l, 0, j)),
        compiler_params=_cp("parallel", "parallel"),
        name="modulation",
    )(cc, w_mod, b_mod.reshape(depth, 1, d6))


def _mod_spec(n_lat_tiles, d, tn, chunk):
    per = d // tn
    return BS((None, 1, tn), lambda i, j: (jnp.where(i < n_lat_tiles, 0, 1), 0, chunk * per + j))


def _mod_spec_full(n_lat_tiles, d, chunk):
    return BS((None, 1, d), lambda i, j: (jnp.where(i < n_lat_tiles, 0, 1), 0, chunk))


def _qkv_kernel(x_ref, gain_ref, sc_ref, sh_ref, w_ref, qg_ref, kg_ref, cos_ref, sin_ref, o_ref, h_scr,
                *, n_lat_tiles):
    i = pl.program_id(0)
    j = pl.program_id(1)

    @pl.when(j == 0)
    def _():
        h_scr[...] = _norm_mod(x_ref[...], gain_ref[...], sc_ref[...], sh_ref[...]).astype(BF16)

    acc = jnp.dot(h_scr[...], w_ref[...], preferred_element_type=F32)

    @pl.when(j < 5)
    def _():
        gain = jnp.where(j < 4, qg_ref[...], kg_ref[...])
        post = jnp.where(j < 4, HEAD_DIM ** -0.5 * LOG2E, 1.0)
        is_lat = i < n_lat_tiles
        for hh in range(4):
            t = acc[:, hh * HEAD_DIM:(hh + 1) * HEAD_DIM]
            t = t * lax.rsqrt(jnp.mean(t * t, axis=-1, keepdims=True) + EPS) * gain
            rot = pltpu.roll(t, HEAD_DIM // 2, axis=1)
            tr = t * cos_ref[...] + rot * sin_ref[...]
            t = jnp.where(is_lat, tr, t) * post
            o_ref[:, hh * HEAD_DIM:(hh + 1) * HEAD_DIM] = t.astype(BF16)

    @pl.when(j == 5)
    def _():
        o_ref[...] = acc.astype(BF16)


def _qkv(x, gain, mod3, w_bf, q_gain, k_gain, cos2, sin2, n_lat):
    np_, d = x.shape
    nt, nlt = np_ // TM, n_lat // TM
    width = w_bf.shape[1]
    tn = 4 * HEAD_DIM
    return pl.pallas_call(
        functools.partial(_qkv_kernel, n_lat_tiles=nlt),
        out_shape=SDS((np_, width), BF16),
        grid=(nt, width // tn),
        in_specs=[BS((TM, d), lambda i, j: (i, 0)),
                  BS((1, d), lambda i, j: (0, 0)),
                  _mod_spec_full(nlt, d, 1),
                  _mod_spec_full(nlt, d, 0),
                  BS((d, tn), lambda i, j: (0, j)),
                  BS((1, HEAD_DIM), lambda i, j: (0, 0)),
                  BS((1, HEAD_DIM), lambda i, j: (0, 0)),
                  BS((TM, HEAD_DIM), lambda i, j: (jnp.minimum(i, nlt - 1), 0)),
                  BS((TM, HEAD_DIM), lambda i, j: (jnp.minimum(i, nlt - 1), 0))],
        out_specs=BS((TM, tn), lambda i, j: (i, j)),
        scratch_shapes=[pltpu.VMEM((TM, d), BF16)],
        compiler_params=_cp("parallel", "arbitrary"),
        name="qkv_proj",
    )(x, gain.reshape(1, d), mod3, mod3, w_bf, q_gain.reshape(1, HEAD_DIM), k_gain.reshape(1, HEAD_DIM),
      cos2, sin2)


def _rope_tables(n_lat):
    pairs = HEAD_DIM // 4
    t = jnp.arange(n_lat, dtype=I32)
    row = (t // GRID_W).astype(F32)
    col = (t % GRID_W).astype(F32)
    freqs = ROPE_BASE ** (-jnp.arange(pairs, dtype=F32) / pairs)
    ang = jnp.concatenate([row[:, None] * freqs, col[:, None] * freqs], axis=-1)
    cos, sin = jnp.cos(ang), jnp.sin(ang)
    return jnp.concatenate([cos, cos], axis=-1), jnp.concatenate([-sin, sin], axis=-1)


def _attn_kernel(sink_ref, q_ref, k_ref, v_ref, o_ref, q_scr, m_scr, acc_scr,
                 *, windowed, has_sink, n_lat, n_ctx, n_ctx_pad):
    kvh = pl.program_id(0)
    i = pl.program_id(1)
    is_lat = i < n_lat // TQ
    rows = GROUP * TQ
    for h in range(GROUP):
        q_scr[h * TQ:(h + 1) * TQ, :] = q_ref[:, h * HEAD_DIM:(h + 1) * HEAD_DIM]
    m_scr[...] = jnp.full((rows, LANES), NEG_INF, F32)
    acc_scr[...] = jnp.zeros((rows, 2 * HEAD_DIM), F32)

    def step(start, size, mask_fn):
        k = k_ref[pl.ds(start, size), :]
        v = jnp.concatenate([v_ref[pl.ds(start, size), :], jnp.ones((size, HEAD_DIM), BF16)], axis=1)
        for r in range(rows // ATT_RB):
            sl = slice(r * ATT_RB, (r + 1) * ATT_RB)
            s = lax.dot_general(q_scr[sl, :], k, (((1,), (1,)), ((), ())), preferred_element_type=F32)
            if mask_fn is not None:
                s = jnp.where(mask_fn(r), s, NEG_INF)
            m_old = m_scr[sl, :]
            m_new = jnp.maximum(m_old, jnp.max(s, axis=-1, keepdims=True))
            alpha = jnp.exp2(m_old - m_new)
            p = jnp.exp2(s - jnp.concatenate([m_new] * (size // LANES), axis=1))
            acc_scr[sl, :] = (jnp.concatenate([alpha, alpha], axis=1) * acc_scr[sl, :]
                              + jnp.dot(p.astype(BF16), v, preferred_element_type=F32))
            m_scr[sl, :] = m_new

    if windowed:
        wk = TQ + 2 * WINDOW

        @pl.when(is_lat)
        def _():
            start = pl.multiple_of(jnp.clip(i * TQ - WINDOW, 0, n_lat - wk), WINDOW)

            def band(r):
                q_pos = i * TQ + (r * ATT_RB) % TQ + lax.broadcasted_iota(I32, (ATT_RB, wk), 0)
                k_pos = start + lax.broadcasted_iota(I32, (ATT_RB, wk), 1)
                return jnp.abs(q_pos - k_pos) <= WINDOW

            step(start, wk, band)
    else:
        def body(c, carry):
            step(pl.multiple_of(c * TK, TK), TK, None)
            return carry

        lax.fori_loop(0, jnp.where(is_lat, n_lat // TK, 0), body, 0)

    ctx_mask = None
    if n_ctx != n_ctx_pad:
        ctx_mask = lambda r: lax.broadcasted_iota(I32, (ATT_RB, n_ctx_pad), 1) < n_ctx
    step(n_lat, n_ctx_pad, ctx_mask)

    for h in range(GROUP):
        sl = slice(h * TQ, (h + 1) * TQ)
        acc = acc_scr[sl, :]
        num, den = acc[:, :HEAD_DIM], acc[:, HEAD_DIM:]
        if has_sink:
            m = m_scr[sl, :]
            sink = sink_ref[kvh * GROUP + h] * LOG2E
            m_f = jnp.maximum(m, sink)
            scale = jnp.exp2(m - m_f)
            num = num * scale
            den = den * scale + jnp.exp2(sink - m_f)
        o_ref[:, h * HEAD_DIM:(h + 1) * HEAD_DIM] = (num * (1.0 / den)).astype(BF16)


def _attention(qkv, sinks, n_lat, n_ctx, windowed):
    np_ = qkv.shape[0]
    n_heads = N_KV_HEADS * GROUP
    has_sink = sinks is not None
    if sinks is None:
        sinks = jnp.zeros((n_heads,), F32)
    rows = GROUP * TQ
    kern = functools.partial(_attn_kernel, windowed=windowed, has_sink=has_sink, n_lat=n_lat, n_ctx=n_ctx,
                             n_ctx_pad=np_ - n_lat)
    return pl.pallas_call(
        kern,
        out_shape=SDS((np_, n_heads * HEAD_DIM), BF16),
        grid_spec=pltpu.PrefetchScalarGridSpec(
            num_scalar_prefetch=1,
            grid=(N_KV_HEADS, np_ // TQ),
            in_specs=[BS((TQ, GROUP * HEAD_DIM), lambda kv, i, s: (i, kv)),
                      BS((np_, HEAD_DIM), lambda kv, i, s: (0, n_heads + kv)),
                      BS((np_, HEAD_DIM), lambda kv, i, s: (0, n_heads + N_KV_HEADS + kv))],
            out_specs=BS((TQ, GROUP * HEAD_DIM), lambda kv, i, s: (i, kv)),
            scratch_shapes=[pltpu.VMEM((rows, HEAD_DIM), BF16), pltpu.VMEM((rows, LANES), F32),
                            pltpu.VMEM((rows, 2 * HEAD_DIM), F32)]),
        compiler_params=_cp("parallel", "arbitrary"),
        name="window_attention" if windowed else "full_attention",
    )(sinks.astype(F32), qkv, qkv, qkv)


def _proj_res_kernel(a_ref, w_ref, b_ref, x_ref, gt_ref, o_ref):
    y = jnp.dot(a_ref[...], w_ref[...], preferred_element_type=F32) + b_ref[...]
    o_ref[...] = x_ref[...] + gt_ref[...] * y


def _proj_res(a, w_bf, bias, x, mod3, n_lat, gate_chunk):
    np_, d = x.shape
    kdim = a.shape[1]
    tn = 1024
    nlt = n_lat // TM
    return pl.pallas_call(
        _proj_res_kernel,
        out_shape=SDS((np_, d), F32),
        grid=(np_ // TM, d // tn),
        in_specs=[BS((TM, kdim), lambda i, j: (i, 0)),
                  BS((kdim, tn), lambda i, j: (0, j)),
                  BS((1, tn), lambda i, j: (0, j)),
                  BS((TM, tn), lambda i, j: (i, j)),
                  _mod_spec(nlt, d, tn, gate_chunk)],
        out_specs=BS((TM, tn), lambda i, j: (i, j)),
        compiler_params=_cp("parallel", "parallel"),
        name="proj_residual",
    )(a, w_bf, bias.reshape(1, d), x, mod3)


def _router_kernel(x_ref, gain_ref, sc_ref, sh_ref, wr_ref, br_ref, h_ref, idx_ref, gate_ref):
    h = _norm_mod(x_ref[...], gain_ref[...], sc_ref[...], sh_ref[...])
    h_ref[...] = h
    vals = _dot_f32(h, wr_ref[...]) + br_ref[...]
    lane = lax.broadcasted_iota(I32, vals.shape, 1).astype(F32)
    tops, idxs = [], []
    for _ in range(TOP_K):
        m = jnp.max(vals, axis=-1, keepdims=True)
        sel = jnp.min(jnp.where(vals == m, lane, float(LANES)), axis=-1, keepdims=True)
        tops.append(m)
        idxs.append(sel)
        vals = jnp.where(lane == sel, 2.0 * NEG_INF, vals)
    es = [jnp.exp(t - tops[0]) for t in tops]
    inv = 1.0 / (es[0] + es[1] + es[2] + es[3])
    idx_out = jnp.zeros(vals.shape, F32)
    gate_out = jnp.zeros(vals.shape, F32)
    for k in range(TOP_K):
        idx_out = jnp.where(lane == float(k), idxs[k], idx_out)
        gate_out = jnp.where(lane == float(k), es[k] * inv, gate_out)
    idx_ref[...] = idx_out.astype(I32)
    gate_ref[...] = gate_out


def _router(x, gain, mod3, wr_pad, br_pad, n_lat):
    np_, d = x.shape
    nlt = n_lat // TM
    return pl.pallas_call(
        _router_kernel,
        out_shape=(SDS((np_, d), F32), SDS((np_, LANES), I32), SDS((np_, LANES), F32)),
        grid=(np_ // TM, 1),
        in_specs=[BS((TM, d), lambda i, j: (i, 0)),
                  BS((1, d), lambda i, j: (0, 0)),
                  _mod_spec_full(nlt, d, 4),
                  _mod_spec_full(nlt, d, 3),
                  BS((d, LANES), lambda i, j: (0, 0)),
                  BS((1, LANES), lambda i, j: (0, 0))],
        out_specs=(BS((TM, d), lambda i, j: (i, 0)),
                   BS((TM, LANES), lambda i, j: (i, 0)),
                   BS((TM, LANES), lambda i, j: (i, 0))),
        compiler_params=_cp("parallel", "arbitrary"),
        name="router_top4",
    )(x, gain.reshape(1, d), mod3, mod3, wr_pad, br_pad)


def _rank_kernel(idx_ref, rank_ref, cnt_ref, base_scr, *, n_tok):
    i = pl.program_id(0)

    @pl.when(i == 0)
    def _():
        base_scr[...] = jnp.zeros(base_scr.shape, F32)

    idx = idx_ref[...]
    lane = lax.broadcasted_iota(I32, idx.shape, 1)
    row = i * TM + lax.broadcasted_iota(I32, idx.shape, 0)
    valid = row < n_tok
    hots = [jnp.where(valid & (lane == idx[:, k:k + 1]), 1.0, 0.0) for k in range(TOP_K)]
    cnt = hots[0] + hots[1] + hots[2] + hots[3]
    tri = jnp.where(lax.broadcasted_iota(I32, (TM, TM), 0) > lax.broadcasted_iota(I32, (TM, TM), 1), 1.0, 0.0)
    before = base_scr[...] + jnp.dot(tri.astype(BF16), cnt.astype(BF16), preferred_element_type=F32)
    out = jnp.zeros(idx.shape, F32)
    for k in range(TOP_K):
        out = jnp.where(lane == k, jnp.sum(hots[k] * before, axis=-1, keepdims=True), out)
    rank_ref[...] = out.astype(I32)
    base_scr[...] = base_scr[...] + jnp.sum(cnt, axis=0, keepdims=True)
    cnt_ref[...] = base_scr[...].astype(I32)


def _expert_ranks(idx, n_tok):
    np_ = idx.shape[0]
    return pl.pallas_call(
        functools.partial(_rank_kernel, n_tok=n_tok),
        out_shape=(SDS((np_, LANES), I32), SDS((1, LANES), I32)),
        grid=(np_ // TM,),
        in_specs=[BS((TM, LANES), lambda i: (i, 0))],
        out_specs=(BS((TM, LANES), lambda i: (i, 0)), BS((1, LANES), lambda i: (0, 0))),
        scratch_shapes=[pltpu.VMEM((1, LANES), F32)],
        compiler_params=_cp("arbitrary"),
        name="expert_ranks",
    )(idx)


def _row_copy(src, dst, sem):
    return pltpu.make_async_copy(src, dst, sem)


def _dispatch_kernel(dest_ref, h_ref, xs_init, xs_hbm, sem):
    del xs_init

    def issue(t, carry):
        src = h_ref.at[pl.ds(t, 1)]
        for k in range(TOP_K):
            _row_copy(src, xs_hbm.at[pl.ds(dest_ref[t * TOP_K + k], 1)], sem).start()
        return carry

    lax.fori_loop(0, TOK_DMA, issue, 0)

    def drain(t, carry):
        for k in range(TOP_K):
            _row_copy(h_ref.at[pl.ds(0, 1)], xs_hbm.at[pl.ds(0, 1)], sem).wait()
        return carry

    lax.fori_loop(0, TOK_DMA, drain, 0)


def _dispatch(dest_flat, h, n_tok, n_rows):
    d = h.shape[1]
    xs0 = jnp.zeros((n_rows, d), F32)
    return pl.pallas_call(
        _dispatch_kernel,
        out_shape=SDS((n_rows, d), F32),
        grid=(n_tok // TOK_DMA,),
        in_specs=[BS((TOK_DMA * TOP_K,), lambda i: (i,), memory_space=pltpu.SMEM),
                  BS((TOK_DMA, d), lambda i: (i, 0)),
                  BS(memory_space=pl.ANY)],
        out_specs=BS(memory_space=pl.ANY),
        scratch_shapes=[pltpu.SemaphoreType.DMA(())],
        input_output_aliases={2: 0},
        compiler_params=pltpu.CompilerParams(dimension_semantics=("arbitrary",), has_side_effects=True),
        name="moe_dispatch",
    )(dest_flat, h, xs0)


def _ffn_kernel(be_ref, nu_ref, x_ref, wgu_ref, bgu_ref, wdn_ref, bdn_ref, o_ref, *, ff):
    b = pl.program_id(0)

    @pl.when(b < nu_ref[0])
    def _():
        gu = jnp.dot(x_ref[...].astype(BF16), wgu_ref[...].astype(BF16), preferred_element_type=F32) + bgu_ref[...]
        glu = jnp.minimum(gu[:, :ff], SWIGLU_LIMIT)
        lin = jnp.clip(gu[:, ff:], -SWIGLU_LIMIT, SWIGLU_LIMIT)
        act = glu * (1.0 / (1.0 + jnp.exp(-SWIGLU_ALPHA * glu))) * (lin + 1.0)
        o_ref[...] = jnp.dot(act.astype(BF16), wdn_ref[...].astype(BF16), preferred_element_type=F32) + bdn_ref[...]

    @pl.when(b >= nu_ref[0])
    def _():
        o_ref[...] = jnp.zeros(o_ref.shape, F32)


def _expert_ffn(blk_expert, n_used, xs, w_gu, bgu, w_dn, bdn):
    n_rows, d = xs.shape
    n_exp, _, ff2 = w_gu.shape
    ff = ff2 // 2
    return pl.pallas_call(
        functools.partial(_ffn_kernel, ff=ff),
        out_shape=SDS((n_rows, d), F32),
        grid_spec=pltpu.PrefetchScalarGridSpec(
            num_scalar_prefetch=2,
            grid=(n_rows // MOE_BLK,),
            in_specs=[BS((MOE_BLK, d), lambda b, be, nu: (b, 0)),
                      BS((None, d, ff2), lambda b, be, nu: (be[b], 0, 0)),
                      BS((None, 1, ff2), lambda b, be, nu: (be[b], 0, 0)),
                      BS((None, ff, d), lambda b, be, nu: (be[b], 0, 0)),
                      BS((None, 1, d), lambda b, be, nu: (be[b], 0, 0))],
            out_specs=BS((MOE_BLK, d), lambda b, be, nu: (b, 0))),
        compiler_params=_cp("arbitrary"),
        name="expert_ffn",
    )(blk_expert, n_used, xs, w_gu, bgu.reshape(n_exp, 1, ff2), w_dn, bdn.reshape(n_exp, 1, d))


def _combine_kernel(dest_ref, g_ref, x_ref, gt_ref, y_hbm, o_ref, buf, sem):
    def issue(t, carry):
        for k in range(TOP_K):
            _row_copy(y_hbm.at[pl.ds(dest_ref[t * TOP_K + k], 1)], buf.at[k, pl.ds(t, 1)], sem).start()
        return carry

    lax.fori_loop(0, TOK_DMA, issue, 0)

    def drain(t, carry):
        for k in range(TOP_K):
            _row_copy(y_hbm.at[pl.ds(0, 1)], buf.at[k, pl.ds(0, 1)], sem).wait()
        return carry

    lax.fori_loop(0, TOK_DMA, drain, 0)
    g = g_ref[...]
    acc = g[:, 0:1] * buf[0]
    for k in range(1, TOP_K):
        acc = acc + g[:, k:k + 1] * buf[k]
    o_ref[...] = x_ref[...] + gt_ref[...] * acc


def _combine(dest_flat, gates, x, mod3, y, n_lat):
    np_, d = x.shape
    nlt = n_lat // TOK_DMA
    return pl.pallas_call(
        _combine_kernel,
        out_shape=SDS((np_, d), F32),
        grid=(np_ // TOK_DMA,),
        in_specs=[BS((TOK_DMA * TOP_K,), lambda i: (i,), memory_space=pltpu.SMEM),
                  BS((TOK_DMA, LANES), lambda i: (i, 0)),
                  BS((TOK_DMA, d), lambda i: (i, 0)),
                  BS((None, 1, d), lambda i: (jnp.where(i < nlt, 0, 1), 0, 5)),
                  BS(memory_space=pl.ANY)],
        out_specs=BS((TOK_DMA, d), lambda i: (i, 0)),
        scratch_shapes=[pltpu.VMEM((TOP_K, TOK_DMA, d), F32), pltpu.SemaphoreType.DMA(())],
        compiler_params=_cp("arbitrary"),
        name="moe_combine",
    )(dest_flat, gates, x, mod3, y)


def _moe(x, gain, mod3, router_w, router_b, w_gu, bgu, w_dn, bdn, n_lat, n_tok):
    np_, d = x.shape
    n_exp = router_w.shape[1]
    wr_pad = jnp.zeros((d, LANES), F32).at[:, :n_exp].set(router_w)
    br_pad = jnp.full((1, LANES), NEG_INF, F32).at[0, :n_exp].set(router_b)
    h, idx, gates = _router(x, gain, mod3, wr_pad, br_pad, n_lat)
    rank, counts = _expert_ranks(idx, n_tok)
    counts = counts[0, :n_exp]
    padded = (counts + MOE_BLK - 1) // MOE_BLK * MOE_BLK
    pad_end = jnp.cumsum(padded)
    pad_start = pad_end - padded
    dest = pad_start[idx[:, :TOP_K]] + rank[:, :TOP_K]
    dest = jnp.where(jnp.arange(np_)[:, None] < n_tok, dest, 0).astype(I32).reshape(-1)
    n_blocks = -(-(n_tok * TOP_K + n_exp * (MOE_BLK - 1)) // MOE_BLK)
    blk_row = jnp.arange(n_blocks, dtype=I32) * MOE_BLK
    blk_expert = jnp.minimum(jnp.sum(pad_end[None, :] <= blk_row[:, None], axis=1), n_exp - 1).astype(I32)
    n_used = (pad_end[-1:] // MOE_BLK).astype(I32)
    xs = _dispatch(dest, h, n_tok, n_blocks * MOE_BLK)
    y = _expert_ffn(blk_expert, n_used, xs, w_gu, bgu, w_dn, bdn)
    return _combine(dest, gates, x, mod3, y, n_lat)


def _attn_layer(x, mod3, norm_gain, w_qkv, w_o, q_gain, k_gain, sinks, cos2, sin2, n_lat, n_ctx, windowed):
    d = x.shape[1]
    qkv = _qkv(x, norm_gain, mod3, w_qkv.astype(BF16), q_gain, k_gain, cos2, sin2, n_lat)
    o = _attention(qkv, sinks, n_lat, n_ctx, windowed)
    return _proj_res(o, w_o.astype(BF16), jnp.zeros((d,), F32), x, mod3, n_lat, 2)


def _linear_norm_kernel(x_ref, gain_ref, sc_ref, sh_ref, w_ref, b_ref, o_ref, h_scr):
    @pl.when(pl.program_id(1) == 0)
    def _():
        h_scr[...] = _norm_mod(x_ref[...], gain_ref[...], sc_ref[...], sh_ref[...]).astype(BF16)

    o_ref[...] = (jnp.dot(h_scr[...], w_ref[...], preferred_element_type=F32) + b_ref[...]).astype(o_ref.dtype)


def _linear_norm(x, gain, mod3, w_bf, bias, n_lat, out_dtype):
    np_, d = x.shape
    width = w_bf.shape[1]
    tn = 1024
    nlt = n_lat // TM
    return pl.pallas_call(
        _linear_norm_kernel,
        out_shape=SDS((np_, width), out_dtype),
        grid=(np_ // TM, width // tn),
        in_specs=[BS((TM, d), lambda i, j: (i, 0)),
                  BS((1, d), lambda i, j: (0, 0)),
                  _mod_spec_full(nlt, d, 1),
                  _mod_spec_full(nlt, d, 0),
                  BS((d, tn), lambda i, j: (0, j)),
                  BS((1, tn), lambda i, j: (0, j))],
        out_specs=BS((TM, tn), lambda i, j: (i, j)),
        scratch_shapes=[pltpu.VMEM((TM, d), BF16)],
        compiler_params=_cp("parallel", "arbitrary"),
        name="linear_norm",
    )(x, gain.reshape(1, d), mod3, mod3, w_bf, bias.reshape(1, width))


HY_NB = 128
HY_BANDS = 8
HY_MOD_SHIFT = 0.05
HALO = 16


def _short_conv_kernel(z0c, z0p, z0n, z1c, z1p, z1n, z2c, z2p, z2n, w_ref, b_ref, x0_ref, u_ref,
                       *, n_lat_tiles, n_ctx):
    i = pl.program_id(0)
    row = lax.broadcasted_iota(I32, (TM, 1), 0)
    is_ctx = i >= n_lat_tiles
    has_prev = jnp.logical_and(i > 0, i != n_lat_tiles)
    has_next = i < n_lat_tiles - 1
    keep = jnp.logical_or(jnp.logical_not(is_ctx), row < n_ctx)

    def conv(cur, prv, nxt, part):
        zc = jnp.where(keep, cur[...].astype(F32), 0.0)
        first = jnp.where(has_prev, prv[HALO - 1:HALO, :].astype(F32), 0.0)
        last = jnp.where(has_next, nxt[0:1, :].astype(F32), 0.0)
        up = jnp.where(row == 0, first, pltpu.roll(zc, 1, axis=0))
        dn = jnp.where(row == TM - 1, last, pltpu.roll(zc, TM - 1, axis=0))
        w = w_ref[part]
        return up * w[0:1, :] + zc * w[1:2, :] + dn * w[2:3, :] + b_ref[part]

    x0 = conv(z0c, z0p, z0n, 0)
    x1 = conv(z1c, z1p, z1n, 1)
    v = conv(z2c, z2p, z2n, 2)
    x0_ref[...] = x0.astype(BF16)
    u_ref[...] = (x1 * v).astype(BF16)


def _short_conv(z, w_short, b_short, n_lat, n_ctx):
    np_, d3 = z.shape
    d = d3 // 3
    tn = 512
    per = d // tn
    nlt = n_lat // TM
    hb = TM // HALO
    last_halo = np_ // HALO - 1
    specs = []
    for part in range(3):
        specs.append(BS((TM, tn), lambda i, j, part=part: (i, part * per + j)))
        specs.append(BS((HALO, tn), lambda i, j, part=part: (jnp.maximum(i * hb - 1, 0), part * per + j)))
        specs.append(BS((HALO, tn), lambda i, j, part=part: (jnp.minimum((i + 1) * hb, last_halo), part * per + j)))
    w3 = w_short.reshape(3, 3, d).transpose(1, 0, 2)
    b3 = b_short.reshape(3, 1, d)
    return pl.pallas_call(
        functools.partial(_short_conv_kernel, n_lat_tiles=nlt, n_ctx=n_ctx),
        out_shape=(SDS((np_, d), BF16), SDS((np_, d), BF16)),
        grid=(np_ // TM, per),
        in_specs=specs + [BS((3, 3, tn), lambda i, j: (0, 0, j)), BS((3, 1, tn), lambda i, j: (0, 0, j))],
        out_specs=(BS((TM, tn), lambda i, j: (i, j)), BS((TM, tn), lambda i, j: (i, j))),
        compiler_params=_cp("parallel", "parallel"),
        name="hyena_short_conv",
    )(z, z, z, z, z, z, z, z, z, w3, b3)


def _filter_kernel(w1_ref, b1_ref, w2_ref, b2_ref, sf_ref, w3f_ref, b3f_ref, w3b_ref, b3b_ref, dec_ref, o_ref,
                   *, n, rows, s_mul, s_blk):
    blk = pl.program_id(0)
    a = lax.broadcasted_iota(I32, (rows, 1), 0)
    s = s_mul * a + s_blk * blk
    lag = jnp.where(s < n, s, 2 * n - s)
    t = lag.astype(F32) / float(n)
    lane = lax.broadcasted_iota(I32, (rows, LANES), 1)
    band = jnp.where(lane <= HY_BANDS, lane, lane - HY_BANDS).astype(F32)
    ang = 2.0 * jnp.pi * t * band
    feats = jnp.where(lane == 0, t,
                      jnp.where(lane <= HY_BANDS, jnp.cos(ang),
                                jnp.where(lane <= 2 * HY_BANDS, -jnp.sin(ang), 0.0)))
    hid = jnp.sin(sf_ref[0:1, :] * (_dot_f32(feats, w1_ref[...]) + b1_ref[...]))
    hid = jnp.sin(sf_ref[1:2, :] * (_dot_f32(hid, w2_ref[...]) + b2_ref[...]))
    fwd = _dot_f32(hid, w3f_ref[...]) + b3f_ref[...]
    bwd = _dot_f32(hid, w3b_ref[...]) + b3b_ref[...]
    window = jnp.exp(-t * jnp.abs(dec_ref[...])) + HY_MOD_SHIFT
    g = jnp.where(s < n, fwd, jnp.where(s == n, 0.0, bwd)) * window
    o_ref[...] = g.astype(o_ref.dtype)


def _hyena_filter(n, fw1, fb1, fw2, fb2, fw3, fb3, sin_freq, decay, two_d, out_dtype):
    width = fw1.shape[1]
    d = decay.shape[0]
    tn = 1024
    per = d // tn
    pad = lambda m, r, c: jnp.zeros((r, c), F32).at[:m.shape[0], :m.shape[1]].set(m)
    w1p = pad(fw1, LANES, LANES)
    b1p = pad(fb1.reshape(1, width), 1, LANES)
    w2p = pad(fw2, LANES, LANES)
    b2p = pad(fb2.reshape(1, width), 1, LANES)
    sfp = pad(sin_freq, 2, LANES)
    w3p = pad(fw3, LANES, 2 * d)
    b3 = fb3.reshape(1, 2 * d)
    if two_d:
        rows, nblk, s_mul, s_blk = 2 * n // HY_NB, HY_NB, HY_NB, 1
        out_shape = SDS((rows, HY_NB * d), out_dtype)
    else:
        rows = min(2 * n, 256)
        nblk, s_mul, s_blk = 2 * n // rows, 1, rows
        out_shape = SDS((2 * n, d), out_dtype)
    if two_d:
        out_spec = BS((rows, tn), lambda b, j: (0, b * per + j))
    else:
        out_spec = BS((rows, tn), lambda b, j: (b, j))
    const = lambda r, c: BS((r, c), lambda b, j: (0, 0))
    return pl.pallas_call(
        functools.partial(_filter_kernel, n=n, rows=rows, s_mul=s_mul, s_blk=s_blk),
        out_shape=out_shape,
        grid=(nblk, per),
        in_specs=[const(LANES, LANES), const(1, LANES), const(LANES, LANES), const(1, LANES), const(2, LANES),
                  BS((LANES, tn), lambda b, j: (0, j)), BS((1, tn), lambda b, j: (0, j)),
                  BS((LANES, tn), lambda b, j: (0, per + j)), BS((1, tn), lambda b, j: (0, per + j)),
                  BS((1, tn), lambda b, j: (0, j))],
        out_specs=out_spec,
        compiler_params=_cp("parallel", "parallel"),
        name="hyena_filter",
    )(w1p, b1p, w2p, b2p, sfp, w3p, b3, w3p, b3, decay.reshape(1, d))


def _dft_tables(n_total):
    nb = HY_NB
    na = n_total // nb
    ka = np.arange(na)[:, None]
    a = np.arange(na)[None, :]
    ang_a = 2.0 * np.pi * ((ka * a) % na) / na
    fa = np.concatenate([np.cos(ang_a), -np.sin(ang_a)], axis=0)
    kb = np.arange(nb)[:, None]
    b = np.arange(nb)[None, :]
    ang_b = 2.0 * np.pi * ((kb * b) % nb) / nb
    c, s = np.cos(ang_b), np.sin(ang_b)
    fwd_b = np.block([[c, s], [-s, c]])
    inv_b = np.block([[c, -s], [s, c]]) / float(n_total)
    ang_t = 2.0 * np.pi * ((np.arange(nb)[:, None] * np.arange(na)[None, :]) % n_total) / n_total
    tw = (np.cos(ang_t), -np.sin(ang_t))
    inv_a = np.concatenate([np.cos(ang_a[:na // 2]), -np.sin(ang_a[:na // 2])], axis=1)
    f = lambda m: jnp.asarray(m, F32)
    return dict(na=na, fa=f(fa), fwd_b=f(fwd_b), inv_b=f(inv_b), tw_re=f(tw[0]), tw_im=f(tw[1]), inv_a=f(inv_a))


def _dft_a_kernel(f_ref, x_ref, twr_ref, twi_ref, re_ref, im_ref, *, na):
    pq = jnp.dot(f_ref[...], x_ref[...], preferred_element_type=F32)
    p, q = pq[:na], pq[na:]
    twr, twi = twr_ref[...], twi_ref[...]
    re_ref[...] = (p * twr - q * twi).astype(BF16)
    im_ref[...] = (p * twi + q * twr).astype(BF16)


def _dft_a(x2d, f_bf, tw_re, tw_im, d):
    ka_rows, width = x2d.shape
    na = f_bf.shape[0] // 2
    tn = min(d, 2048)
    per = d // tn
    twr = tw_re.reshape(HY_NB, na, 1)
    twi = tw_im.reshape(HY_NB, na, 1)
    return pl.pallas_call(
        functools.partial(_dft_a_kernel, na=na),
        out_shape=(SDS((na, width), BF16), SDS((na, width), BF16)),
        grid=(width // tn,),
        in_specs=[BS((2 * na, ka_rows), lambda l: (0, 0)),
                  BS((ka_rows, tn), lambda l: (0, l)),
                  BS((None, na, 1), lambda l: (l // per, 0, 0)),
                  BS((None, na, 1), lambda l: (l // per, 0, 0))],
        out_specs=(BS((na, tn), lambda l: (0, l)), BS((na, tn), lambda l: (0, l))),
        compiler_params=_cp("parallel"),
        name="dft_stage_a",
    )(f_bf, x2d, twr, twi)


def _dft_b_kernel(fwd_ref, inv_ref, ur_ref, ui_ref, gr_ref, gi_ref, twr_ref, twi_ref, br_ref, bi_ref):
    nb = HY_NB
    fwd = fwd_ref[...]
    us = jnp.dot(fwd, jnp.concatenate([ur_ref[...], ui_ref[...]], axis=0), preferred_element_type=F32)
    gs = jnp.dot(fwd, jnp.concatenate([gr_ref[...], gi_ref[...]], axis=0), preferred_element_type=F32)
    ure, uim, gre, gim = us[:nb], us[nb:], gs[:nb], gs[nb:]
    yre = ure * gre - uim * gim
    yim = ure * gim + uim * gre
    ys = jnp.concatenate([yre, yim], axis=0).astype(BF16)
    bs = jnp.dot(inv_ref[...], ys, preferred_element_type=F32)
    bre, bim = bs[:nb], bs[nb:]
    twr, twi = twr_ref[...], twi_ref[...]
    br_ref[...] = (bre * twr + bim * twi).astype(BF16)
    bi_ref[...] = (bim * twr - bre * twi).astype(BF16)


def _dft_b(ur, ui, gr, gi, fwd_bf, inv_bf, tw_re, tw_im, na):
    rows, d = ur.shape
    tn = min(d, 2048)
    nb = HY_NB
    twr = tw_re.T.reshape(na, nb, 1)
    twi = tw_im.T.reshape(na, nb, 1)
    data = BS((nb, tn), lambda k, j: (k, j))
    tws = BS((None, nb, 1), lambda k, j: (k, 0, 0))
    return pl.pallas_call(
        _dft_b_kernel,
        out_shape=(SDS((rows, d), BF16), SDS((rows, d), BF16)),
        grid=(na, d // tn),
        in_specs=[BS((2 * nb, 2 * nb), lambda k, j: (0, 0)), BS((2 * nb, 2 * nb), lambda k, j: (0, 0)),
                  data, data, data, data, tws, tws],
        out_specs=(data, data),
        compiler_params=_cp("parallel", "parallel"),
        name="dft_stage_b",
    )(fwd_bf, inv_bf, ur, ui, gr, gi, twr, twi)


def _idft_a_kernel(f_ref, br_ref, bi_ref, u_ref, skip_ref, o_ref):
    bs = jnp.concatenate([br_ref[...], bi_ref[...]], axis=0)
    y = jnp.dot(f_ref[...], bs, preferred_element_type=F32)
    o_ref[...] = y + u_ref[...].astype(F32) * skip_ref[...]


def _idft_a(br2d, bi2d, u2d, skip, inv_a_bf, d):
    na, width = br2d.shape
    tn = min(d, 2048)
    per = d // tn
    return pl.pallas_call(
        _idft_a_kernel,
        out_shape=SDS((na // 2, width), F32),
        grid=(width // tn,),
        in_specs=[BS((na // 2, 2 * na), lambda l: (0, 0)),
                  BS((na, tn), lambda l: (0, l)),
                  BS((na, tn), lambda l: (0, l)),
                  BS((na // 2, tn), lambda l: (0, l)),
                  BS((1, tn), lambda l: (0, l % per))],
        out_specs=BS((na // 2, tn), lambda l: (0, l)),
        compiler_params=_cp("parallel"),
        name="idft_stage_a",
    )(inv_a_bf, br2d, bi2d, u2d, skip.reshape(1, d))


def _ctx_conv_kernel(g_ref, u_ref, skip_ref, o_ref, gg_scr, u_scr, *, n):
    g = g_ref[...]
    gg_scr[0:2 * n, :] = g
    gg_scr[2 * n:4 * n, :] = g
    u_scr[...] = u_ref[...].astype(F32)

    def body(s, acc):
        return acc + gg_scr[pl.ds(2 * n - s, n), :] * u_scr[pl.ds(s, 1), :]

    acc = lax.fori_loop(0, n, body, jnp.zeros((n, LANES), F32))
    o_ref[...] = acc + u_scr[...] * skip_ref[...]


def _ctx_conv(g, u, skip):
    n2, d = g.shape
    n = n2 // 2
    return pl.pallas_call(
        functools.partial(_ctx_conv_kernel, n=n),
        out_shape=SDS((n, d), F32),
        grid=(d // LANES,),
        in_specs=[BS((n2, LANES), lambda j: (0, j)), BS((n, LANES), lambda j: (0, j)),
                  BS((1, LANES), lambda j: (0, j))],
        out_specs=BS((n, LANES), lambda j: (0, j)),
        scratch_shapes=[pltpu.VMEM((2 * n2, LANES), F32), pltpu.VMEM((n, LANES), F32)],
        compiler_params=_cp("parallel"),
        name="hyena_ctx_conv",
    )(g, u, skip.reshape(1, d))


def _gated_proj_res_kernel(a_ref, y_ref, w_ref, b_ref, x_ref, gt_ref, o_ref):
    a = (a_ref[...].astype(F32) * y_ref[...]).astype(BF16)
    y = jnp.dot(a, w_ref[...], preferred_element_type=F32) + b_ref[...]
    o_ref[...] = x_ref[...] + gt_ref[...] * y


def _gated_proj_res(a, y, w_bf, bias, x, mod3, n_lat, gate_chunk):
    np_, d = x.shape
    kdim = a.shape[1]
    tn = 1024
    nlt = n_lat // TM
    return pl.pallas_call(
        _gated_proj_res_kernel,
        out_shape=SDS((np_, d), F32),
        grid=(np_ // TM, d // tn),
        in_specs=[BS((TM, kdim), lambda i, j: (i, 0)),
                  BS((TM, kdim), lambda i, j: (i, 0)),
                  BS((kdim, tn), lambda i, j: (0, j)),
                  BS((1, tn), lambda i, j: (0, j)),
                  BS((TM, tn), lambda i, j: (i, j)),
                  _mod_spec(nlt, d, tn, gate_chunk)],
        out_specs=BS((TM, tn), lambda i, j: (i, j)),
        compiler_params=_cp("parallel", "parallel"),
        name="gated_proj_residual",
    )(a, y, w_bf, bias.reshape(1, d), x, mod3)


def _hyena_layer(x, mod3, norm_gain, w_in, b_in, w_short, b_short, fw1, fb1, fw2, fb2, fw3, fb3, sin_freq, decay,
                 skip, w_out, b_out, n_lat, n_ctx):
    np_, d = x.shape
    z = _linear_norm(x, norm_gain, mod3, w_in.astype(BF16), b_in, n_lat, BF16)
    x0, u = _short_conv(z, w_short, b_short, n_lat, n_ctx)
    tabs = _dft_tables(2 * n_lat)
    na = tabs["na"]
    g2d = _hyena_filter(n_lat, fw1, fb1, fw2, fb2, fw3, fb3, sin_freq, decay, True, BF16)
    u2d = u[:n_lat].reshape(na // 2, HY_NB * d)
    fa_bf = tabs["fa"].astype(BF16)
    ur, ui = _dft_a(u2d, fa_bf[:, :na // 2], tabs["tw_re"], tabs["tw_im"], d)
    gr, gi = _dft_a(g2d, fa_bf, tabs["tw_re"], tabs["tw_im"], d)
    rs = lambda m: m.reshape(na * HY_NB, d)
    br, bi = _dft_b(rs(ur), rs(ui), rs(gr), rs(gi), tabs["fwd_b"].astype(BF16), tabs["inv_b"].astype(BF16),
                    tabs["tw_re"], tabs["tw_im"], na)
    rs2 = lambda m: m.reshape(na, HY_NB * d)
    y_lat = _idft_a(rs2(br), rs2(bi), u2d, skip, tabs["inv_a"].astype(BF16), d).reshape(n_lat, d)
    g_ctx = _hyena_filter(n_ctx, fw1, fb1, fw2, fb2, fw3, fb3, sin_freq, decay, False, F32)
    y_ctx = _ctx_conv(g_ctx, u[n_lat:n_lat + n_ctx], skip)
    y = jnp.concatenate([y_lat, y_ctx, jnp.zeros((np_ - n_lat - n_ctx, d), F32)], axis=0)
    return _gated_proj_res(x0, y, w_out.astype(BF16), b_out, x, mod3, n_lat, 2)


S5_T = 16
S5_GS = 16
S5_P = 64
S5_GB = 8


def _s5_discretize(lr_row, li_row, dt):
    mag = jnp.exp(lr_row * dt)
    a_re, a_im = mag * jnp.cos(li_row * dt), mag * jnp.sin(li_row * dt)
    xr, xi = a_re - 1.0, a_im
    den = 1.0 / (lr_row * lr_row + li_row * li_row)
    return (xr * lr_row + xi * li_row) * den, (xi * lr_row - xr * li_row) * den


def _s5_power_c(lr_col, li_col, dt, cre_t, cim_t, tau):
    width = S5_T * S5_GS
    lane = lax.broadcasted_iota(I32, (S5_GS, width), 1)
    expand = jnp.where(lane % S5_GS == lax.broadcasted_iota(I32, (S5_GS, width), 0), 1.0, 0.0)
    c_re = _dot_f32(cre_t, expand)
    c_im = _dot_f32(cim_t, expand)
    mag = jnp.exp(tau * (lr_col * dt))
    p_re, p_im = mag * jnp.cos(tau * (li_col * dt)), mag * jnp.sin(tau * (li_col * dt))
    z_re = p_re * c_re - p_im * c_im
    z_im = p_re * c_im + p_im * c_re
    return jnp.concatenate([z_re, -z_im], axis=0)


def _s5_chunk_kernel(u_ref, lr_row, li_row, lr_col, li_col, ls_ref, bre_t, bim_t, cre_t, cim_t,
                     y_ref, s_ref, zc_ref, mt_scr):
    dd = pl.program_id(1)
    width = S5_T * S5_GS
    lane = lax.broadcasted_iota(I32, (S5_GS, width), 1)
    t_lane = lax.broadcasted_iota(I32, (1, width), 1) // S5_GS
    t_row = lax.broadcasted_iota(I32, (width, 1), 0) // S5_GS
    row = lax.broadcasted_iota(I32, (width, S5_GS), 0)
    expand_t = jnp.where(row % S5_GS == lax.broadcasted_iota(I32, (width, S5_GS), 1), 1.0, 0.0)
    tau_k = jnp.where(dd == 0, t_lane, S5_T - 1 - t_lane).astype(F32)
    tau_g = jnp.where(dd == 0, S5_T - 1 - t_row, t_row).astype(F32)
    for j in range(S5_GB):
        dt = jnp.exp(ls_ref[j])
        q_re, q_im = _s5_discretize(lr_row[j], li_row[j], dt)
        bb_re = q_re * bre_t[j] - q_im * bim_t[j]
        bb_im = q_re * bim_t[j] + q_im * bre_t[j]
        z = _s5_power_c(lr_col[j], li_col[j], dt, cre_t[j], cim_t[j], tau_k)
        k0 = _dot_f32(jnp.concatenate([bb_re, bb_im], axis=1), z)
        a_mag = jnp.exp(lr_col[j] * dt)
        a_re, a_im = a_mag * jnp.cos(li_col[j] * dt), a_mag * jnp.sin(li_col[j] * dt)
        z_re, z_nim = z[:S5_P], z[S5_P:]
        zc_ref[j] = jnp.concatenate([a_re * z_re + a_im * z_nim, a_re * z_nim - a_im * z_re], axis=0).astype(BF16)

        @pl.when(dd == 0)
        def _():
            for s in range(S5_T):
                blk = k0 if s == 0 else jnp.where(lane >= s * S5_GS, pltpu.roll(k0, s * S5_GS, axis=1), 0.0)
                mt_scr[s * S5_GS:(s + 1) * S5_GS, :] = blk.astype(BF16)

        @pl.when(dd == 1)
        def _():
            for s in range(S5_T):
                sh = (S5_T - 1 - s) * S5_GS
                blk = k0 if sh == 0 else jnp.where(lane < (s + 1) * S5_GS, pltpu.roll(k0, width - sh, axis=1), 0.0)
                mt_scr[s * S5_GS:(s + 1) * S5_GS, :] = blk.astype(BF16)

        u = u_ref[j]
        y_ref[j] = jnp.dot(u, mt_scr[...], preferred_element_type=F32).astype(BF16)
        be_re = _dot_f32(expand_t, bb_re)
        be_im = _dot_f32(expand_t, bb_im)
        mag = jnp.exp(tau_g * (lr_row[j] * dt))
        p_re, p_im = mag * jnp.cos(tau_g * (li_row[j] * dt)), mag * jnp.sin(tau_g * (li_row[j] * dt))
        g = jnp.concatenate([p_re * be_re - p_im * be_im, p_re * be_im + p_im * be_re], axis=1)
        s_ref[:, j, :] = jnp.dot(u, g.astype(BF16), preferred_element_type=F32)


def _s5_chunks(ut, prm):
    groups, nc, width = ut.shape
    par = lambda r, c: BS((None, S5_GB, r, c), lambda i, dd: (dd, i, 0, 0))
    row, col, one = par(1, S5_P), par(S5_P, 1), par(1, 1)
    bt, ct = par(S5_GS, S5_P), par(S5_P, S5_GS)
    return pl.pallas_call(
        _s5_chunk_kernel,
        out_shape=(SDS((2, groups, nc, width), BF16), SDS((2, nc, groups, 2 * S5_P), F32),
                   SDS((2, groups, 2 * S5_P, width), BF16)),
        grid=(groups // S5_GB, 2),
        in_specs=[BS((S5_GB, nc, width), lambda i, dd: (i, 0, 0)), row, row, col, col, one, bt, bt, ct, ct],
        out_specs=(BS((None, S5_GB, nc, width), lambda i, dd: (dd, i, 0, 0)),
                   BS((None, nc, S5_GB, 2 * S5_P), lambda i, dd: (dd, 0, i, 0)),
                   BS((None, S5_GB, 2 * S5_P, width), lambda i, dd: (dd, i, 0, 0))),
        scratch_shapes=[pltpu.VMEM((width, width), BF16)],
        compiler_params=_cp("parallel", "arbitrary"),
        name="s5_chunks",
    )(ut, prm["lr_row"], prm["li_row"], prm["lr_col"], prm["li_col"], prm["ls"], prm["bre_t"], prm["bim_t"],
      prm["cre_t"], prm["cim_t"])


def _s5_state_kernel(s0_ref, s1_ref, lr_ref, li_ref, ls_ref, h0_ref, h1_ref, st_scr, *, steps):
    @pl.when(pl.program_id(0) == 0)
    def _():
        st_scr[...] = jnp.zeros(st_scr.shape, F32)

    def coeffs(dd):
        dt = jnp.exp(ls_ref[dd])
        mag = jnp.exp(float(S5_T) * lr_ref[dd] * dt)
        ang = float(S5_T) * li_ref[dd] * dt
        a_re, a_im = mag * jnp.cos(ang), mag * jnp.sin(ang)
        lane = lax.broadcasted_iota(I32, a_im.shape, 1)
        a_sw = jnp.where(lane < S5_P, -a_im, a_im)
        return a_re, a_sw

    def run(dd, s_ref, h_ref, order):
        a_re, a_sw = coeffs(dd)
        h, hs = st_scr[2 * dd], st_scr[2 * dd + 1]
        for c in order:
            h_ref[c] = h
            s = s_ref[c]
            s_swapped = pltpu.roll(s, S5_P, axis=1)
            h, hs = a_re * h + a_sw * hs + s, a_re * hs - a_sw * h + s_swapped
        st_scr[2 * dd] = h
        st_scr[2 * dd + 1] = hs

    run(0, s0_ref, h0_ref, range(steps))
    run(1, s1_ref, h1_ref, reversed(range(steps)))


def _s5_states(s_t, prm, first_chunk):
    _, nc, groups, w = s_t.shape
    steps = 8
    nblk = nc // steps
    off = first_chunk // steps
    fwd = BS((None, steps, groups, w), lambda c: (0, (c + off) % nblk, 0, 0))
    bwd = BS((None, steps, groups, w), lambda c: (1, nblk - 1 - c, 0, 0))
    fwd_o = BS((steps, groups, w), lambda c: ((c + off) % nblk, 0, 0))
    bwd_o = BS((steps, groups, w), lambda c: (nblk - 1 - c, 0, 0))
    par = lambda last: BS((2, groups, last), lambda c: (0, 0, 0))
    return pl.pallas_call(
        functools.partial(_s5_state_kernel, steps=steps),
        out_shape=(SDS((nc, groups, w), F32), SDS((nc, groups, w), F32)),
        grid=(nblk,),
        in_specs=[fwd, bwd, par(w), par(w), par(1)],
        out_specs=(fwd_o, bwd_o),
        scratch_shapes=[pltpu.VMEM((4, groups, w), F32)],
        compiler_params=_cp("arbitrary"),
        name="s5_state_scan",
    )(s_t, s_t, prm["lr2"], prm["li2"], prm["ls_g"])


def _s5_output_kernel(y1_ref, h0_ref, h1_ref, zc_ref, y_ref):
    j = pl.program_id(1)
    z = jnp.concatenate([zc_ref[0], zc_ref[1]], axis=0)
    y1 = y1_ref[0].astype(F32) + y1_ref[1].astype(F32)
    for jj in range(S5_GB):
        @pl.when(j == jj)
        def _():
            h = jnp.concatenate([h0_ref[:, jj, :], h1_ref[:, jj, :]], axis=1).astype(BF16)
            y_ref[...] = (y1 + jnp.dot(h, z, preferred_element_type=F32)).astype(BF16)


def _s5_outputs(y1, h0, h1, zc):
    _, groups, nc, width = y1.shape
    per_group = lambda r: BS((2, None, r, width), lambda i, j: (0, i * S5_GB + j, 0, 0))
    states = BS((nc, S5_GB, 2 * S5_P), lambda i, j: (0, i, 0))
    return pl.pallas_call(
        _s5_output_kernel,
        out_shape=SDS((groups, nc, width), BF16),
        grid=(groups // S5_GB, S5_GB),
        in_specs=[per_group(nc), states, states, per_group(2 * S5_P)],
        out_specs=BS((None, nc, width), lambda i, j: (i * S5_GB + j, 0, 0)),
        compiler_params=_cp("parallel", "arbitrary"),
        name="s5_outputs",
    )(y1, h0, h1, zc)


def _s5_glu_kernel(x_ref, gain_ref, sc_ref, sh_ref, ys_ref, skip_ref, wv_ref, wg_ref, bv_ref, bg_ref,
                   xo_ref, gt_ref, o_ref, z_scr):
    @pl.when(pl.program_id(1) == 0)
    def _():
        h = _norm_mod(x_ref[...], gain_ref[...], sc_ref[...], sh_ref[...])
        z_scr[...] = jax.nn.gelu(skip_ref[...] * h + ys_ref[...]).astype(BF16)

    z = z_scr[...]
    val = jnp.dot(z, wv_ref[...], preferred_element_type=F32) + bv_ref[...]
    gate = jnp.dot(z, wg_ref[...], preferred_element_type=F32) + bg_ref[...]
    o_ref[...] = xo_ref[...] + gt_ref[...] * (val * (1.0 / (1.0 + jnp.exp(-gate))))


def _s5_glu(x, gain, mod3, ys, skip, w_bf, bias, n_lat):
    np_, d = x.shape
    tn = 512
    per = d // tn
    nlt = n_lat // TM
    b2 = bias.reshape(1, 2 * d)
    return pl.pallas_call(
        _s5_glu_kernel,
        out_shape=SDS((np_, d), F32),
        grid=(np_ // TM, per),
        in_specs=[BS((TM, d), lambda i, j: (i, 0)),
                  BS((1, d), lambda i, j: (0, 0)),
                  _mod_spec_full(nlt, d, 1),
                  _mod_spec_full(nlt, d, 0),
                  BS((TM, d), lambda i, j: (i, 0)),
                  BS((1, d), lambda i, j: (0, 0)),
                  BS((d, tn), lambda i, j: (0, j)),
                  BS((d, tn), lambda i, j: (0, per + j)),
                  BS((1, tn), lambda i, j: (0, j)),
                  BS((1, tn), lambda i, j: (0, per + j)),
                  BS((TM, tn), lambda i, j: (i, j)),
                  _mod_spec(nlt, d, tn, 2)],
        out_specs=BS((TM, tn), lambda i, j: (i, j)),
        scratch_shapes=[pltpu.VMEM((TM, d), BF16)],
        compiler_params=_cp("parallel", "arbitrary"),
        name="s5_glu",
    )(x, gain.reshape(1, d), mod3, mod3, ys, skip.reshape(1, d), w_bf, w_bf, b2, b2, x, mod3)


def _norm_mod_rows_kernel(x_ref, gain_ref, sc_ref, sh_ref, o_ref):
    o_ref[...] = _norm_mod(x_ref[...], gain_ref[...], sc_ref[...], sh_ref[...]).astype(BF16)


def _norm_mod_rows(x, gain, mod3, n_lat):
    np_, d = x.shape
    nlt = n_lat // TM
    return pl.pallas_call(
        _norm_mod_rows_kernel,
        out_shape=SDS((np_, d), BF16),
        grid=(np_ // TM, 1),
        in_specs=[BS((TM, d), lambda i, j: (i, 0)),
                  BS((1, d), lambda i, j: (0, 0)),
                  _mod_spec_full(nlt, d, 1),
                  _mod_spec_full(nlt, d, 0)],
        out_specs=BS((TM, d), lambda i, j: (i, 0)),
        compiler_params=_cp("parallel", "arbitrary"),
        name="norm_mod_rows",
    )(x, gain.reshape(1, d), mod3, mod3)


def _s5_layer(x, mod3, norm_gain, lam_re, lam_im, log_step, b_re, b_im, c_re, c_im, skip, w_glu, b_glu,
              n_lat, n_ctx):
    np_, d = x.shape
    groups = d // S5_GS
    n_tok = n_lat + n_ctx
    nc = n_tok // S5_T
    h = _norm_mod_rows(x, norm_gain, mod3, n_lat)
    ut = h[:n_tok].reshape(nc, S5_T, groups, S5_GS).transpose(2, 0, 1, 3).reshape(groups, nc, S5_T * S5_GS)
    prm = dict(
        lr_row=lam_re.reshape(2, groups, 1, S5_P), li_row=lam_im.reshape(2, groups, 1, S5_P),
        lr_col=lam_re.reshape(2, groups, S5_P, 1), li_col=lam_im.reshape(2, groups, S5_P, 1),
        ls=log_step.reshape(2, groups, 1, 1), ls_g=log_step.reshape(2, groups, 1),
        bre_t=b_re.transpose(0, 1, 3, 2), bim_t=b_im.transpose(0, 1, 3, 2),
        cre_t=c_re.transpose(0, 1, 3, 2), cim_t=c_im.transpose(0, 1, 3, 2),
        lr2=jnp.concatenate([lam_re, lam_re], axis=-1), li2=jnp.concatenate([lam_im, lam_im], axis=-1))
    y1, s, zc = _s5_chunks(ut, prm)
    h0, h1 = _s5_states(s, prm, n_lat // S5_T)
    y = _s5_outputs(y1, h0, h1, zc)
    y = y.reshape(groups, nc, S5_T, S5_GS).transpose(1, 2, 0, 3).reshape(n_tok, d)
    ys = jnp.concatenate([y, jnp.zeros((np_ - n_tok, d), BF16)], axis=0)
    return _s5_glu(x, norm_gain, mod3, ys, skip, w_glu.astype(BF16), b_glu, n_lat)


def kernel(x, c, ctx, c_ctx, w_mod, b_mod, norm_mix, norm_ffn, router_w, router_b, w_gate_up, b_gate_up, w_down, b_down, win_w_qkv, win_w_o, win_q_gain, win_k_gain, win_sinks, full_w_qkv, full_w_o, full_q_gain, full_k_gain, hy_w_in, hy_b_in, hy_w_short, hy_b_short, hy_filt_w1, hy_filt_b1, hy_filt_w2, hy_filt_b2, hy_filt_w3, hy_filt_b3, hy_sin_freq, hy_decay, hy_skip, hy_w_out, hy_b_out, s5_lam_re, s5_lam_im, s5_log_step, s5_b_re, s5_b_im, s5_c_re, s5_c_im, s5_skip, s5_w_glu, s5_b_glu):
    batch, n_lat, d = x.shape
    n_ctx = ctx.shape[1]
    assert batch == 1 and n_lat % TM == 0 and n_ctx % TOK_DMA == 0 and (n_lat + n_ctx) % S5_T == 0
    depth = w_mod.shape[0]
    n_tok = n_lat + n_ctx
    pad = (-n_ctx) % TM
    cc = jnp.zeros((8, d), F32).at[0].set(c[0]).at[1].set(c_ctx)
    mod = _modulation(cc, w_mod, b_mod)
    xs = jnp.concatenate([x[0], ctx[0], jnp.zeros((pad, d), F32)], axis=0)
    cos2, sin2 = _rope_tables(n_lat)
    for i in range(depth):
        kind, j = i % 4, i // 4
        mod3 = mod[i, :2].reshape(2, 1, 6 * d)
        if kind == 0:
            xs = _attn_layer(xs, mod3, norm_mix[i], win_w_qkv[j], win_w_o[j], win_q_gain[j], win_k_gain[j],
                             win_sinks[j], cos2, sin2, n_lat, n_ctx, True)
        elif kind == 1:
            xs = _attn_layer(xs, mod3, norm_mix[i], full_w_qkv[j], full_w_o[j], full_q_gain[j], full_k_gain[j],
                             None, cos2, sin2, n_lat, n_ctx, False)
        elif kind == 2:
            xs = _hyena_layer(xs, mod3, norm_mix[i], hy_w_in[j], hy_b_in[j], hy_w_short[j], hy_b_short[j],
                              hy_filt_w1[j], hy_filt_b1[j], hy_filt_w2[j], hy_filt_b2[j], hy_filt_w3[j],
                              hy_filt_b3[j], hy_sin_freq[j], hy_decay[j], hy_skip[j], hy_w_out[j], hy_b_out[j],
                              n_lat, n_ctx)
        else:
            xs = _s5_layer(xs, mod3, norm_mix[i], s5_lam_re[j], s5_lam_im[j], s5_log_step[j], s5_b_re[j],
                           s5_b_im[j], s5_c_re[j], s5_c_im[j], s5_skip[j], s5_w_glu[j], s5_b_glu[j], n_lat, n_ctx)
        xs = _moe(xs, norm_ffn[i], mod3, router_w[i], router_b[i], w_gate_up[i], b_gate_up[i],
                  w_down[i], b_down[i], n_lat, n_tok)
    return xs[:n_lat].reshape(x.shape)
```

```python
import functools
import math

import numpy as np
import jax
import jax.numpy as jnp
from jax import lax
from jax.experimental import pallas as pl
from jax.experimental.pallas import tpu as pltpu

F32, BF16, I32 = jnp.float32, jnp.bfloat16, jnp.int32
SDS = jax.ShapeDtypeStruct
BS = pl.BlockSpec

HEAD_DIM = 128
N_KV_HEADS = 4
GROUP = 4
GRID_W = 64
WINDOW = 128
ROPE_BASE = 10000.0
TOP_K = 4
SWIGLU_LIMIT = 7.0
SWIGLU_ALPHA = 1.702
EPS = 1e-6
NEG_INF = -1e30

LANES = 128
TM = 512
TQ = 256
TK = 1024
ATT_RB = 128
LOG2E = math.log2(math.e)
MOE_BLK = 256
TOK_DMA = 256
VMEM_LIMIT = 56 * 1024 * 1024


def _cp(*sem):
    return pltpu.CompilerParams(dimension_semantics=sem, vmem_limit_bytes=VMEM_LIMIT)


def _norm_mod(x, gain, scale, shift):
    ms = jnp.mean(x * x, axis=-1, keepdims=True)
    return (x * lax.rsqrt(ms + EPS)) * gain * (1.0 + scale) + shift


def _split3(a):
    a0 = a.astype(BF16)
    r1 = a - a0.astype(F32)
    a1 = r1.astype(BF16)
    a2 = (r1 - a1.astype(F32)).astype(BF16)
    return a0, a1, a2


def _dot_f32(a, b):
    a0, a1, a2 = _split3(a)
    b0, b1, b2 = _split3(b)
    d = lambda u, v: jnp.dot(u, v, preferred_element_type=F32)
    return (d(a0, b0) + (d(a0, b1) + d(a1, b0)) + (d(a0, b2) + d(a1, b1) + d(a2, b0)))


def _mod_kernel(c_ref, w_ref, b_ref, o_ref):
    cc = c_ref[...]
    s = cc * (1.0 / (1.0 + jnp.exp(-cc)))
    o_ref[...] = jnp.dot(s.astype(BF16), w_ref[...].astype(BF16), preferred_element_type=F32) + b_ref[...]


def _modulation(cc, w_mod, b_mod):
    depth, d, d6 = w_mod.shape
    tn = 1024
    return pl.pallas_call(
        _mod_kernel,
        out_shape=SDS((depth, 8, d6), F32),
        grid=(depth, d6 // tn),
        in_specs=[BS((8, d), lambda l, j: (0, 0)),
                  BS((None, d, tn), lambda l, j: (l, 0, j)),
                  BS((None, 1, tn), lambda l, j: (l, 0, j))],
        out_specs=BS((None, 8, tn), lambda l, j: (l, 0, j)),
        compiler_params=_cp("parallel", "parallel"),
        name="modulation",
    )(cc, w_mod, b_mod.reshape(depth, 1, d6))


def _mod_spec(n_lat_tiles, d, tn, chunk):
    per = d // tn
    return BS((None, 1, tn), lambda i, j: (jnp.where(i < n_lat_tiles, 0, 1), 0, chunk * per + j))


def _mod_spec_full(n_lat_tiles, d, chunk):
    return BS((None, 1, d), lambda i, j: (jnp.where(i < n_lat_tiles, 0, 1), 0, chunk))


def _qkv_kernel(x_ref, gain_ref, sc_ref, sh_ref, w_ref, qg_ref, kg_ref, cos_ref, sin_ref, o_ref, h_scr,
                *, n_lat_tiles):
    i = pl.program_id(0)
    j = pl.program_id(1)

    @pl.when(j == 0)
    def _():
        h_scr[...] = _norm_mod(x_ref[...], gain_ref[...], sc_ref[...], sh_ref[...]).astype(BF16)

    acc = jnp.dot(h_scr[...], w_ref[...], preferred_element_type=F32)

    @pl.when(j < 5)
    def _():
        gain = jnp.where(j < 4, qg_ref[...], kg_ref[...])
        post = jnp.where(j < 4, HEAD_DIM ** -0.5 * LOG2E, 1.0)
        is_lat = i < n_lat_tiles
        for hh in range(4):
            t = acc[:, hh * HEAD_DIM:(hh + 1) * HEAD_DIM]
            t = t * lax.rsqrt(jnp.mean(t * t, axis=-1, keepdims=True) + EPS) * gain
            rot = pltpu.roll(t, HEAD_DIM // 2, axis=1)
            tr = t * cos_ref[...] + rot * sin_ref[...]
            t = jnp.where(is_lat, tr, t) * post
            o_ref[:, hh * HEAD_DIM:(hh + 1) * HEAD_DIM] = t.astype(BF16)

    @pl.when(j == 5)
    def _():
        o_ref[...] = acc.astype(BF16)


def _qkv(x, gain, mod3, w_bf, q_gain, k_gain, cos2, sin2, n_lat):
    np_, d = x.shape
    nt, nlt = np_ // TM, n_lat // TM
    width = w_bf.shape[1]
    tn = 4 * HEAD_DIM
    return pl.pallas_call(
        functools.partial(_qkv_kernel, n_lat_tiles=nlt),
        out_shape=SDS((np_, width), BF16),
        grid=(nt, width // tn),
        in_specs=[BS((TM, d), lambda i, j: (i, 0)),
                  BS((1, d), lambda i, j: (0, 0)),
                  _mod_spec_full(nlt, d, 1),
                  _mod_spec_full(nlt, d, 0),
                  BS((d, tn), lambda i, j: (0, j)),
                  BS((1, HEAD_DIM), lambda i, j: (0, 0)),
                  BS((1, HEAD_DIM), lambda i, j: (0, 0)),
                  BS((TM, HEAD_DIM), lambda i, j: (jnp.minimum(i, nlt - 1), 0)),
                  BS((TM, HEAD_DIM), lambda i, j: (jnp.minimum(i, nlt - 1), 0))],
        out_specs=BS((TM, tn), lambda i, j: (i, j)),
        scratch_shapes=[pltpu.VMEM((TM, d), BF16)],
        compiler_params=_cp("parallel", "arbitrary"),
        name="qkv_proj",
    )(x, gain.reshape(1, d), mod3, mod3, w_bf, q_gain.reshape(1, HEAD_DIM), k_gain.reshape(1, HEAD_DIM),
      cos2, sin2)


def _rope_tables(n_lat):
    pairs = HEAD_DIM // 4
    t = jnp.arange(n_lat, dtype=I32)
    row = (t // GRID_W).astype(F32)
    col = (t % GRID_W).astype(F32)
    freqs = ROPE_BASE ** (-jnp.arange(pairs, dtype=F32) / pairs)
    ang = jnp.concatenate([row[:, None] * freqs, col[:, None] * freqs], axis=-1)
    cos, sin = jnp.cos(ang), jnp.sin(ang)
    return jnp.concatenate([cos, cos], axis=-1), jnp.concatenate([-sin, sin], axis=-1)


def _attn_kernel(sink_ref, q_ref, k_ref, v_ref, o_ref, q_scr, m_scr, acc_scr,
                 *, windowed, has_sink, n_lat, n_ctx, n_ctx_pad):
    kvh = pl.program_id(0)
    i = pl.program_id(1)
    is_lat = i < n_lat // TQ
    rows = GROUP * TQ
    for h in range(GROUP):
        q_scr[h * TQ:(h + 1) * TQ, :] = q_ref[:, h * HEAD_DIM:(h + 1) * HEAD_DIM]
    m_scr[...] = jnp.full((rows, LANES), NEG_INF, F32)
    acc_scr[...] = jnp.zeros((rows, 2 * HEAD_DIM), F32)

    def step(start, size, mask_fn):
        k = k_ref[pl.ds(start, size), :]
        v = jnp.concatenate([v_ref[pl.ds(start, size), :], jnp.ones((size, HEAD_DIM), BF16)], axis=1)
        for r in range(rows // ATT_RB):
            sl = slice(r * ATT_RB, (r + 1) * ATT_RB)
            s = lax.dot_general(q_scr[sl, :], k, (((1,), (1,)), ((), ())), preferred_element_type=F32)
            if mask_fn is not None:
                s = jnp.where(mask_fn(r), s, NEG_INF)
            m_old = m_scr[sl, :]
            m_new = jnp.maximum(m_old, jnp.max(s, axis=-1, keepdims=True))
            alpha = jnp.exp2(m_old - m_new)
            p = jnp.exp2(s - jnp.concatenate([m_new] * (size // LANES), axis=1))
            acc_scr[sl, :] = (jnp.concatenate([alpha, alpha], axis=1) * acc_scr[sl, :]
                              + jnp.dot(p.astype(BF16), v, preferred_element_type=F32))
            m_scr[sl, :] = m_new

    if windowed:
        wk = TQ + 2 * WINDOW

        @pl.when(is_lat)
        def _():
            start = pl.multiple_of(jnp.clip(i * TQ - WINDOW, 0, n_lat - wk), WINDOW)

            def band(r):
                q_pos = i * TQ + (r * ATT_RB) % TQ + lax.broadcasted_iota(I32, (ATT_RB, wk), 0)
                k_pos = start + lax.broadcasted_iota(I32, (ATT_RB, wk), 1)
                return jnp.abs(q_pos - k_pos) <= WINDOW

            step(start, wk, band)
    else:
        def body(c, carry):
            step(pl.multiple_of(c * TK, TK), TK, None)
            return carry

        lax.fori_loop(0, jnp.where(is_lat, n_lat // TK, 0), body, 0)

    ctx_mask = None
    if n_ctx != n_ctx_pad:
        ctx_mask = lambda r: lax.broadcasted_iota(I32, (ATT_RB, n_ctx_pad), 1) < n_ctx
    step(n_lat, n_ctx_pad, ctx_mask)

    for h in range(GROUP):
        sl = slice(h * TQ, (h + 1) * TQ)
        acc = acc_scr[sl, :]
        num, den = acc[:, :HEAD_DIM], acc[:, HEAD_DIM:]
        if has_sink:
            m = m_scr[sl, :]
            sink = sink_ref[kvh * GROUP + h] * LOG2E
            m_f = jnp.maximum(m, sink)
            scale = jnp.exp2(m - m_f)
            num = num * scale
            den = den * scale + jnp.exp2(sink - m_f)
        o_ref[:, h * HEAD_DIM:(h + 1) * HEAD_DIM] = (num * (1.0 / den)).astype(BF16)


def _attention(qkv, sinks, n_lat, n_ctx, windowed):
    np_ = qkv.shape[0]
    n_heads = N_KV_HEADS * GROUP
    has_sink = sinks is not None
    if sinks is None:
        sinks = jnp.zeros((n_heads,), F32)
    rows = GROUP * TQ
    kern = functools.partial(_attn_kernel, windowed=windowed, has_sink=has_sink, n_lat=n_lat, n_ctx=n_ctx,
                             n_ctx_pad=np_ - n_lat)
    return pl.pallas_call(
        kern,
        out_shape=SDS((np_, n_heads * HEAD_DIM), BF16),
        grid_spec=pltpu.PrefetchScalarGridSpec(
            num_scalar_prefetch=1,
            grid=(N_KV_HEADS, np_ // TQ),
            in_specs=[BS((TQ, GROUP * HEAD_DIM), lambda kv, i, s: (i, kv)),
                      BS((np_, HEAD_DIM), lambda kv, i, s: (0, n_heads + kv)),
                      BS((np_, HEAD_DIM), lambda kv, i, s: (0, n_heads + N_KV_HEADS + kv))],
            out_specs=BS((TQ, GROUP * HEAD_DIM), lambda kv, i, s: (i, kv)),
            scratch_shapes=[pltpu.VMEM((rows, HEAD_DIM), BF16), pltpu.VMEM((rows, LANES), F32),
                            pltpu.VMEM((rows, 2 * HEAD_DIM), F32)]),
        compiler_params=_cp("parallel", "arbitrary"),
        name="window_attention" if windowed else "full_attention",
    )(sinks.astype(F32), qkv, qkv, qkv)


def _proj_res_kernel(a_ref, w_ref, b_ref, x_ref, gt_ref, o_ref):
    y = jnp.dot(a_ref[...], w_ref[...], preferred_element_type=F32) + b_ref[...]
    o_ref[...] = x_ref[...] + gt_ref[...] * y


def _proj_res(a, w_bf, bias, x, mod3, n_lat, gate_chunk):
    np_, d = x.shape
    kdim = a.shape[1]
    tn = 1024
    nlt = n_lat // TM
    return pl.pallas_call(
        _proj_res_kernel,
        out_shape=SDS((np_, d), F32),
        grid=(np_ // TM, d // tn),
        in_specs=[BS((TM, kdim), lambda i, j: (i, 0)),
                  BS((kdim, tn), lambda i, j: (0, j)),
                  BS((1, tn), lambda i, j: (0, j)),
                  BS((TM, tn), lambda i, j: (i, j)),
                  _mod_spec(nlt, d, tn, gate_chunk)],
        out_specs=BS((TM, tn), lambda i, j: (i, j)),
        compiler_params=_cp("parallel", "parallel"),
        name="proj_residual",
    )(a, w_bf, bias.reshape(1, d), x, mod3)


def _router_kernel(x_ref, gain_ref, sc_ref, sh_ref, wr_ref, br_ref, h_ref, idx_ref, gate_ref):
    h = _norm_mod(x_ref[...], gain_ref[...], sc_ref[...], sh_ref[...])
    h_ref[...] = h
    vals = _dot_f32(h, wr_ref[...]) + br_ref[...]
    lane = lax.broadcasted_iota(I32, vals.shape, 1).astype(F32)
    tops, idxs = [], []
    for _ in range(TOP_K):
        m = jnp.max(vals, axis=-1, keepdims=True)
        sel = jnp.min(jnp.where(vals == m, lane, float(LANES)), axis=-1, keepdims=True)
        tops.append(m)
        idxs.append(sel)
        vals = jnp.where(lane == sel, 2.0 * NEG_INF, vals)
    es = [jnp.exp(t - tops[0]) for t in tops]
    inv = 1.0 / (es[0] + es[1] + es[2] + es[3])
    idx_out = jnp.zeros(vals.shape, F32)
    gate_out = jnp.zeros(vals.shape, F32)
    for k in range(TOP_K):
        idx_out = jnp.where(lane == float(k), idxs[k], idx_out)
        gate_out = jnp.where(lane == float(k), es[k] * inv, gate_out)
    idx_ref[...] = idx_out.astype(I32)
    gate_ref[...] = gate_out


def _router(x, gain, mod3, wr_pad, br_pad, n_lat):
    np_, d = x.shape
    nlt = n_lat // TM
    return pl.pallas_call(
        _router_kernel,
        out_shape=(SDS((np_, d), F32), SDS((np_, LANES), I32), SDS((np_, LANES), F32)),
        grid=(np_ // TM, 1),
        in_specs=[BS((TM, d), lambda i, j: (i, 0)),
                  BS((1, d), lambda i, j: (0, 0)),
                  _mod_spec_full(nlt, d, 4),
                  _mod_spec_full(nlt, d, 3),
                  BS((d, LANES), lambda i, j: (0, 0)),
                  BS((1, LANES), lambda i, j: (0, 0))],
        out_specs=(BS((TM, d), lambda i, j: (i, 0)),
                   BS((TM, LANES), lambda i, j: (i, 0)),
                   BS((TM, LANES), lambda i, j: (i, 0))),
        compiler_params=_cp("parallel", "arbitrary"),
        name="router_top4",
    )(x, gain.reshape(1, d), mod3, mod3, wr_pad, br_pad)


def _rank_kernel(idx_ref, rank_ref, cnt_ref, base_scr, *, n_tok):
    i = pl.program_id(0)

    @pl.when(i == 0)
    def _():
        base_scr[...] = jnp.zeros(base_scr.shape, F32)

    idx = idx_ref[...]
    lane = lax.broadcasted_iota(I32, idx.shape, 1)
    row = i * TM + lax.broadcasted_iota(I32, idx.shape, 0)
    valid = row < n_tok
    hots = [jnp.where(valid & (lane == idx[:, k:k + 1]), 1.0, 0.0) for k in range(TOP_K)]
    cnt = hots[0] + hots[1] + hots[2] + hots[3]
    tri = jnp.where(lax.broadcasted_iota(I32, (TM, TM), 0) > lax.broadcasted_iota(I32, (TM, TM), 1), 1.0, 0.0)
    before = base_scr[...] + jnp.dot(tri.astype(BF16), cnt.astype(BF16), preferred_element_type=F32)
    out = jnp.zeros(idx.shape, F32)
    for k in range(TOP_K):
        out = jnp.where(lane == k, jnp.sum(hots[k] * before, axis=-1, keepdims=True), out)
    rank_ref[...] = out.astype(I32)
    base_scr[...] = base_scr[...] + jnp.sum(cnt, axis=0, keepdims=True)
    cnt_ref[...] = base_scr[...].astype(I32)


def _expert_ranks(idx, n_tok):
    np_ = idx.shape[0]
    return pl.pallas_call(
        functools.partial(_rank_kernel, n_tok=n_tok),
        out_shape=(SDS((np_, LANES), I32), SDS((1, LANES), I32)),
        grid=(np_ // TM,),
        in_specs=[BS((TM, LANES), lambda i: (i, 0))],
        out_specs=(BS((TM, LANES), lambda i: (i, 0)), BS((1, LANES), lambda i: (0, 0))),
        scratch_shapes=[pltpu.VMEM((1, LANES), F32)],
        compiler_params=_cp("arbitrary"),
        name="expert_ranks",
    )(idx)


def _row_copy(src, dst, sem):
    return pltpu.make_async_copy(src, dst, sem)


def _dispatch_kernel(dest_ref, h_ref, xs_init, xs_hbm, sem):
    del xs_init

    def issue(t, carry):
        src = h_ref.at[pl.ds(t, 1)]
        for k in range(TOP_K):
            _row_copy(src, xs_hbm.at[pl.ds(dest_ref[t * TOP_K + k], 1)], sem).start()
        return carry

    lax.fori_loop(0, TOK_DMA, issue, 0)

    def drain(t, carry):
        for k in range(TOP_K):
            _row_copy(h_ref.at[pl.ds(0, 1)], xs_hbm.at[pl.ds(0, 1)], sem).wait()
        return carry

    lax.fori_loop(0, TOK_DMA, drain, 0)


def _dispatch(dest_flat, h, n_tok, n_rows):
    d = h.shape[1]
    xs0 = jnp.zeros((n_rows, d), F32)
    return pl.pallas_call(
        _dispatch_kernel,
        out_shape=SDS((n_rows, d), F32),
        grid=(n_tok // TOK_DMA,),
        in_specs=[BS((TOK_DMA * TOP_K,), lambda i: (i,), memory_space=pltpu.SMEM),
                  BS((TOK_DMA, d), lambda i: (i, 0)),
                  BS(memory_space=pl.ANY)],
        out_specs=BS(memory_space=pl.ANY),
        scratch_shapes=[pltpu.SemaphoreType.DMA(())],
        input_output_aliases={2: 0},
        compiler_params=pltpu.CompilerParams(dimension_semantics=("arbitrary",), has_side_effects=True),
        name="moe_dispatch",
    )(dest_flat, h, xs0)


def _ffn_kernel(be_ref, nu_ref, x_ref, wgu_ref, bgu_ref, wdn_ref, bdn_ref, o_ref, *, ff):
    b = pl.program_id(0)

    @pl.when(b < nu_ref[0])
    def _():
        gu = jnp.dot(x_ref[...].astype(BF16), wgu_ref[...].astype(BF16), preferred_element_type=F32) + bgu_ref[...]
        glu = jnp.minimum(gu[:, :ff], SWIGLU_LIMIT)
        lin = jnp.clip(gu[:, ff:], -SWIGLU_LIMIT, SWIGLU_LIMIT)
        act = glu * (1.0 / (1.0 + jnp.exp(-SWIGLU_ALPHA * glu))) * (lin + 1.0)
        o_ref[...] = jnp.dot(act.astype(BF16), wdn_ref[...].astype(BF16), preferred_element_type=F32) + bdn_ref[...]

    @pl.when(b >= nu_ref[0])
    def _():
        o_ref[...] = jnp.zeros(o_ref.shape, F32)


def _expert_ffn(blk_expert, n_used, xs, w_gu, bgu, w_dn, bdn):
    n_rows, d = xs.shape
    n_exp, _, ff2 = w_gu.shape
    ff = ff2 // 2
    return pl.pallas_call(
        functools.partial(_ffn_kernel, ff=ff),
        out_shape=SDS((n_rows, d), F32),
        grid_spec=pltpu.PrefetchScalarGridSpec(
            num_scalar_prefetch=2,
            grid=(n_rows // MOE_BLK,),
            in_specs=[BS((MOE_BLK, d), lambda b, be, nu: (b, 0)),
                      BS((None, d, ff2), lambda b, be, nu: (be[b], 0, 0)),
                      BS((None, 1, ff2), lambda b, be, nu: (be[b], 0, 0)),
                      BS((None, ff, d), lambda b, be, nu: (be[b], 0, 0)),
                      BS((None, 1, d), lambda b, be, nu: (be[b], 0, 0))],
            out_specs=BS((MOE_BLK, d), lambda b, be, nu: (b, 0))),
        compiler_params=_cp("arbitrary"),
        name="expert_ffn",
    )(blk_expert, n_used, xs, w_gu, bgu.reshape(n_exp, 1, ff2), w_dn, bdn.reshape(n_exp, 1, d))


def _combine_kernel(dest_ref, g_ref, x_ref, gt_ref, y_hbm, o_ref, buf, sem):
    def issue(t, carry):
        for k in range(TOP_K):
            _row_copy(y_hbm.at[pl.ds(dest_ref[t * TOP_K + k], 1)], buf.at[k, pl.ds(t, 1)], sem).start()
        return carry

    lax.fori_loop(0, TOK_DMA, issue, 0)

    def drain(t, carry):
        for k in range(TOP_K):
            _row_copy(y_hbm.at[pl.ds(0, 1)], buf.at[k, pl.ds(0, 1)], sem).wait()
        return carry

    lax.fori_loop(0, TOK_DMA, drain, 0)
    g = g_ref[...]
    acc = g[:, 0:1] * buf[0]
    for k in range(1, TOP_K):
        acc = acc + g[:, k:k + 1] * buf[k]
    o_ref[...] = x_ref[...] + gt_ref[...] * acc


def _combine(dest_flat, gates, x, mod3, y, n_lat):
    np_, d = x.shape
    nlt = n_lat // TOK_DMA
    return pl.pallas_call(
        _combine_kernel,
        out_shape=SDS((np_, d), F32),
        grid=(np_ // TOK_DMA,),
        in_specs=[BS((TOK_DMA * TOP_K,), lambda i: (i,), memory_space=pltpu.SMEM),
                  BS((TOK_DMA, LANES), lambda i: (i, 0)),
                  BS((TOK_DMA, d), lambda i: (i, 0)),
                  BS((None, 1, d), lambda i: (jnp.where(i < nlt, 0, 1), 0, 5)),
                  BS(memory_space=pl.ANY)],
        out_specs=BS((TOK_DMA, d), lambda i: (i, 0)),
        scratch_shapes=[pltpu.VMEM((TOP_K, TOK_DMA, d), F32), pltpu.SemaphoreType.DMA(())],
        compiler_params=_cp("arbitrary"),
        name="moe_combine",
    )(dest_flat, gates, x, mod3, y)


def _moe(x, gain, mod3, router_w, router_b, w_gu, bgu, w_dn, bdn, n_lat, n_tok):
    np_, d = x.shape
    n_exp = router_w.shape[1]
    wr_pad = jnp.zeros((d, LANES), F32).at[:, :n_exp].set(router_w)
    br_pad = jnp.full((1, LANES), NEG_INF, F32).at[0, :n_exp].set(router_b)
    h, idx, gates = _router(x, gain, mod3, wr_pad, br_pad, n_lat)
    rank, counts = _expert_ranks(idx, n_tok)
    counts = counts[0, :n_exp]
    padded = (counts + MOE_BLK - 1) // MOE_BLK * MOE_BLK
    pad_end = jnp.cumsum(padded)
    pad_start = pad_end - padded
    dest = pad_start[idx[:, :TOP_K]] + rank[:, :TOP_K]
    dest = jnp.where(jnp.arange(np_)[:, None] < n_tok, dest, 0).astype(I32).reshape(-1)
    n_blocks = -(-(n_tok * TOP_K + n_exp * (MOE_BLK - 1)) // MOE_BLK)
    blk_row = jnp.arange(n_blocks, dtype=I32) * MOE_BLK
    blk_expert = jnp.minimum(jnp.sum(pad_end[None, :] <= blk_row[:, None], axis=1), n_exp - 1).astype(I32)
    n_used = (pad_end[-1:] // MOE_BLK).astype(I32)
    xs = _dispatch(dest, h, n_tok, n_blocks * MOE_BLK)
    y = _expert_ffn(blk_expert, n_used, xs, w_gu, bgu, w_dn, bdn)
    return _combine(dest, gates, x, mod3, y, n_lat)


def _attn_layer(x, mod3, norm_gain, w_qkv, w_o, q_gain, k_gain, sinks, cos2, sin2, n_lat, n_ctx, windowed):
    d = x.shape[1]
    qkv = _qkv(x, norm_gain, mod3, w_qkv.astype(BF16), q_gain, k_gain, cos2, sin2, n_lat)
    o = _attention(qkv, sinks, n_lat, n_ctx, windowed)
    return _proj_res(o, w_o.astype(BF16), jnp.zeros((d,), F32), x, mod3, n_lat, 2)


def _linear_norm_kernel(x_ref, gain_ref, sc_ref, sh_ref, w_ref, b_ref, o_ref, h_scr):
    @pl.when(pl.program_id(1) == 0)
    def _():
        h_scr[...] = _norm_mod(x_ref[...], gain_ref[...], sc_ref[...], sh_ref[...]).astype(BF16)

    o_ref[...] = (jnp.dot(h_scr[...], w_ref[...], preferred_element_type=F32) + b_ref[...]).astype(o_ref.dtype)


def _linear_norm(x, gain, mod3, w_bf, bias, n_lat, out_dtype):
    np_, d = x.shape
    width = w_bf.shape[1]
    tn = 1024
    nlt = n_lat // TM
    return pl.pallas_call(
        _linear_norm_kernel,
        out_shape=SDS((np_, width), out_dtype),
        grid=(np_ // TM, width // tn),
        in_specs=[BS((TM, d), lambda i, j: (i, 0)),
                  BS((1, d), lambda i, j: (0, 0)),
                  _mod_spec_full(nlt, d, 1),
                  _mod_spec_full(nlt, d, 0),
                  BS((d, tn), lambda i, j: (0, j)),
                  BS((1, tn), lambda i, j: (0, j))],
        out_specs=BS((TM, tn), lambda i, j: (i, j)),
        scratch_shapes=[pltpu.VMEM((TM, d), BF16)],
        compiler_params=_cp("parallel", "arbitrary"),
        name="linear_norm",
    )(x, gain.reshape(1, d), mod3, mod3, w_bf, bias.reshape(1, width))


HY_NB = 128
HY_SUB = 8
HY_BANDS = 8
HY_MOD_SHIFT = 0.05
HALO = 16


def _short_conv_kernel(z0c, z0p, z0n, z1c, z1p, z1n, z2c, z2p, z2n, w_ref, b_ref, x0_ref, u_ref,
                       *, n_lat_tiles, n_ctx):
    i = pl.program_id(0)
    row = lax.broadcasted_iota(I32, (TM, 1), 0)
    is_ctx = i >= n_lat_tiles
    has_prev = jnp.logical_and(i > 0, i != n_lat_tiles)
    has_next = i < n_lat_tiles - 1
    keep = jnp.logical_or(jnp.logical_not(is_ctx), row < n_ctx)

    def conv(cur, prv, nxt, part):
        zc = jnp.where(keep, cur[...].astype(F32), 0.0)
        first = jnp.where(has_prev, prv[HALO - 1:HALO, :].astype(F32), 0.0)
        last = jnp.where(has_next, nxt[0:1, :].astype(F32), 0.0)
        up = jnp.where(row == 0, first, pltpu.roll(zc, 1, axis=0))
        dn = jnp.where(row == TM - 1, last, pltpu.roll(zc, TM - 1, axis=0))
        w = w_ref[part]
        return up * w[0:1, :] + zc * w[1:2, :] + dn * w[2:3, :] + b_ref[part]

    x0 = conv(z0c, z0p, z0n, 0)
    x1 = conv(z1c, z1p, z1n, 1)
    v = conv(z2c, z2p, z2n, 2)
    x0_ref[...] = x0.astype(BF16)
    u_ref[...] = x1 * v


def _short_conv(z, w_short, b_short, n_lat, n_ctx):
    np_, d3 = z.shape
    d = d3 // 3
    tn = 512
    per = d // tn
    nlt = n_lat // TM
    hb = TM // HALO
    last_halo = np_ // HALO - 1
    specs = []
    for part in range(3):
        specs.append(BS((TM, tn), lambda i, j, part=part: (i, part * per + j)))
        specs.append(BS((HALO, tn), lambda i, j, part=part: (jnp.maximum(i * hb - 1, 0), part * per + j)))
        specs.append(BS((HALO, tn), lambda i, j, part=part: (jnp.minimum((i + 1) * hb, last_halo), part * per + j)))
    w3 = w_short.reshape(3, 3, d).transpose(1, 0, 2)
    b3 = b_short.reshape(3, 1, d)
    return pl.pallas_call(
        functools.partial(_short_conv_kernel, n_lat_tiles=nlt, n_ctx=n_ctx),
        out_shape=(SDS((np_, d), BF16), SDS((np_, d), F32)),
        grid=(np_ // TM, per),
        in_specs=specs + [BS((3, 3, tn), lambda i, j: (0, 0, j)), BS((3, 1, tn), lambda i, j: (0, 0, j))],
        out_specs=(BS((TM, tn), lambda i, j: (i, j)), BS((TM, tn), lambda i, j: (i, j))),
        compiler_params=_cp("parallel", "parallel"),
        name="hyena_short_conv",
    )(z, z, z, z, z, z, z, z, z, w3, b3)


def _filter_kernel(w1_ref, b1_ref, w2_ref, b2_ref, sf_ref, w3f_ref, b3f_ref, w3b_ref, b3b_ref, dec_ref, o_ref,
                   hid_scr, *, n, rows, nsub, s_mul, s_sub, s_blk):
    blk = pl.program_id(0)
    a = lax.broadcasted_iota(I32, (rows, 1), 0)
    lane = lax.broadcasted_iota(I32, (rows, LANES), 1)

    def positions(sub):
        s = s_mul * a + (s_sub * sub + s_blk * blk)
        lag = jnp.where(s < n, s, 2 * n - s)
        return s, lag.astype(F32) / float(n)

    @pl.when(pl.program_id(1) == 0)
    def _():
        band = jnp.where(lane <= HY_BANDS, lane, HY_BANDS - lane).astype(F32)
        phase = jnp.where(lane <= HY_BANDS, 0.5 * jnp.pi, 0.0)
        for sub in range(nsub):
            _, t = positions(sub)
            trig = jnp.sin(2.0 * jnp.pi * t * band + phase)
            feats = jnp.where(lane == 0, t, jnp.where(lane <= 2 * HY_BANDS, trig, 0.0))
            hid = jnp.sin(sf_ref[0:1, :] * (_dot_f32(feats, w1_ref[...]) + b1_ref[...]))
            hid_scr[sub] = jnp.sin(sf_ref[1:2, :] * (_dot_f32(hid, w2_ref[...]) + b2_ref[...]))

    w3f, w3b = w3f_ref[...].astype(BF16), w3b_ref[...].astype(BF16)
    for sub in range(nsub):
        s, t = positions(sub)
        hid = hid_scr[sub].astype(BF16)
        fwd = jnp.dot(hid, w3f, preferred_element_type=F32) + b3f_ref[...]
        bwd = jnp.dot(hid, w3b, preferred_element_type=F32) + b3b_ref[...]
        window = jnp.exp(-t * jnp.abs(dec_ref[...])) + HY_MOD_SHIFT
        g = jnp.where(s < n, fwd, jnp.where(s == n, 0.0, bwd)) * window
        if nsub == 1:
            o_ref[...] = g
        else:
            o_ref[:, sub, :] = g


def _hyena_filter(n, fw1, fb1, fw2, fb2, fw3, fb3, sin_freq, decay, two_d):
    width = fw1.shape[1]
    d = decay.shape[0]
    tn = 512
    per = d // tn
    pad = lambda m, r, c: jnp.zeros((r, c), F32).at[:m.shape[0], :m.shape[1]].set(m)
    w1p = pad(fw1, LANES, LANES)
    b1p = pad(fb1.reshape(1, width), 1, LANES)
    w2p = pad(fw2, LANES, LANES)
    b2p = pad(fb2.reshape(1, width), 1, LANES)
    sfp = pad(sin_freq, 2, LANES)
    w3p = pad(fw3, LANES, 2 * d)
    b3 = fb3.reshape(1, 2 * d)
    if two_d:
        rows, nsub, nblk = 2 * n // HY_NB, HY_SUB, HY_NB // HY_SUB
        s_mul, s_sub, s_blk = HY_NB, 1, HY_SUB
        out_shape = SDS((rows, HY_NB, d), F32)
        out_spec = BS((rows, HY_SUB, tn), lambda b, j: (0, b, j))
    else:
        rows, nsub = min(2 * n, 256), 1
        nblk, s_mul, s_sub, s_blk = 2 * n // rows, 1, 0, rows
        out_shape = SDS((2 * n, d), F32)
        out_spec = BS((rows, tn), lambda b, j: (b, j))
    const = lambda r, c: BS((r, c), lambda b, j: (0, 0))
    return pl.pallas_call(
        functools.partial(_filter_kernel, n=n, rows=rows, nsub=nsub, s_mul=s_mul, s_sub=s_sub, s_blk=s_blk),
        out_shape=out_shape,
        grid=(nblk, per),
        in_specs=[const(LANES, LANES), const(1, LANES), const(LANES, LANES), const(1, LANES), const(2, LANES),
                  BS((LANES, tn), lambda b, j: (0, j)), BS((1, tn), lambda b, j: (0, j)),
                  BS((LANES, tn), lambda b, j: (0, per + j)), BS((1, tn), lambda b, j: (0, per + j)),
                  BS((1, tn), lambda b, j: (0, j))],
        out_specs=out_spec,
        scratch_shapes=[pltpu.VMEM((nsub, rows, LANES), F32)],
        compiler_params=_cp("parallel", "arbitrary"),
        name="hyena_filter",
    )(w1p, b1p, w2p, b2p, sfp, w3p, b3, w3p, b3, decay.reshape(1, d))


def _dft_tables(n_total):
    nb = HY_NB
    na = n_total // nb
    ka = np.arange(na)[:, None]
    a = np.arange(na)[None, :]
    ang_a = 2.0 * np.pi * ((ka * a) % na) / na
    fa = np.concatenate([np.cos(ang_a), -np.sin(ang_a)], axis=0)
    kb = np.arange(nb)[:, None]
    b = np.arange(nb)[None, :]
    ang_b = 2.0 * np.pi * ((kb * b) % nb) / nb
    c, s = np.cos(ang_b), np.sin(ang_b)
    fwd_b = np.block([[c, s], [-s, c]])
    inv_b = np.block([[c, -s], [s, c]]) / float(n_total)
    ang_t = 2.0 * np.pi * ((np.arange(nb)[:, None] * np.arange(na)[None, :]) % n_total) / n_total
    tw = (np.cos(ang_t), -np.sin(ang_t))
    inv_a = np.concatenate([np.cos(ang_a[:na // 2]), -np.sin(ang_a[:na // 2])], axis=1)
    f = lambda m: jnp.asarray(m, F32)
    return dict(na=na, fa=f(fa), fwd_b=f(fwd_b), inv_b=f(inv_b), tw_re=f(tw[0]), tw_im=f(tw[1]), inv_a=f(inv_a))


def _dft_a_kernel(f_ref, x_ref, twr_ref, twi_ref, re_ref, im_ref, *, na):
    f = f_ref[...]
    for sub in range(HY_SUB):
        pq = jnp.dot(f, x_ref[:, sub, :].astype(BF16), preferred_element_type=F32)
        p, q = pq[:na], pq[na:]
        twr, twi = twr_ref[sub], twi_ref[sub]
        re_ref[:, sub, :] = p * twr - q * twi
        im_ref[:, sub, :] = p * twi + q * twr


def _dft_a(x3, f_bf, tw_re, tw_im):
    ka_rows, nb, d = x3.shape
    na = f_bf.shape[0] // 2
    tn = min(d, 512)
    twr = tw_re.reshape(nb, na, 1)
    twi = tw_im.reshape(nb, na, 1)
    tws = BS((HY_SUB, na, 1), lambda b, j: (b, 0, 0))
    out = BS((na, HY_SUB, tn), lambda b, j: (0, b, j))
    return pl.pallas_call(
        functools.partial(_dft_a_kernel, na=na),
        out_shape=(SDS((na, nb, d), F32), SDS((na, nb, d), F32)),
        grid=(nb // HY_SUB, d // tn),
        in_specs=[BS((2 * na, ka_rows), lambda b, j: (0, 0)),
                  BS((ka_rows, HY_SUB, tn), lambda b, j: (0, b, j)),
                  tws, tws],
        out_specs=(out, out),
        compiler_params=_cp("parallel", "parallel"),
        name="dft_stage_a",
    )(f_bf, x3, twr, twi)


def _dft_b_kernel(fwd_ref, inv_ref, ur_ref, ui_ref, gr_ref, gi_ref, twr_ref, twi_ref, br_ref, bi_ref):
    nb = HY_NB
    fwd = fwd_ref[...]
    stack = lambda re, im: jnp.concatenate([re[...], im[...]], axis=0).astype(BF16)
    us = jnp.dot(fwd, stack(ur_ref, ui_ref), preferred_element_type=F32)
    gs = jnp.dot(fwd, stack(gr_ref, gi_ref), preferred_element_type=F32)
    ure, uim, gre, gim = us[:nb], us[nb:], gs[:nb], gs[nb:]
    yre = ure * gre - uim * gim
    yim = ure * gim + uim * gre
    ys = jnp.concatenate([yre, yim], axis=0).astype(BF16)
    bs = jnp.dot(inv_ref[...], ys, preferred_element_type=F32)
    bre, bim = bs[:nb], bs[nb:]
    twr, twi = twr_ref[...], twi_ref[...]
    br_ref[...] = bre * twr + bim * twi
    bi_ref[...] = bim * twr - bre * twi


def _dft_b(ur, ui, gr, gi, fwd_bf, inv_bf, tw_re, tw_im):
    na, nb, d = ur.shape
    tn = min(d, 2048)
    twr = tw_re.T.reshape(na, nb, 1)
    twi = tw_im.T.reshape(na, nb, 1)
    data = BS((None, nb, tn), lambda k, j: (k, 0, j))
    tws = BS((None, nb, 1), lambda k, j: (k, 0, 0))
    return pl.pallas_call(
        _dft_b_kernel,
        out_shape=(SDS((na, nb, d), F32), SDS((na, nb, d), F32)),
        grid=(na, d // tn),
        in_specs=[BS((2 * nb, 2 * nb), lambda k, j: (0, 0)), BS((2 * nb, 2 * nb), lambda k, j: (0, 0)),
                  data, data, data, data, tws, tws],
        out_specs=(data, data),
        compiler_params=_cp("parallel", "parallel"),
        name="dft_stage_b",
    )(fwd_bf, inv_bf, ur, ui, gr, gi, twr, twi)


def _idft_a_kernel(f_ref, br_ref, bi_ref, o_ref):
    f = f_ref[...]
    for sub in range(HY_SUB):
        bs = jnp.concatenate([br_ref[:, sub, :], bi_ref[:, sub, :]], axis=0).astype(BF16)
        o_ref[:, sub, :] = jnp.dot(f, bs, preferred_element_type=F32)


def _idft_a(br, bi, inv_a_bf):
    na, nb, d = br.shape
    tn = min(d, 512)
    data = BS((na, HY_SUB, tn), lambda b, j: (0, b, j))
    return pl.pallas_call(
        _idft_a_kernel,
        out_shape=SDS((na // 2, nb, d), F32),
        grid=(nb // HY_SUB, d // tn),
        in_specs=[BS((na // 2, 2 * na), lambda b, j: (0, 0)), data, data],
        out_specs=BS((na // 2, HY_SUB, tn), lambda b, j: (0, b, j)),
        compiler_params=_cp("parallel", "parallel"),
        name="idft_stage_a",
    )(inv_a_bf, br, bi)


def _ctx_conv_kernel(g_ref, u_ref, o_ref, gg_scr, *, n):
    g = g_ref[...]
    gg_scr[0:2 * n, :] = g
    gg_scr[2 * n:4 * n, :] = g

    def body(s, acc):
        return acc + gg_scr[pl.ds(2 * n - s, n), :] * u_ref[pl.ds(s, 1), :]

    o_ref[...] = lax.fori_loop(0, n, body, jnp.zeros((n, LANES), F32))


def _ctx_conv(g, u):
    n2, d = g.shape
    n = n2 // 2
    return pl.pallas_call(
        functools.partial(_ctx_conv_kernel, n=n),
        out_shape=SDS((n, d), F32),
        grid=(d // LANES,),
        in_specs=[BS((n2, LANES), lambda j: (0, j)), BS((n, LANES), lambda j: (0, j))],
        out_specs=BS((n, LANES), lambda j: (0, j)),
        scratch_shapes=[pltpu.VMEM((2 * n2, LANES), F32)],
        compiler_params=_cp("parallel"),
        name="hyena_ctx_conv",
    )(g, u)


def _gated_proj_res_kernel(a_ref, y_ref, u_ref, skip_ref, w_ref, b_ref, x_ref, gt_ref, o_ref):
    a = (a_ref[...].astype(F32) * (y_ref[...] + u_ref[...] * skip_ref[...])).astype(BF16)
    y = jnp.dot(a, w_ref[...], preferred_element_type=F32) + b_ref[...]
    o_ref[...] = x_ref[...] + gt_ref[...] * y


def _gated_proj_res(a, y, u, skip, w_bf, bias, x, mod3, n_lat, gate_chunk):
    np_, d = x.shape
    kdim = a.shape[1]
    tn = 1024
    nlt = n_lat // TM
    rows = BS((TM, kdim), lambda i, j: (i, 0))
    return pl.pallas_call(
        _gated_proj_res_kernel,
        out_shape=SDS((np_, d), F32),
        grid=(np_ // TM, d // tn),
        in_specs=[rows, rows, rows,
                  BS((1, kdim), lambda i, j: (0, 0)),
                  BS((kdim, tn), lambda i, j: (0, j)),
                  BS((1, tn), lambda i, j: (0, j)),
                  BS((TM, tn), lambda i, j: (i, j)),
                  _mod_spec(nlt, d, tn, gate_chunk)],
        out_specs=BS((TM, tn), lambda i, j: (i, j)),
        compiler_params=_cp("parallel", "parallel"),
        name="gated_proj_residual",
    )(a, y, u, skip.reshape(1, kdim), w_bf, bias.reshape(1, d), x, mod3)


def _hyena_layer(x, mod3, norm_gain, w_in, b_in, w_short, b_short, fw1, fb1, fw2, fb2, fw3, fb3, sin_freq, decay,
                 skip, w_out, b_out, n_lat, n_ctx):
    np_, d = x.shape
    z = _linear_norm(x, norm_gain, mod3, w_in.astype(BF16), b_in, n_lat, BF16)
    x0, u = _short_conv(z, w_short, b_short, n_lat, n_ctx)
    tabs = _dft_tables(2 * n_lat)
    na = tabs["na"]
    g3 = _hyena_filter(n_lat, fw1, fb1, fw2, fb2, fw3, fb3, sin_freq, decay, True)
    u3 = u[:n_lat].reshape(na // 2, HY_NB, d)
    fa_bf = tabs["fa"].astype(BF16)
    ur, ui = _dft_a(u3, fa_bf[:, :na // 2], tabs["tw_re"], tabs["tw_im"])
    gr, gi = _dft_a(g3, fa_bf, tabs["tw_re"], tabs["tw_im"])
    br, bi = _dft_b(ur, ui, gr, gi, tabs["fwd_b"].astype(BF16), tabs["inv_b"].astype(BF16),
                    tabs["tw_re"], tabs["tw_im"])
    y_lat = _idft_a(br, bi, tabs["inv_a"].astype(BF16)).reshape(n_lat, d)
    g_ctx = _hyena_filter(n_ctx, fw1, fb1, fw2, fb2, fw3, fb3, sin_freq, decay, False)
    y_ctx = _ctx_conv(g_ctx, u[n_lat:n_lat + n_ctx])
    y = jnp.concatenate([y_lat, y_ctx, jnp.zeros((np_ - n_lat - n_ctx, d), F32)], axis=0)
    return _gated_proj_res(x0, y, u, skip, w_out.astype(BF16), b_out, x, mod3, n_lat, 2)


S5_T = 16
S5_GS = 16
S5_P = 64
S5_GB = 8


def _s5_discretize(lr_row, li_row, dt):
    mag = jnp.exp(lr_row * dt)
    a_re, a_im = mag * jnp.cos(li_row * dt), mag * jnp.sin(li_row * dt)
    xr, xi = a_re - 1.0, a_im
    den = 1.0 / (lr_row * lr_row + li_row * li_row)
    return (xr * lr_row + xi * li_row) * den, (xi * lr_row - xr * li_row) * den


def _s5_power_c(lr_col, li_col, dt, cre_t, cim_t, tau):
    width = S5_T * S5_GS
    lane = lax.broadcasted_iota(I32, (S5_GS, width), 1)
    expand = jnp.where(lane % S5_GS == lax.broadcasted_iota(I32, (S5_GS, width), 0), 1.0, 0.0)
    c_re = _dot_f32(cre_t, expand)
    c_im = _dot_f32(cim_t, expand)
    mag = jnp.exp(tau * (lr_col * dt))
    p_re, p_im = mag * jnp.cos(tau * (li_col * dt)), mag * jnp.sin(tau * (li_col * dt))
    z_re = p_re * c_re - p_im * c_im
    z_im = p_re * c_im + p_im * c_re
    return jnp.concatenate([z_re, -z_im], axis=0)


def _fill_group_perm(perm_scr, transposed):
    n = S5_GB * S5_T * S5_GS
    rows_per = 256
    for b in range(n // rows_per):
        r = b * rows_per + lax.broadcasted_iota(I32, (rows_per, n), 0)
        q = lax.broadcasted_iota(I32, (rows_per, n), 1)
        grouped, stream = (q, r) if transposed else (r, q)
        j = grouped // (S5_T * S5_GS)
        s = (grouped // S5_GS) % S5_T
        c = grouped % S5_GS
        hit = stream == s * (S5_GB * S5_GS) + j * S5_GS + c
        perm_scr[b * rows_per:(b + 1) * rows_per, :] = jnp.where(hit, 1.0, 0.0).astype(BF16)


def _s5_chunk_kernel(h_ref, lr_row, li_row, lr_col, li_col, ls_ref, bre_t, bim_t, cre_t, cim_t,
                     y_ref, s_ref, zc_ref, mt_scr, u_scr, perm_scr):
    dd = pl.program_id(1)
    width = S5_T * S5_GS

    @pl.when(jnp.logical_and(pl.program_id(0) == 0, dd == 0))
    def _():
        _fill_group_perm(perm_scr, True)

    @pl.when(dd == 0)
    def _():
        hcat = jnp.concatenate([h_ref[:, s, :].astype(BF16) for s in range(S5_T)], axis=1)
        u_scr[...] = jnp.dot(hcat, perm_scr[...], preferred_element_type=F32).astype(BF16)

    lane = lax.broadcasted_iota(I32, (S5_GS, width), 1)
    t_lane = lax.broadcasted_iota(I32, (1, width), 1) // S5_GS
    t_row = lax.broadcasted_iota(I32, (width, 1), 0) // S5_GS
    row = lax.broadcasted_iota(I32, (width, S5_GS), 0)
    expand_t = jnp.where(row % S5_GS == lax.broadcasted_iota(I32, (width, S5_GS), 1), 1.0, 0.0)
    tau_k = jnp.where(dd == 0, t_lane, S5_T - 1 - t_lane).astype(F32)
    tau_g = jnp.where(dd == 0, S5_T - 1 - t_row, t_row).astype(F32)
    for j in range(S5_GB):
        dt = jnp.exp(ls_ref[j])
        q_re, q_im = _s5_discretize(lr_row[j], li_row[j], dt)
        bb_re = q_re * bre_t[j] - q_im * bim_t[j]
        bb_im = q_re * bim_t[j] + q_im * bre_t[j]
        z = _s5_power_c(lr_col[j], li_col[j], dt, cre_t[j], cim_t[j], tau_k)
        k0 = _dot_f32(jnp.concatenate([bb_re, bb_im], axis=1), z)
        a_mag = jnp.exp(lr_col[j] * dt)
        a_re, a_im = a_mag * jnp.cos(li_col[j] * dt), a_mag * jnp.sin(li_col[j] * dt)
        z_re, z_nim = z[:S5_P], z[S5_P:]
        zc_ref[j] = jnp.concatenate([a_re * z_re + a_im * z_nim, a_re * z_nim - a_im * z_re], axis=0).astype(BF16)

        @pl.when(dd == 0)
        def _():
            for s in range(S5_T):
                blk = k0 if s == 0 else jnp.where(lane >= s * S5_GS, pltpu.roll(k0, s * S5_GS, axis=1), 0.0)
                mt_scr[s * S5_GS:(s + 1) * S5_GS, :] = blk.astype(BF16)

        @pl.when(dd == 1)
        def _():
            for s in range(S5_T):
                sh = (S5_T - 1 - s) * S5_GS
                blk = k0 if sh == 0 else jnp.where(lane < (s + 1) * S5_GS, pltpu.roll(k0, width - sh, axis=1), 0.0)
                mt_scr[s * S5_GS:(s + 1) * S5_GS, :] = blk.astype(BF16)

        u = u_scr[:, j * width:(j + 1) * width]
        y_ref[j] = jnp.dot(u, mt_scr[...], preferred_element_type=F32).astype(BF16)
        be_re = _dot_f32(expand_t, bb_re)
        be_im = _dot_f32(expand_t, bb_im)
        mag = jnp.exp(tau_g * (lr_row[j] * dt))
        p_re, p_im = mag * jnp.cos(tau_g * (li_row[j] * dt)), mag * jnp.sin(tau_g * (li_row[j] * dt))
        g = jnp.concatenate([p_re * be_re - p_im * be_im, p_re * be_im + p_im * be_re], axis=1)
        s_ref[:, j, :] = jnp.dot(u, g.astype(BF16), preferred_element_type=F32)


def _s5_chunks(h3, prm):
    nc, _, d = h3.shape
    groups, width = d // S5_GS, S5_T * S5_GS
    gl = S5_GB * S5_GS
    par = lambda r, c: BS((None, S5_GB, r, c), lambda i, dd: (dd, i, 0, 0))
    row, col, one = par(1, S5_P), par(S5_P, 1), par(1, 1)
    bt, ct = par(S5_GS, S5_P), par(S5_P, S5_GS)
    return pl.pallas_call(
        _s5_chunk_kernel,
        out_shape=(SDS((2, groups, nc, width), BF16), SDS((2, nc, groups, 2 * S5_P), F32),
                   SDS((2, groups, 2 * S5_P, width), BF16)),
        grid=(groups // S5_GB, 2),
        in_specs=[BS((nc, S5_T, gl), lambda i, dd: (0, 0, i)), row, row, col, col, one, bt, bt, ct, ct],
        out_specs=(BS((None, S5_GB, nc, width), lambda i, dd: (dd, i, 0, 0)),
                   BS((None, nc, S5_GB, 2 * S5_P), lambda i, dd: (dd, 0, i, 0)),
                   BS((None, S5_GB, 2 * S5_P, width), lambda i, dd: (dd, i, 0, 0))),
        scratch_shapes=[pltpu.VMEM((width, width), BF16), pltpu.VMEM((nc, S5_GB * width), BF16),
                        pltpu.VMEM((S5_GB * width, S5_GB * width), BF16)],
        compiler_params=_cp("arbitrary", "arbitrary"),
        name="s5_chunks",
    )(h3, prm["lr_row"], prm["li_row"], prm["lr_col"], prm["li_col"], prm["ls"], prm["bre_t"], prm["bim_t"],
      prm["cre_t"], prm["cim_t"])


def _s5_state_kernel(s0_ref, s1_ref, lr_ref, li_ref, ls_ref, h0_ref, h1_ref, st_scr, *, steps):
    @pl.when(pl.program_id(0) == 0)
    def _():
        st_scr[...] = jnp.zeros(st_scr.shape, F32)

    def coeffs(dd):
        dt = jnp.exp(ls_ref[dd])
        mag = jnp.exp(float(S5_T) * lr_ref[dd] * dt)
        ang = float(S5_T) * li_ref[dd] * dt
        a_re, a_im = mag * jnp.cos(ang), mag * jnp.sin(ang)
        lane = lax.broadcasted_iota(I32, a_im.shape, 1)
        a_sw = jnp.where(lane < S5_P, -a_im, a_im)
        return a_re, a_sw

    def run(dd, s_ref, h_ref, order):
        a_re, a_sw = coeffs(dd)
        h, hs = st_scr[2 * dd], st_scr[2 * dd + 1]
        for c in order:
            h_ref[c] = h
            s = s_ref[c]
            s_swapped = pltpu.roll(s, S5_P, axis=1)
            h, hs = a_re * h + a_sw * hs + s, a_re * hs - a_sw * h + s_swapped
        st_scr[2 * dd] = h
        st_scr[2 * dd + 1] = hs

    run(0, s0_ref, h0_ref, range(steps))
    run(1, s1_ref, h1_ref, reversed(range(steps)))


def _s5_states(s_t, prm, first_chunk):
    _, nc, groups, w = s_t.shape
    steps = 8
    nblk = nc // steps
    off = first_chunk // steps
    fwd = BS((None, steps, groups, w), lambda c: (0, (c + off) % nblk, 0, 0))
    bwd = BS((None, steps, groups, w), lambda c: (1, nblk - 1 - c, 0, 0))
    fwd_o = BS((steps, groups, w), lambda c: ((c + off) % nblk, 0, 0))
    bwd_o = BS((steps, groups, w), lambda c: (nblk - 1 - c, 0, 0))
    par = lambda last: BS((2, groups, last), lambda c: (0, 0, 0))
    return pl.pallas_call(
        functools.partial(_s5_state_kernel, steps=steps),
        out_shape=(SDS((nc, groups, w), F32), SDS((nc, groups, w), F32)),
        grid=(nblk,),
        in_specs=[fwd, bwd, par(w), par(w), par(1)],
        out_specs=(fwd_o, bwd_o),
        scratch_shapes=[pltpu.VMEM((4, groups, w), F32)],
        compiler_params=_cp("arbitrary"),
        name="s5_state_scan",
    )(s_t, s_t, prm["lr2"], prm["li2"], prm["ls_g"])


def _s5_output_kernel(y1_ref, h0_ref, h1_ref, zc_ref, y_ref, perm_scr):
    @pl.when(jnp.logical_and(pl.program_id(0) == 0, pl.program_id(1) == 0))
    def _():
        _fill_group_perm(perm_scr, False)

    parts = []
    for j in range(S5_GB):
        z = jnp.concatenate([zc_ref[0, j], zc_ref[1, j]], axis=0)
        h = jnp.concatenate([h0_ref[:, j, :], h1_ref[:, j, :]], axis=1).astype(BF16)
        yj = y1_ref[0, j].astype(F32) + y1_ref[1, j].astype(F32) + jnp.dot(h, z, preferred_element_type=F32)
        parts.append(yj.astype(BF16))
    ynat = jnp.dot(jnp.concatenate(parts, axis=1), perm_scr[...], preferred_element_type=F32)
    gl = S5_GB * S5_GS
    for s in range(S5_T):
        y_ref[:, s, :] = ynat[:, s * gl:(s + 1) * gl]


def _s5_outputs(y1, h0, h1, zc):
    _, groups, nc, width = y1.shape
    halves = 2
    nch = nc // halves
    gl = S5_GB * S5_GS
    states = BS((nch, S5_GB, 2 * S5_P), lambda i, c: (c, i, 0))
    return pl.pallas_call(
        _s5_output_kernel,
        out_shape=SDS((nc, S5_T, groups * S5_GS), F32),
        grid=(groups // S5_GB, halves),
        in_specs=[BS((2, S5_GB, nch, width), lambda i, c: (0, i, c, 0)), states, states,
                  BS((2, S5_GB, 2 * S5_P, width), lambda i, c: (0, i, 0, 0))],
        out_specs=BS((nch, S5_T, gl), lambda i, c: (c, 0, i)),
        scratch_shapes=[pltpu.VMEM((S5_GB * width, S5_GB * width), BF16)],
        compiler_params=_cp("arbitrary", "arbitrary"),
        name="s5_outputs",
    )(y1, h0, h1, zc)


def _s5_glu_kernel(x_ref, gain_ref, sc_ref, sh_ref, ys_ref, skip_ref, wv_ref, wg_ref, bv_ref, bg_ref,
                   xo_ref, gt_ref, o_ref, z_scr):
    @pl.when(pl.program_id(1) == 0)
    def _():
        h = _norm_mod(x_ref[...], gain_ref[...], sc_ref[...], sh_ref[...])
        z_scr[...] = jax.nn.gelu(skip_ref[...] * h + ys_ref[...]).astype(BF16)

    z = z_scr[...]
    val = jnp.dot(z, wv_ref[...], preferred_element_type=F32) + bv_ref[...]
    gate = jnp.dot(z, wg_ref[...], preferred_element_type=F32) + bg_ref[...]
    o_ref[...] = xo_ref[...] + gt_ref[...] * (val * (1.0 / (1.0 + jnp.exp(-gate))))


def _s5_glu(x, gain, mod3, ys, skip, w_bf, bias, n_lat):
    np_, d = x.shape
    tn = 512
    per = d // tn
    nlt = n_lat // TM
    b2 = bias.reshape(1, 2 * d)
    return pl.pallas_call(
        _s5_glu_kernel,
        out_shape=SDS((np_, d), F32),
        grid=(np_ // TM, per),
        in_specs=[BS((TM, d), lambda i, j: (i, 0)),
                  BS((1, d), lambda i, j: (0, 0)),
                  _mod_spec_full(nlt, d, 1),
                  _mod_spec_full(nlt, d, 0),
                  BS((TM, d), lambda i, j: (i, 0)),
                  BS((1, d), lambda i, j: (0, 0)),
                  BS((d, tn), lambda i, j: (0, j)),
                  BS((d, tn), lambda i, j: (0, per + j)),
                  BS((1, tn), lambda i, j: (0, j)),
                  BS((1, tn), lambda i, j: (0, per + j)),
                  BS((TM, tn), lambda i, j: (i, j)),
                  _mod_spec(nlt, d, tn, 2)],
        out_specs=BS((TM, tn), lambda i, j: (i, j)),
        scratch_shapes=[pltpu.VMEM((TM, d), BF16)],
        compiler_params=_cp("parallel", "arbitrary"),
        name="s5_glu",
    )(x, gain.reshape(1, d), mod3, mod3, ys, skip.reshape(1, d), w_bf, w_bf, b2, b2, x, mod3)


def _norm_mod_rows_kernel(x_ref, gain_ref, sc_ref, sh_ref, o_ref):
    o_ref[...] = _norm_mod(x_ref[...], gain_ref[...], sc_ref[...], sh_ref[...])


def _norm_mod_rows(x, gain, mod3, n_lat):
    np_, d = x.shape
    nlt = n_lat // TM
    return pl.pallas_call(
        _norm_mod_rows_kernel,
        out_shape=SDS((np_, d), F32),
        grid=(np_ // TM, 1),
        in_specs=[BS((TM, d), lambda i, j: (i, 0)),
                  BS((1, d), lambda i, j: (0, 0)),
                  _mod_spec_full(nlt, d, 1),
                  _mod_spec_full(nlt, d, 0)],
        out_specs=BS((TM, d), lambda i, j: (i, 0)),
        compiler_params=_cp("parallel", "arbitrary"),
        name="norm_mod_rows",
    )(x, gain.reshape(1, d), mod3, mod3)


def _s5_layer(x, mod3, norm_gain, lam_re, lam_im, log_step, b_re, b_im, c_re, c_im, skip, w_glu, b_glu,
              n_lat, n_ctx):
    np_, d = x.shape
    groups = d // S5_GS
    n_tok = n_lat + n_ctx
    nc = n_tok // S5_T
    h = _norm_mod_rows(x, norm_gain, mod3, n_lat)
    h3 = h[:n_tok].reshape(nc, S5_T, d)
    prm = dict(
        lr_row=lam_re.reshape(2, groups, 1, S5_P), li_row=lam_im.reshape(2, groups, 1, S5_P),
        lr_col=lam_re.reshape(2, groups, S5_P, 1), li_col=lam_im.reshape(2, groups, S5_P, 1),
        ls=log_step.reshape(2, groups, 1, 1), ls_g=log_step.reshape(2, groups, 1),
        bre_t=b_re.transpose(0, 1, 3, 2), bim_t=b_im.transpose(0, 1, 3, 2),
        cre_t=c_re.transpose(0, 1, 3, 2), cim_t=c_im.transpose(0, 1, 3, 2),
        lr2=jnp.concatenate([lam_re, lam_re], axis=-1), li2=jnp.concatenate([lam_im, lam_im], axis=-1))
    y1, s, zc = _s5_chunks(h3, prm)
    h0, h1 = _s5_states(s, prm, n_lat // S5_T)
    y = _s5_outputs(y1, h0, h1, zc).reshape(n_tok, d)
    ys = jnp.concatenate([y, jnp.zeros((np_ - n_tok, d), F32)], axis=0)
    return _s5_glu(x, norm_gain, mod3, ys, skip, w_glu.astype(BF16), b_glu, n_lat)


def kernel(x, c, ctx, c_ctx, w_mod, b_mod, norm_mix, norm_ffn, router_w, router_b, w_gate_up, b_gate_up, w_down, b_down, win_w_qkv, win_w_o, win_q_gain, win_k_gain, win_sinks, full_w_qkv, full_w_o, full_q_gain, full_k_gain, hy_w_in, hy_b_in, hy_w_short, hy_b_short, hy_filt_w1, hy_filt_b1, hy_filt_w2, hy_filt_b2, hy_filt_w3, hy_filt_b3, hy_sin_freq, hy_decay, hy_skip, hy_w_out, hy_b_out, s5_lam_re, s5_lam_im, s5_log_step, s5_b_re, s5_b_im, s5_c_re, s5_c_im, s5_skip, s5_w_glu, s5_b_glu):
    batch, n_lat, d = x.shape
    n_ctx = ctx.shape[1]
    assert batch == 1 and n_lat % TM == 0 and n_ctx % TOK_DMA == 0 and (n_lat + n_ctx) % S5_T == 0
    depth = w_mod.shape[0]
    n_tok = n_lat + n_ctx
    pad = (-n_ctx) % TM
    cc = jnp.zeros((8, d), F32).at[0].set(c[0]).at[1].set(c_ctx)
    mod = _modulation(cc, w_mod, b_mod)
    xs = jnp.concatenate([x[0], ctx[0], jnp.zeros((pad, d), F32)], axis=0)
    cos2, sin2 = _rope_tables(n_lat)
    for i in range(depth):
        kind, j = i % 4, i // 4
        mod3 = mod[i, :2].reshape(2, 1, 6 * d)
        if kind == 0:
            xs = _attn_layer(xs, mod3, norm_mix[i], win_w_qkv[j], win_w_o[j], win_q_gain[j], win_k_gain[j],
                             win_sinks[j], cos2, sin2, n_lat, n_ctx, True)
        elif kind == 1:
            xs = _attn_layer(xs, mod3, norm_mix[i], full_w_qkv[j], full_w_o[j], full_q_gain[j], full_k_gain[j],
                             None, cos2, sin2, n_lat, n_ctx, False)
        elif kind == 2:
            xs = _hyena_layer(xs, mod3, norm_mix[i], hy_w_in[j], hy_b_in[j], hy_w_short[j], hy_b_short[j],
                              hy_filt_w1[j], hy_filt_b1[j], hy_filt_w2[j], hy_filt_b2[j], hy_filt_w3[j],
                              hy_filt_b3[j], hy_sin_freq[j], hy_decay[j], hy_skip[j], hy_w_out[j], hy_b_out[j],
                              n_lat, n_ctx)
        else:
            xs = _s5_layer(xs, mod3, norm_mix[i], s5_lam_re[j], s5_lam_im[j], s5_log_step[j], s5_b_re[j],
                           s5_b_im[j], s5_c_re[j], s5_c_im[j], s5_skip[j], s5_w_glu[j], s5_b_glu[j], n_lat, n_ctx)
        xs = _moe(xs, norm_ffn[i], mod3, router_w[i], router_b[i], w_gate_up[i], b_gate_up[i],
                  w_down[i], b_down[i], n_lat, n_tok)
    return xs[:n_lat].reshape(x.shape)
```

```python
import functools
import math

import numpy as np
import jax
import jax.numpy as jnp
from jax import lax
from jax.experimental import pallas as pl
from jax.experimental.pallas import tpu as pltpu

F32, BF16, I32 = jnp.float32, jnp.bfloat16, jnp.int32
SDS = jax.ShapeDtypeStruct
BS = pl.BlockSpec

HEAD_DIM = 128
N_KV_HEADS = 4
GROUP = 4
GRID_W = 64
WINDOW = 128
ROPE_BASE = 10000.0
TOP_K = 4
SWIGLU_LIMIT = 7.0
SWIGLU_ALPHA = 1.702
EPS = 1e-6
NEG_INF = -1e30

LANES = 128
TM = 512
TQ = 256
TK = 1024
ATT_RB = 128
ATT_UNROLL = 4
LOG2E = math.log2(math.e)
MOE_BLK = 256
TOK_DMA = 256
VMEM_LIMIT = 56 * 1024 * 1024


def _cp(*sem):
    return pltpu.CompilerParams(dimension_semantics=sem, vmem_limit_bytes=VMEM_LIMIT)


def _norm_mod(x, gain, scale, shift):
    ms = jnp.mean(x * x, axis=-1, keepdims=True)
    return (x * lax.rsqrt(ms + EPS)) * gain * (1.0 + scale) + shift


def _split3(a):
    a0 = a.astype(BF16)
    r1 = a - a0.astype(F32)
    a1 = r1.astype(BF16)
    a2 = (r1 - a1.astype(F32)).astype(BF16)
    return a0, a1, a2


def _dot_f32(a, b):
    a0, a1, a2 = _split3(a)
    b0, b1, b2 = _split3(b)
    d = lambda u, v: jnp.dot(u, v, preferred_element_type=F32)
    return (d(a0, b0) + (d(a0, b1) + d(a1, b0)) + (d(a0, b2) + d(a1, b1) + d(a2, b0)))


def _mod_kernel(c_ref, w_ref, b_ref, o_ref):
    cc = c_ref[...]
    s = cc * (1.0 / (1.0 + jnp.exp(-cc)))
    o_ref[...] = jnp.dot(s.astype(BF16), w_ref[...].astype(BF16), preferred_element_type=F32) + b_ref[...]


def _modulation(cc, w_mod, b_mod):
    depth, d, d6 = w_mod.shape
    tn = 1024
    return pl.pallas_call(
        _mod_kernel,
        out_shape=SDS((depth, 8, d6), F32),
        grid=(depth, d6 // tn),
        in_specs=[BS((8, d), lambda l, j: (0, 0)),
                  BS((None, d, tn), lambda l, j: (l, 0, j)),
                  BS((None, 1, tn), lambda l, j: (l, 0, j))],
        out_specs=BS((None, 8, tn), lambda l, j: (l, 0, j)),
        compiler_params=_cp("parallel", "parallel"),
        name="modulation",
    )(cc, w_mod, b_mod.reshape(depth, 1, d6))


def _mod_spec(n_lat_tiles, d, tn, chunk):
    per = d // tn
    return BS((None, 1, tn), lambda i, j: (jnp.where(i < n_lat_tiles, 0, 1), 0, chunk * per + j))


def _mod_spec_full(n_lat_tiles, d, chunk):
    return BS((None, 1, d), lambda i, j: (jnp.where(i < n_lat_tiles, 0, 1), 0, chunk))


def _qkv_kernel(x_ref, gain_ref, sc_ref, sh_ref, w_ref, qg_ref, kg_ref, cos_ref, sin_ref, o_ref, h_scr,
                *, n_lat_tiles):
    i = pl.program_id(0)
    j = pl.program_id(1)

    @pl.when(j == 0)
    def _():
        h_scr[...] = _norm_mod(x_ref[...], gain_ref[...], sc_ref[...], sh_ref[...]).astype(BF16)

    acc = jnp.dot(h_scr[...], w_ref[...], preferred_element_type=F32)

    @pl.when(j < 5)
    def _():
        gain = jnp.where(j < 4, qg_ref[...], kg_ref[...])
        post = jnp.where(j < 4, HEAD_DIM ** -0.5 * LOG2E, 1.0)
        is_lat = i < n_lat_tiles
        for hh in range(4):
            t = acc[:, hh * HEAD_DIM:(hh + 1) * HEAD_DIM]
            t = t * lax.rsqrt(jnp.mean(t * t, axis=-1, keepdims=True) + EPS) * gain
            rot = pltpu.roll(t, HEAD_DIM // 2, axis=1)
            tr = t * cos_ref[...] + rot * sin_ref[...]
            t = jnp.where(is_lat, tr, t) * post
            o_ref[:, hh * HEAD_DIM:(hh + 1) * HEAD_DIM] = t.astype(BF16)

    @pl.when(j == 5)
    def _():
        o_ref[...] = acc.astype(BF16)


def _qkv(x, gain, mod3, w_bf, q_gain, k_gain, cos2, sin2, n_lat):
    np_, d = x.shape
    nt, nlt = np_ // TM, n_lat // TM
    width = w_bf.shape[1]
    tn = 4 * HEAD_DIM
    return pl.pallas_call(
        functools.partial(_qkv_kernel, n_lat_tiles=nlt),
        out_shape=SDS((np_, width), BF16),
        grid=(nt, width // tn),
        in_specs=[BS((TM, d), lambda i, j: (i, 0)),
                  BS((1, d), lambda i, j: (0, 0)),
                  _mod_spec_full(nlt, d, 1),
                  _mod_spec_full(nlt, d, 0),
                  BS((d, tn), lambda i, j: (0, j)),
                  BS((1, HEAD_DIM), lambda i, j: (0, 0)),
                  BS((1, HEAD_DIM), lambda i, j: (0, 0)),
                  BS((TM, HEAD_DIM), lambda i, j: (jnp.minimum(i, nlt - 1), 0)),
                  BS((TM, HEAD_DIM), lambda i, j: (jnp.minimum(i, nlt - 1), 0))],
        out_specs=BS((TM, tn), lambda i, j: (i, j)),
        scratch_shapes=[pltpu.VMEM((TM, d), BF16)],
        compiler_params=_cp("parallel", "arbitrary"),
        name="qkv_proj",
    )(x, gain.reshape(1, d), mod3, mod3, w_bf, q_gain.reshape(1, HEAD_DIM), k_gain.reshape(1, HEAD_DIM),
      cos2, sin2)


def _rope_tables(n_lat):
    pairs = HEAD_DIM // 4
    t = jnp.arange(n_lat, dtype=I32)
    row = (t // GRID_W).astype(F32)
    col = (t % GRID_W).astype(F32)
    freqs = ROPE_BASE ** (-jnp.arange(pairs, dtype=F32) / pairs)
    ang = jnp.concatenate([row[:, None] * freqs, col[:, None] * freqs], axis=-1)
    cos, sin = jnp.cos(ang), jnp.sin(ang)
    return jnp.concatenate([cos, cos], axis=-1), jnp.concatenate([-sin, sin], axis=-1)


def _attn_kernel(sink_ref, q_ref, k_ref, v_ref, o_ref, q_scr, m_scr, acc_scr,
                 *, windowed, has_sink, n_lat, n_ctx, n_ctx_pad):
    kvh = pl.program_id(0)
    i = pl.program_id(1)
    is_lat = i < n_lat // TQ
    rows = GROUP * TQ
    for h in range(GROUP):
        q_scr[h * TQ:(h + 1) * TQ, :] = q_ref[:, h * HEAD_DIM:(h + 1) * HEAD_DIM]
    m_scr[...] = jnp.full((rows, LANES), NEG_INF, F32)
    acc_scr[...] = jnp.zeros((rows, 2 * HEAD_DIM), F32)

    def step(start, size, mask_fn):
        k = k_ref[pl.ds(start, size), :]
        v = jnp.concatenate([v_ref[pl.ds(start, size), :], jnp.ones((size, HEAD_DIM), BF16)], axis=1)
        for r in range(rows // ATT_RB):
            sl = slice(r * ATT_RB, (r + 1) * ATT_RB)
            s = lax.dot_general(q_scr[sl, :], k, (((1,), (1,)), ((), ())), preferred_element_type=F32)
            if mask_fn is not None:
                s = jnp.where(mask_fn(r), s, NEG_INF)
            m_old = m_scr[sl, :]
            m_new = jnp.maximum(m_old, jnp.max(s, axis=-1, keepdims=True))
            alpha = jnp.exp2(m_old - m_new)
            p = jnp.exp2(s - jnp.concatenate([m_new] * (size // LANES), axis=1))
            acc_scr[sl, :] = (jnp.concatenate([alpha, alpha], axis=1) * acc_scr[sl, :]
                              + jnp.dot(p.astype(BF16), v, preferred_element_type=F32))
            m_scr[sl, :] = m_new

    if windowed:
        wk = TQ + 2 * WINDOW

        @pl.when(is_lat)
        def _():
            start = pl.multiple_of(jnp.clip(i * TQ - WINDOW, 0, n_lat - wk), WINDOW)

            def band(r):
                q_pos = i * TQ + (r * ATT_RB) % TQ + lax.broadcasted_iota(I32, (ATT_RB, wk), 0)
                k_pos = start + lax.broadcasted_iota(I32, (ATT_RB, wk), 1)
                return jnp.abs(q_pos - k_pos) <= WINDOW

            step(start, wk, band)
    else:
        def body(c, carry):
            for half in range(ATT_UNROLL):
                step(pl.multiple_of((c * ATT_UNROLL + half) * TK, TK), TK, None)
            return carry

        trips = n_lat // (TK * ATT_UNROLL)
        lax.fori_loop(0, jnp.where(is_lat, trips, 0), body, 0)
        if n_lat // TK > trips * ATT_UNROLL:
            @pl.when(is_lat)
            def _():
                for c in range(trips * ATT_UNROLL, n_lat // TK):
                    step(c * TK, TK, None)

    ctx_mask = None
    if n_ctx != n_ctx_pad:
        ctx_mask = lambda r: lax.broadcasted_iota(I32, (ATT_RB, n_ctx_pad), 1) < n_ctx
    step(n_lat, n_ctx_pad, ctx_mask)

    for h in range(GROUP):
        sl = slice(h * TQ, (h + 1) * TQ)
        acc = acc_scr[sl, :]
        num, den = acc[:, :HEAD_DIM], acc[:, HEAD_DIM:]
        if has_sink:
            m = m_scr[sl, :]
            sink = sink_ref[kvh * GROUP + h] * LOG2E
            m_f = jnp.maximum(m, sink)
            scale = jnp.exp2(m - m_f)
            num = num * scale
            den = den * scale + jnp.exp2(sink - m_f)
        o_ref[:, h * HEAD_DIM:(h + 1) * HEAD_DIM] = (num * (1.0 / den)).astype(BF16)


def _attention(qkv, sinks, n_lat, n_ctx, windowed):
    np_ = qkv.shape[0]
    n_heads = N_KV_HEADS * GROUP
    has_sink = sinks is not None
    if sinks is None:
        sinks = jnp.zeros((n_heads,), F32)
    rows = GROUP * TQ
    kern = functools.partial(_attn_kernel, windowed=windowed, has_sink=has_sink, n_lat=n_lat, n_ctx=n_ctx,
                             n_ctx_pad=np_ - n_lat)
    return pl.pallas_call(
        kern,
        out_shape=SDS((np_, n_heads * HEAD_DIM), BF16),
        grid_spec=pltpu.PrefetchScalarGridSpec(
            num_scalar_prefetch=1,
            grid=(N_KV_HEADS, np_ // TQ),
            in_specs=[BS((TQ, GROUP * HEAD_DIM), lambda kv, i, s: (i, kv)),
                      BS((np_, HEAD_DIM), lambda kv, i, s: (0, n_heads + kv)),
                      BS((np_, HEAD_DIM), lambda kv, i, s: (0, n_heads + N_KV_HEADS + kv))],
            out_specs=BS((TQ, GROUP * HEAD_DIM), lambda kv, i, s: (i, kv)),
            scratch_shapes=[pltpu.VMEM((rows, HEAD_DIM), BF16), pltpu.VMEM((rows, LANES), F32),
                            pltpu.VMEM((rows, 2 * HEAD_DIM), F32)]),
        compiler_params=_cp("parallel", "arbitrary"),
        name="window_attention" if windowed else "full_attention",
    )(sinks.astype(F32), qkv, qkv, qkv)


def _proj_res_kernel(a_ref, w_ref, b_ref, x_ref, gt_ref, o_ref):
    y = jnp.dot(a_ref[...], w_ref[...], preferred_element_type=F32) + b_ref[...]
    o_ref[...] = x_ref[...] + gt_ref[...] * y


def _proj_res(a, w_bf, bias, x, mod3, n_lat, gate_chunk):
    np_, d = x.shape
    kdim = a.shape[1]
    tn = 1024
    nlt = n_lat // TM
    return pl.pallas_call(
        _proj_res_kernel,
        out_shape=SDS((np_, d), F32),
        grid=(np_ // TM, d // tn),
        in_specs=[BS((TM, kdim), lambda i, j: (i, 0)),
                  BS((kdim, tn), lambda i, j: (0, j)),
                  BS((1, tn), lambda i, j: (0, j)),
                  BS((TM, tn), lambda i, j: (i, j)),
                  _mod_spec(nlt, d, tn, gate_chunk)],
        out_specs=BS((TM, tn), lambda i, j: (i, j)),
        compiler_params=_cp("parallel", "parallel"),
        name="proj_residual",
    )(a, w_bf, bias.reshape(1, d), x, mod3)


def _router_kernel(x_ref, gain_ref, sc_ref, sh_ref, wr_ref, br_ref, h_ref, idx_ref, gate_ref):
    h = _norm_mod(x_ref[...], gain_ref[...], sc_ref[...], sh_ref[...])
    h_ref[...] = h
    vals = _dot_f32(h, wr_ref[...]) + br_ref[...]
    lane = lax.broadcasted_iota(I32, vals.shape, 1).astype(F32)
    tops, idxs = [], []
    for _ in range(TOP_K):
        m = jnp.max(vals, axis=-1, keepdims=True)
        sel = jnp.min(jnp.where(vals == m, lane, float(LANES)), axis=-1, keepdims=True)
        tops.append(m)
        idxs.append(sel)
        vals = jnp.where(lane == sel, 2.0 * NEG_INF, vals)
    es = [jnp.exp(t - tops[0]) for t in tops]
    inv = 1.0 / (es[0] + es[1] + es[2] + es[3])
    idx_out = jnp.zeros(vals.shape, F32)
    gate_out = jnp.zeros(vals.shape, F32)
    for k in range(TOP_K):
        idx_out = jnp.where(lane == float(k), idxs[k], idx_out)
        gate_out = jnp.where(lane == float(k), es[k] * inv, gate_out)
    idx_ref[...] = idx_out.astype(I32)
    gate_ref[...] = gate_out


def _router(x, gain, mod3, wr_pad, br_pad, n_lat):
    np_, d = x.shape
    nlt = n_lat // TM
    return pl.pallas_call(
        _router_kernel,
        out_shape=(SDS((np_, d), F32), SDS((np_, LANES), I32), SDS((np_, LANES), F32)),
        grid=(np_ // TM, 1),
        in_specs=[BS((TM, d), lambda i, j: (i, 0)),
                  BS((1, d), lambda i, j: (0, 0)),
                  _mod_spec_full(nlt, d, 4),
                  _mod_spec_full(nlt, d, 3),
                  BS((d, LANES), lambda i, j: (0, 0)),
                  BS((1, LANES), lambda i, j: (0, 0))],
        out_specs=(BS((TM, d), lambda i, j: (i, 0)),
                   BS((TM, LANES), lambda i, j: (i, 0)),
                   BS((TM, LANES), lambda i, j: (i, 0))),
        compiler_params=_cp("parallel", "arbitrary"),
        name="router_top4",
    )(x, gain.reshape(1, d), mod3, mod3, wr_pad, br_pad)


def _rank_kernel(idx_ref, rank_ref, cnt_ref, base_scr, *, n_tok):
    i = pl.program_id(0)

    @pl.when(i == 0)
    def _():
        base_scr[...] = jnp.zeros(base_scr.shape, F32)

    idx = idx_ref[...]
    lane = lax.broadcasted_iota(I32, idx.shape, 1)
    row = i * TM + lax.broadcasted_iota(I32, idx.shape, 0)
    valid = row < n_tok
    hots = [jnp.where(valid & (lane == idx[:, k:k + 1]), 1.0, 0.0) for k in range(TOP_K)]
    cnt = hots[0] + hots[1] + hots[2] + hots[3]
    tri = jnp.where(lax.broadcasted_iota(I32, (TM, TM), 0) > lax.broadcasted_iota(I32, (TM, TM), 1), 1.0, 0.0)
    before = base_scr[...] + jnp.dot(tri.astype(BF16), cnt.astype(BF16), preferred_element_type=F32)
    out = jnp.zeros(idx.shape, F32)
    for k in range(TOP_K):
        out = jnp.where(lane == k, jnp.sum(hots[k] * before, axis=-1, keepdims=True), out)
    rank_ref[...] = out.astype(I32)
    base_scr[...] = base_scr[...] + jnp.sum(cnt, axis=0, keepdims=True)
    cnt_ref[...] = base_scr[...].astype(I32)


def _expert_ranks(idx, n_tok):
    np_ = idx.shape[0]
    return pl.pallas_call(
        functools.partial(_rank_kernel, n_tok=n_tok),
        out_shape=(SDS((np_, LANES), I32), SDS((1, LANES), I32)),
        grid=(np_ // TM,),
        in_specs=[BS((TM, LANES), lambda i: (i, 0))],
        out_specs=(BS((TM, LANES), lambda i: (i, 0)), BS((1, LANES), lambda i: (0, 0))),
        scratch_shapes=[pltpu.VMEM((1, LANES), F32)],
        compiler_params=_cp("arbitrary"),
        name="expert_ranks",
    )(idx)


def _row_copy(src, dst, sem):
    return pltpu.make_async_copy(src, dst, sem)


def _dispatch_kernel(dest_ref, h_ref, xs_init, xs_hbm, sem):
    del xs_init

    def issue(t, carry):
        src = h_ref.at[pl.ds(t, 1)]
        for k in range(TOP_K):
            _row_copy(src, xs_hbm.at[pl.ds(dest_ref[t * TOP_K + k], 1)], sem).start(priority=k % 2)
        return carry

    lax.fori_loop(0, TOK_DMA, issue, 0)

    def drain(t, carry):
        for k in range(TOP_K):
            _row_copy(h_ref.at[pl.ds(0, 1)], xs_hbm.at[pl.ds(0, 1)], sem).wait()
        return carry

    lax.fori_loop(0, TOK_DMA, drain, 0)


def _dispatch(dest_flat, h, n_tok, n_rows):
    d = h.shape[1]
    xs0 = jnp.zeros((n_rows, d), F32)
    return pl.pallas_call(
        _dispatch_kernel,
        out_shape=SDS((n_rows, d), F32),
        grid=(n_tok // TOK_DMA,),
        in_specs=[BS((TOK_DMA * TOP_K,), lambda i: (i,), memory_space=pltpu.SMEM),
                  BS((TOK_DMA, d), lambda i: (i, 0)),
                  BS(memory_space=pl.ANY)],
        out_specs=BS(memory_space=pl.ANY),
        scratch_shapes=[pltpu.SemaphoreType.DMA(())],
        input_output_aliases={2: 0},
        compiler_params=pltpu.CompilerParams(dimension_semantics=("arbitrary",), has_side_effects=True),
        name="moe_dispatch",
    )(dest_flat, h, xs0)


def _ffn_kernel(be_ref, nu_ref, x_ref, wgu_ref, bgu_ref, wdn_ref, bdn_ref, o_ref, *, ff):
    b = pl.program_id(0)

    @pl.when(b < nu_ref[0])
    def _():
        gu = jnp.dot(x_ref[...].astype(BF16), wgu_ref[...].astype(BF16), preferred_element_type=F32) + bgu_ref[...]
        glu = jnp.minimum(gu[:, :ff], SWIGLU_LIMIT)
        lin = jnp.clip(gu[:, ff:], -SWIGLU_LIMIT, SWIGLU_LIMIT)
        act = glu * (1.0 / (1.0 + jnp.exp(-SWIGLU_ALPHA * glu))) * (lin + 1.0)
        o_ref[...] = jnp.dot(act.astype(BF16), wdn_ref[...].astype(BF16), preferred_element_type=F32) + bdn_ref[...]

    @pl.when(b >= nu_ref[0])
    def _():
        o_ref[...] = jnp.zeros(o_ref.shape, F32)


def _expert_ffn(blk_expert, n_used, xs, layer, w_gu, bgu, w_dn, bdn):
    n_rows, d = xs.shape
    _, n_exp, _, ff2 = w_gu.shape
    ff = ff2 // 2
    return pl.pallas_call(
        functools.partial(_ffn_kernel, ff=ff),
        out_shape=SDS((n_rows, d), F32),
        grid_spec=pltpu.PrefetchScalarGridSpec(
            num_scalar_prefetch=2,
            grid=(n_rows // MOE_BLK,),
            in_specs=[BS((MOE_BLK, d), lambda b, be, nu: (b, 0)),
                      BS((None, None, d, ff2), lambda b, be, nu: (layer, be[b], 0, 0)),
                      BS((None, 1, ff2), lambda b, be, nu: (be[b], 0, 0)),
                      BS((None, None, ff, d), lambda b, be, nu: (layer, be[b], 0, 0)),
                      BS((None, 1, d), lambda b, be, nu: (be[b], 0, 0))],
            out_specs=BS((MOE_BLK, d), lambda b, be, nu: (b, 0))),
        compiler_params=_cp("arbitrary"),
        name="expert_ffn",
    )(blk_expert, n_used, xs, w_gu, bgu.reshape(n_exp, 1, ff2), w_dn, bdn.reshape(n_exp, 1, d))


def _combine_kernel(dest_ref, g_ref, x_ref, gt_ref, y_hbm, o_ref, buf, sem):
    def issue(t, carry):
        for k in range(TOP_K):
            _row_copy(y_hbm.at[pl.ds(dest_ref[t * TOP_K + k], 1)], buf.at[k, pl.ds(t, 1)], sem).start(
                priority=k % 2)
        return carry

    lax.fori_loop(0, TOK_DMA, issue, 0)

    def drain(t, carry):
        for k in range(TOP_K):
            _row_copy(y_hbm.at[pl.ds(0, 1)], buf.at[k, pl.ds(0, 1)], sem).wait()
        return carry

    lax.fori_loop(0, TOK_DMA, drain, 0)
    g = g_ref[...]
    acc = g[:, 0:1] * buf[0]
    for k in range(1, TOP_K):
        acc = acc + g[:, k:k + 1] * buf[k]
    o_ref[...] = x_ref[...] + gt_ref[...] * acc


def _combine(dest_flat, gates, x, mod3, y, n_lat):
    np_, d = x.shape
    nlt = n_lat // TOK_DMA
    return pl.pallas_call(
        _combine_kernel,
        out_shape=SDS((np_, d), F32),
        grid=(np_ // TOK_DMA,),
        in_specs=[BS((TOK_DMA * TOP_K,), lambda i: (i,), memory_space=pltpu.SMEM),
                  BS((TOK_DMA, LANES), lambda i: (i, 0)),
                  BS((TOK_DMA, d), lambda i: (i, 0)),
                  BS((None, 1, d), lambda i: (jnp.where(i < nlt, 0, 1), 0, 5)),
                  BS(memory_space=pl.ANY)],
        out_specs=BS((TOK_DMA, d), lambda i: (i, 0)),
        scratch_shapes=[pltpu.VMEM((TOP_K, TOK_DMA, d), F32), pltpu.SemaphoreType.DMA(())],
        compiler_params=_cp("arbitrary"),
        name="moe_combine",
    )(dest_flat, gates, x, mod3, y)


def _moe(x, gain, mod3, router_w, router_b, layer, w_gu, bgu, w_dn, bdn, n_lat, n_tok):
    np_, d = x.shape
    n_exp = router_w.shape[1]
    wr_pad = jnp.zeros((d, LANES), F32).at[:, :n_exp].set(router_w)
    br_pad = jnp.full((1, LANES), NEG_INF, F32).at[0, :n_exp].set(router_b)
    h, idx, gates = _router(x, gain, mod3, wr_pad, br_pad, n_lat)
    rank, counts = _expert_ranks(idx, n_tok)
    counts = counts[0, :n_exp]
    padded = (counts + MOE_BLK - 1) // MOE_BLK * MOE_BLK
    pad_end = jnp.cumsum(padded)
    pad_start = pad_end - padded
    dest = pad_start[idx[:, :TOP_K]] + rank[:, :TOP_K]
    dest = jnp.where(jnp.arange(np_)[:, None] < n_tok, dest, 0).astype(I32).reshape(-1)
    n_blocks = -(-(n_tok * TOP_K + n_exp * (MOE_BLK - 1)) // MOE_BLK)
    blk_row = jnp.arange(n_blocks, dtype=I32) * MOE_BLK
    blk_expert = jnp.minimum(jnp.sum(pad_end[None, :] <= blk_row[:, None], axis=1), n_exp - 1).astype(I32)
    n_used = (pad_end[-1:] // MOE_BLK).astype(I32)
    xs = _dispatch(dest, h, n_tok, n_blocks * MOE_BLK)
    y = _expert_ffn(blk_expert, n_used, xs, layer, w_gu, bgu, w_dn, bdn)
    return _combine(dest, gates, x, mod3, y, n_lat)


def _attn_layer(x, mod3, norm_gain, w_qkv, w_o, q_gain, k_gain, sinks, cos2, sin2, n_lat, n_ctx, windowed):
    d = x.shape[1]
    qkv = _qkv(x, norm_gain, mod3, w_qkv.astype(BF16), q_gain, k_gain, cos2, sin2, n_lat)
    o = _attention(qkv, sinks, n_lat, n_ctx, windowed)
    return _proj_res(o, w_o.astype(BF16), jnp.zeros((d,), F32), x, mod3, n_lat, 2)


def _linear_norm_kernel(x_ref, gain_ref, sc_ref, sh_ref, w_ref, b_ref, o_ref, h_scr):
    @pl.when(pl.program_id(1) == 0)
    def _():
        h_scr[...] = _norm_mod(x_ref[...], gain_ref[...], sc_ref[...], sh_ref[...]).astype(BF16)

    o_ref[...] = (jnp.dot(h_scr[...], w_ref[...], preferred_element_type=F32) + b_ref[...]).astype(o_ref.dtype)


def _linear_norm(x, gain, mod3, w_bf, bias, n_lat, out_dtype):
    np_, d = x.shape
    width = w_bf.shape[1]
    tn = 1024
    nlt = n_lat // TM
    return pl.pallas_call(
        _linear_norm_kernel,
        out_shape=SDS((np_, width), out_dtype),
        grid=(np_ // TM, width // tn),
        in_specs=[BS((TM, d), lambda i, j: (i, 0)),
                  BS((1, d), lambda i, j: (0, 0)),
                  _mod_spec_full(nlt, d, 1),
                  _mod_spec_full(nlt, d, 0),
                  BS((d, tn), lambda i, j: (0, j)),
                  BS((1, tn), lambda i, j: (0, j))],
        out_specs=BS((TM, tn), lambda i, j: (i, j)),
        scratch_shapes=[pltpu.VMEM((TM, d), BF16)],
        compiler_params=_cp("parallel", "arbitrary"),
        name="linear_norm",
    )(x, gain.reshape(1, d), mod3, mod3, w_bf, bias.reshape(1, width))


HY_NB = 128
HY_SUB = 8
HY_BANDS = 8
HY_MOD_SHIFT = 0.05
HALO = 16


def _short_conv_kernel(z0c, z0p, z0n, z1c, z1p, z1n, z2c, z2p, z2n, w_ref, b_ref, x0_ref, u_ref,
                       *, n_lat_tiles, n_ctx):
    i = pl.program_id(0)
    row = lax.broadcasted_iota(I32, (TM, 1), 0)
    is_ctx = i >= n_lat_tiles
    has_prev = jnp.logical_and(i > 0, i != n_lat_tiles)
    has_next = i < n_lat_tiles - 1
    keep = jnp.logical_or(jnp.logical_not(is_ctx), row < n_ctx)

    def conv(cur, prv, nxt, part):
        zc = jnp.where(keep, cur[...].astype(F32), 0.0)
        first = jnp.where(has_prev, prv[HALO - 1:HALO, :].astype(F32), 0.0)
        last = jnp.where(has_next, nxt[0:1, :].astype(F32), 0.0)
        up = jnp.where(row == 0, first, pltpu.roll(zc, 1, axis=0))
        dn = jnp.where(row == TM - 1, last, pltpu.roll(zc, TM - 1, axis=0))
        w = w_ref[part]
        return up * w[0:1, :] + zc * w[1:2, :] + dn * w[2:3, :] + b_ref[part]

    x0 = conv(z0c, z0p, z0n, 0)
    x1 = conv(z1c, z1p, z1n, 1)
    v = conv(z2c, z2p, z2n, 2)
    x0_ref[...] = x0.astype(BF16)
    u_ref[...] = x1 * v


def _short_conv(z, w_short, b_short, n_lat, n_ctx):
    np_, d3 = z.shape
    d = d3 // 3
    tn = 512
    per = d // tn
    nlt = n_lat // TM
    hb = TM // HALO
    last_halo = np_ // HALO - 1
    specs = []
    for part in range(3):
        specs.append(BS((TM, tn), lambda i, j, part=part: (i, part * per + j)))
        specs.append(BS((HALO, tn), lambda i, j, part=part: (jnp.maximum(i * hb - 1, 0), part * per + j)))
        specs.append(BS((HALO, tn), lambda i, j, part=part: (jnp.minimum((i + 1) * hb, last_halo), part * per + j)))
    w3 = w_short.reshape(3, 3, d).transpose(1, 0, 2)
    b3 = b_short.reshape(3, 1, d)
    return pl.pallas_call(
        functools.partial(_short_conv_kernel, n_lat_tiles=nlt, n_ctx=n_ctx),
        out_shape=(SDS((np_, d), BF16), SDS((np_, d), F32)),
        grid=(np_ // TM, per),
        in_specs=specs + [BS((3, 3, tn), lambda i, j: (0, 0, j)), BS((3, 1, tn), lambda i, j: (0, 0, j))],
        out_specs=(BS((TM, tn), lambda i, j: (i, j)), BS((TM, tn), lambda i, j: (i, j))),
        compiler_params=_cp("parallel", "parallel"),
        name="hyena_short_conv",
    )(z, z, z, z, z, z, z, z, z, w3, b3)


def _filter_kernel(w1_ref, b1_ref, w2_ref, b2_ref, sf_ref, w3f_ref, b3f_ref, w3b_ref, b3b_ref, dec_ref, o_ref,
                   hid_scr, *, n, rows, nsub, s_mul, s_sub, s_blk):
    blk = pl.program_id(0)
    a = lax.broadcasted_iota(I32, (rows, 1), 0)
    lane = lax.broadcasted_iota(I32, (rows, LANES), 1)

    def positions(sub):
        s = s_mul * a + (s_sub * sub + s_blk * blk)
        lag = jnp.where(s < n, s, 2 * n - s)
        return s, lag.astype(F32) / float(n)

    @pl.when(pl.program_id(1) == 0)
    def _():
        band = jnp.where(lane <= HY_BANDS, lane, HY_BANDS - lane).astype(F32)
        phase = jnp.where(lane <= HY_BANDS, 0.5 * jnp.pi, 0.0)
        for sub in range(nsub):
            _, t = positions(sub)
            trig = jnp.sin(2.0 * jnp.pi * t * band + phase)
            feats = jnp.where(lane == 0, t, jnp.where(lane <= 2 * HY_BANDS, trig, 0.0))
            hid = jnp.sin(sf_ref[0:1, :] * (_dot_f32(feats, w1_ref[...]) + b1_ref[...]))
            hid_scr[sub] = jnp.sin(sf_ref[1:2, :] * (_dot_f32(hid, w2_ref[...]) + b2_ref[...]))

    w3f, w3b = w3f_ref[...].astype(BF16), w3b_ref[...].astype(BF16)
    for sub in range(nsub):
        s, t = positions(sub)
        hid = hid_scr[sub].astype(BF16)
        fwd = jnp.dot(hid, w3f, preferred_element_type=F32) + b3f_ref[...]
        bwd = jnp.dot(hid, w3b, preferred_element_type=F32) + b3b_ref[...]
        window = jnp.exp(-t * jnp.abs(dec_ref[...])) + HY_MOD_SHIFT
        g = jnp.where(s < n, fwd, jnp.where(s == n, 0.0, bwd)) * window
        if nsub == 1:
            o_ref[...] = g
        else:
            o_ref[:, sub, :] = g


def _hyena_filter(n, fw1, fb1, fw2, fb2, fw3, fb3, sin_freq, decay, two_d):
    width = fw1.shape[1]
    d = decay.shape[0]
    tn = 512
    per = d // tn
    pad = lambda m, r, c: jnp.zeros((r, c), F32).at[:m.shape[0], :m.shape[1]].set(m)
    w1p = pad(fw1, LANES, LANES)
    b1p = pad(fb1.reshape(1, width), 1, LANES)
    w2p = pad(fw2, LANES, LANES)
    b2p = pad(fb2.reshape(1, width), 1, LANES)
    sfp = pad(sin_freq, 2, LANES)
    w3p = pad(fw3, LANES, 2 * d)
    b3 = fb3.reshape(1, 2 * d)
    if two_d:
        rows, nsub, nblk = 2 * n // HY_NB, HY_SUB, HY_NB // HY_SUB
        s_mul, s_sub, s_blk = HY_NB, 1, HY_SUB
        out_shape = SDS((rows, HY_NB, d), F32)
        out_spec = BS((rows, HY_SUB, tn), lambda b, j: (0, b, j))
    else:
        rows, nsub = min(2 * n, 256), 1
        nblk, s_mul, s_sub, s_blk = 2 * n // rows, 1, 0, rows
        out_shape = SDS((2 * n, d), F32)
        out_spec = BS((rows, tn), lambda b, j: (b, j))
    const = lambda r, c: BS((r, c), lambda b, j: (0, 0))
    return pl.pallas_call(
        functools.partial(_filter_kernel, n=n, rows=rows, nsub=nsub, s_mul=s_mul, s_sub=s_sub, s_blk=s_blk),
        out_shape=out_shape,
        grid=(nblk, per),
        in_specs=[const(LANES, LANES), const(1, LANES), const(LANES, LANES), const(1, LANES), const(2, LANES),
                  BS((LANES, tn), lambda b, j: (0, j)), BS((1, tn), lambda b, j: (0, j)),
                  BS((LANES, tn), lambda b, j: (0, per + j)), BS((1, tn), lambda b, j: (0, per + j)),
                  BS((1, tn), lambda b, j: (0, j))],
        out_specs=out_spec,
        scratch_shapes=[pltpu.VMEM((nsub, rows, LANES), F32)],
        compiler_params=_cp("parallel", "arbitrary"),
        name="hyena_filter",
    )(w1p, b1p, w2p, b2p, sfp, w3p, b3, w3p, b3, decay.reshape(1, d))


def _dft_tables(n_total):
    nb = HY_NB
    na = n_total // nb
    ka = np.arange(na)[:, None]
    a = np.arange(na)[None, :]
    ang_a = 2.0 * np.pi * ((ka * a) % na) / na
    fa = np.concatenate([np.cos(ang_a), -np.sin(ang_a)], axis=0)
    kb = np.arange(nb)[:, None]
    b = np.arange(nb)[None, :]
    ang_b = 2.0 * np.pi * ((kb * b) % nb) / nb
    c, s = np.cos(ang_b), np.sin(ang_b)
    fwd_b = np.block([[c, s], [-s, c]])
    inv_b = np.block([[c, -s], [s, c]]) / float(n_total)
    ang_t = 2.0 * np.pi * ((np.arange(nb)[:, None] * np.arange(na)[None, :]) % n_total) / n_total
    tw = (np.cos(ang_t), -np.sin(ang_t))
    inv_a = np.concatenate([np.cos(ang_a[:na // 2]), -np.sin(ang_a[:na // 2])], axis=1)
    f = lambda m: jnp.asarray(m, F32)
    return dict(na=na, fa=f(fa), fwd_b=f(fwd_b), inv_b=f(inv_b), tw_re=f(tw[0]), tw_im=f(tw[1]), inv_a=f(inv_a))


def _dft_a_kernel(f_ref, x_ref, twr_ref, twi_ref, re_ref, im_ref, *, na):
    f = f_ref[...]
    for sub in range(HY_SUB):
        pq = jnp.dot(f, x_ref[:, sub, :].astype(BF16), preferred_element_type=F32)
        p, q = pq[:na], pq[na:]
        twr, twi = twr_ref[sub], twi_ref[sub]
        re_ref[:, sub, :] = p * twr - q * twi
        im_ref[:, sub, :] = p * twi + q * twr


def _dft_a(x3, f_bf, tw_re, tw_im):
    ka_rows, nb, d = x3.shape
    na = f_bf.shape[0] // 2
    tn = min(d, 512)
    twr = tw_re.reshape(nb, na, 1)
    twi = tw_im.reshape(nb, na, 1)
    tws = BS((HY_SUB, na, 1), lambda b, j: (b, 0, 0))
    out = BS((na, HY_SUB, tn), lambda b, j: (0, b, j))
    return pl.pallas_call(
        functools.partial(_dft_a_kernel, na=na),
        out_shape=(SDS((na, nb, d), F32), SDS((na, nb, d), F32)),
        grid=(nb // HY_SUB, d // tn),
        in_specs=[BS((2 * na, ka_rows), lambda b, j: (0, 0)),
                  BS((ka_rows, HY_SUB, tn), lambda b, j: (0, b, j)),
                  tws, tws],
        out_specs=(out, out),
        compiler_params=_cp("parallel", "parallel"),
        name="dft_stage_a",
    )(f_bf, x3, twr, twi)


def _dft_b_kernel(fwd_ref, inv_ref, ur_ref, ui_ref, gr_ref, gi_ref, twr_ref, twi_ref, br_ref, bi_ref):
    nb = HY_NB
    fwd = fwd_ref[...]
    stack = lambda re, im: jnp.concatenate([re[...], im[...]], axis=0).astype(BF16)
    us = jnp.dot(fwd, stack(ur_ref, ui_ref), preferred_element_type=F32)
    gs = jnp.dot(fwd, stack(gr_ref, gi_ref), preferred_element_type=F32)
    ure, uim, gre, gim = us[:nb], us[nb:], gs[:nb], gs[nb:]
    yre = ure * gre - uim * gim
    yim = ure * gim + uim * gre
    ys = jnp.concatenate([yre, yim], axis=0).astype(BF16)
    bs = jnp.dot(inv_ref[...], ys, preferred_element_type=F32)
    bre, bim = bs[:nb], bs[nb:]
    twr, twi = twr_ref[...], twi_ref[...]
    br_ref[...] = bre * twr + bim * twi
    bi_ref[...] = bim * twr - bre * twi


def _dft_b(ur, ui, gr, gi, fwd_bf, inv_bf, tw_re, tw_im):
    na, nb, d = ur.shape
    tn = min(d, 2048)
    twr = tw_re.T.reshape(na, nb, 1)
    twi = tw_im.T.reshape(na, nb, 1)
    data = BS((None, nb, tn), lambda k, j: (k, 0, j))
    tws = BS((None, nb, 1), lambda k, j: (k, 0, 0))
    return pl.pallas_call(
        _dft_b_kernel,
        out_shape=(SDS((na, nb, d), F32), SDS((na, nb, d), F32)),
        grid=(na, d // tn),
        in_specs=[BS((2 * nb, 2 * nb), lambda k, j: (0, 0)), BS((2 * nb, 2 * nb), lambda k, j: (0, 0)),
                  data, data, data, data, tws, tws],
        out_specs=(data, data),
        compiler_params=_cp("parallel", "parallel"),
        name="dft_stage_b",
    )(fwd_bf, inv_bf, ur, ui, gr, gi, twr, twi)


def _idft_a_kernel(f_ref, br_ref, bi_ref, o_ref):
    f = f_ref[...]
    for sub in range(HY_SUB):
        bs = jnp.concatenate([br_ref[:, sub, :], bi_ref[:, sub, :]], axis=0).astype(BF16)
        o_ref[:, sub, :] = jnp.dot(f, bs, preferred_element_type=F32)


def _idft_a(br, bi, inv_a_bf):
    na, nb, d = br.shape
    tn = min(d, 512)
    data = BS((na, HY_SUB, tn), lambda b, j: (0, b, j))
    return pl.pallas_call(
        _idft_a_kernel,
        out_shape=SDS((na // 2, nb, d), F32),
        grid=(nb // HY_SUB, d // tn),
        in_specs=[BS((na // 2, 2 * na), lambda b, j: (0, 0)), data, data],
        out_specs=BS((na // 2, HY_SUB, tn), lambda b, j: (0, b, j)),
        compiler_params=_cp("parallel", "parallel"),
        name="idft_stage_a",
    )(inv_a_bf, br, bi)


def _ctx_conv_kernel(g_ref, u_ref, o_ref, gg_scr, *, n):
    g = g_ref[...]
    gg_scr[0:2 * n, :] = g
    gg_scr[2 * n:4 * n, :] = g

    def body(s, acc):
        return acc + gg_scr[pl.ds(2 * n - s, n), :] * u_ref[pl.ds(s, 1), :]

    o_ref[...] = lax.fori_loop(0, n, body, jnp.zeros((n, LANES), F32))


def _ctx_conv(g, u):
    n2, d = g.shape
    n = n2 // 2
    return pl.pallas_call(
        functools.partial(_ctx_conv_kernel, n=n),
        out_shape=SDS((n, d), F32),
        grid=(d // LANES,),
        in_specs=[BS((n2, LANES), lambda j: (0, j)), BS((n, LANES), lambda j: (0, j))],
        out_specs=BS((n, LANES), lambda j: (0, j)),
        scratch_shapes=[pltpu.VMEM((2 * n2, LANES), F32)],
        compiler_params=_cp("parallel"),
        name="hyena_ctx_conv",
    )(g, u)


def _gated_proj_res_kernel(a_ref, y_ref, u_ref, skip_ref, w_ref, b_ref, x_ref, gt_ref, o_ref):
    a = (a_ref[...].astype(F32) * (y_ref[...] + u_ref[...] * skip_ref[...])).astype(BF16)
    y = jnp.dot(a, w_ref[...], preferred_element_type=F32) + b_ref[...]
    o_ref[...] = x_ref[...] + gt_ref[...] * y


def _gated_proj_res(a, y, u, skip, w_bf, bias, x, mod3, n_lat, gate_chunk):
    np_, d = x.shape
    kdim = a.shape[1]
    tn = 1024
    nlt = n_lat // TM
    rows = BS((TM, kdim), lambda i, j: (i, 0))
    return pl.pallas_call(
        _gated_proj_res_kernel,
        out_shape=SDS((np_, d), F32),
        grid=(np_ // TM, d // tn),
        in_specs=[rows, rows, rows,
                  BS((1, kdim), lambda i, j: (0, 0)),
                  BS((kdim, tn), lambda i, j: (0, j)),
                  BS((1, tn), lambda i, j: (0, j)),
                  BS((TM, tn), lambda i, j: (i, j)),
                  _mod_spec(nlt, d, tn, gate_chunk)],
        out_specs=BS((TM, tn), lambda i, j: (i, j)),
        compiler_params=_cp("parallel", "parallel"),
        name="gated_proj_residual",
    )(a, y, u, skip.reshape(1, kdim), w_bf, bias.reshape(1, d), x, mod3)


def _hyena_layer(x, mod3, norm_gain, w_in, b_in, w_short, b_short, fw1, fb1, fw2, fb2, fw3, fb3, sin_freq, decay,
                 skip, w_out, b_out, n_lat, n_ctx):
    np_, d = x.shape
    z = _linear_norm(x, norm_gain, mod3, w_in.astype(BF16), b_in, n_lat, BF16)
    x0, u = _short_conv(z, w_short, b_short, n_lat, n_ctx)
    tabs = _dft_tables(2 * n_lat)
    na = tabs["na"]
    g3 = _hyena_filter(n_lat, fw1, fb1, fw2, fb2, fw3, fb3, sin_freq, decay, True)
    u3 = u[:n_lat].reshape(na // 2, HY_NB, d)
    fa_bf = tabs["fa"].astype(BF16)
    ur, ui = _dft_a(u3, fa_bf[:, :na // 2], tabs["tw_re"], tabs["tw_im"])
    gr, gi = _dft_a(g3, fa_bf, tabs["tw_re"], tabs["tw_im"])
    br, bi = _dft_b(ur, ui, gr, gi, tabs["fwd_b"].astype(BF16), tabs["inv_b"].astype(BF16),
                    tabs["tw_re"], tabs["tw_im"])
    y_lat = _idft_a(br, bi, tabs["inv_a"].astype(BF16)).reshape(n_lat, d)
    g_ctx = _hyena_filter(n_ctx, fw1, fb1, fw2, fb2, fw3, fb3, sin_freq, decay, False)
    y_ctx = _ctx_conv(g_ctx, u[n_lat:n_lat + n_ctx])
    y = jnp.concatenate([y_lat, y_ctx, jnp.zeros((np_ - n_lat - n_ctx, d), F32)], axis=0)
    return _gated_proj_res(x0, y, u, skip, w_out.astype(BF16), b_out, x, mod3, n_lat, 2)


S5_T = 16
S5_GS = 16
S5_P = 64
S5_GB = 8


def _s5_discretize(lr_row, li_row, dt):
    mag = jnp.exp(lr_row * dt)
    a_re, a_im = mag * jnp.cos(li_row * dt), mag * jnp.sin(li_row * dt)
    xr, xi = a_re - 1.0, a_im
    den = 1.0 / (lr_row * lr_row + li_row * li_row)
    return (xr * lr_row + xi * li_row) * den, (xi * lr_row - xr * li_row) * den


def _s5_power_c(lr_col, li_col, dt, cre_t, cim_t, tau):
    width = S5_T * S5_GS
    lane = lax.broadcasted_iota(I32, (S5_GS, width), 1)
    expand = jnp.where(lane % S5_GS == lax.broadcasted_iota(I32, (S5_GS, width), 0), 1.0, 0.0)
    c_re = _dot_f32(cre_t, expand)
    c_im = _dot_f32(cim_t, expand)
    mag = jnp.exp(tau * (lr_col * dt))
    p_re, p_im = mag * jnp.cos(tau * (li_col * dt)), mag * jnp.sin(tau * (li_col * dt))
    z_re = p_re * c_re - p_im * c_im
    z_im = p_re * c_im + p_im * c_re
    return jnp.concatenate([z_re, -z_im], axis=0)


def _fill_group_perm(perm_scr, transposed):
    n = S5_GB * S5_T * S5_GS
    rows_per = 256
    for b in range(n // rows_per):
        r = b * rows_per + lax.broadcasted_iota(I32, (rows_per, n), 0)
        q = lax.broadcasted_iota(I32, (rows_per, n), 1)
        grouped, stream = (q, r) if transposed else (r, q)
        j = grouped // (S5_T * S5_GS)
        s = (grouped // S5_GS) % S5_T
        c = grouped % S5_GS
        hit = stream == s * (S5_GB * S5_GS) + j * S5_GS + c
        perm_scr[b * rows_per:(b + 1) * rows_per, :] = jnp.where(hit, 1.0, 0.0).astype(BF16)


def _s5_chunk_kernel(h_ref, lr_row, li_row, lr_col, li_col, ls_ref, bre_t, bim_t, cre_t, cim_t,
                     y_ref, s_ref, zc_ref, mt_scr, u_scr, perm_scr):
    dd = pl.program_id(1)
    width = S5_T * S5_GS

    @pl.when(jnp.logical_and(pl.program_id(0) == 0, dd == 0))
    def _():
        _fill_group_perm(perm_scr, True)

    @pl.when(dd == 0)
    def _():
        hcat = jnp.concatenate([h_ref[:, s, :].astype(BF16) for s in range(S5_T)], axis=1)
        u_scr[...] = jnp.dot(hcat, perm_scr[...], preferred_element_type=F32).astype(BF16)

    lane = lax.broadcasted_iota(I32, (S5_GS, width), 1)
    t_lane = lax.broadcasted_iota(I32, (1, width), 1) // S5_GS
    t_row = lax.broadcasted_iota(I32, (width, 1), 0) // S5_GS
    row = lax.broadcasted_iota(I32, (width, S5_GS), 0)
    expand_t = jnp.where(row % S5_GS == lax.broadcasted_iota(I32, (width, S5_GS), 1), 1.0, 0.0)
    tau_k = jnp.where(dd == 0, t_lane, S5_T - 1 - t_lane).astype(F32)
    tau_g = jnp.where(dd == 0, S5_T - 1 - t_row, t_row).astype(F32)
    for j in range(S5_GB):
        dt = jnp.exp(ls_ref[j])
        q_re, q_im = _s5_discretize(lr_row[j], li_row[j], dt)
        bb_re = q_re * bre_t[j] - q_im * bim_t[j]
        bb_im = q_re * bim_t[j] + q_im * bre_t[j]
        z = _s5_power_c(lr_col[j], li_col[j], dt, cre_t[j], cim_t[j], tau_k)
        k0 = _dot_f32(jnp.concatenate([bb_re, bb_im], axis=1), z)
        a_mag = jnp.exp(lr_col[j] * dt)
        a_re, a_im = a_mag * jnp.cos(li_col[j] * dt), a_mag * jnp.sin(li_col[j] * dt)
        z_re, z_nim = z[:S5_P], z[S5_P:]
        zc_ref[j] = jnp.concatenate([a_re * z_re + a_im * z_nim, a_re * z_nim - a_im * z_re], axis=0).astype(BF16)

        @pl.when(dd == 0)
        def _():
            for s in range(S5_T):
                blk = k0 if s == 0 else jnp.where(lane >= s * S5_GS, pltpu.roll(k0, s * S5_GS, axis=1), 0.0)
                mt_scr[s * S5_GS:(s + 1) * S5_GS, :] = blk.astype(BF16)

        @pl.when(dd == 1)
        def _():
            for s in range(S5_T):
                sh = (S5_T - 1 - s) * S5_GS
                blk = k0 if sh == 0 else jnp.where(lane < (s + 1) * S5_GS, pltpu.roll(k0, width - sh, axis=1), 0.0)
                mt_scr[s * S5_GS:(s + 1) * S5_GS, :] = blk.astype(BF16)

        u = u_scr[:, j * width:(j + 1) * width]
        y_ref[j] = jnp.dot(u, mt_scr[...], preferred_element_type=F32).astype(BF16)
        be_re = _dot_f32(expand_t, bb_re)
        be_im = _dot_f32(expand_t, bb_im)
        mag = jnp.exp(tau_g * (lr_row[j] * dt))
        p_re, p_im = mag * jnp.cos(tau_g * (li_row[j] * dt)), mag * jnp.sin(tau_g * (li_row[j] * dt))
        g = jnp.concatenate([p_re * be_re - p_im * be_im, p_re * be_im + p_im * be_re], axis=1)
        s_ref[:, j, :] = jnp.dot(u, g.astype(BF16), preferred_element_type=F32)


def _s5_chunks(h3, prm):
    nc, _, d = h3.shape
    groups, width = d // S5_GS, S5_T * S5_GS
    gl = S5_GB * S5_GS
    par = lambda r, c: BS((None, S5_GB, r, c), lambda i, dd: (dd, i, 0, 0))
    row, col, one = par(1, S5_P), par(S5_P, 1), par(1, 1)
    bt, ct = par(S5_GS, S5_P), par(S5_P, S5_GS)
    return pl.pallas_call(
        _s5_chunk_kernel,
        out_shape=(SDS((2, groups, nc, width), BF16), SDS((2, nc, groups, 2 * S5_P), F32),
                   SDS((2, groups, 2 * S5_P, width), BF16)),
        grid=(groups // S5_GB, 2),
        in_specs=[BS((nc, S5_T, gl), lambda i, dd: (0, 0, i)), row, row, col, col, one, bt, bt, ct, ct],
        out_specs=(BS((None, S5_GB, nc, width), lambda i, dd: (dd, i, 0, 0)),
                   BS((None, nc, S5_GB, 2 * S5_P), lambda i, dd: (dd, 0, i, 0)),
                   BS((None, S5_GB, 2 * S5_P, width), lambda i, dd: (dd, i, 0, 0))),
        scratch_shapes=[pltpu.VMEM((width, width), BF16), pltpu.VMEM((nc, S5_GB * width), BF16),
                        pltpu.VMEM((S5_GB * width, S5_GB * width), BF16)],
        compiler_params=_cp("arbitrary", "arbitrary"),
        name="s5_chunks",
    )(h3, prm["lr_row"], prm["li_row"], prm["lr_col"], prm["li_col"], prm["ls"], prm["bre_t"], prm["bim_t"],
      prm["cre_t"], prm["cim_t"])


def _s5_state_kernel(s0_ref, s1_ref, lr_ref, li_ref, ls_ref, h0_ref, h1_ref, st_scr, *, steps):
    @pl.when(pl.program_id(0) == 0)
    def _():
        st_scr[...] = jnp.zeros(st_scr.shape, F32)

    def coeffs(dd):
        dt = jnp.exp(ls_ref[dd])
        mag = jnp.exp(float(S5_T) * lr_ref[dd] * dt)
        ang = float(S5_T) * li_ref[dd] * dt
        a_re, a_im = mag * jnp.cos(ang), mag * jnp.sin(ang)
        lane = lax.broadcasted_iota(I32, a_im.shape, 1)
        a_sw = jnp.where(lane < S5_P, -a_im, a_im)
        return a_re, a_sw

    def run(dd, s_ref, h_ref, order):
        a_re, a_sw = coeffs(dd)
        h, hs = st_scr[2 * dd], st_scr[2 * dd + 1]
        for c in order:
            h_ref[c] = h
            s = s_ref[c]
            s_swapped = pltpu.roll(s, S5_P, axis=1)
            h, hs = a_re * h + a_sw * hs + s, a_re * hs - a_sw * h + s_swapped
        st_scr[2 * dd] = h
        st_scr[2 * dd + 1] = hs

    run(0, s0_ref, h0_ref, range(steps))
    run(1, s1_ref, h1_ref, reversed(range(steps)))


def _s5_states(s_t, prm, first_chunk):
    _, nc, groups, w = s_t.shape
    steps = 8
    nblk = nc // steps
    off = first_chunk // steps
    fwd = BS((None, steps, groups, w), lambda c: (0, (c + off) % nblk, 0, 0))
    bwd = BS((None, steps, groups, w), lambda c: (1, nblk - 1 - c, 0, 0))
    fwd_o = BS((steps, groups, w), lambda c: ((c + off) % nblk, 0, 0))
    bwd_o = BS((steps, groups, w), lambda c: (nblk - 1 - c, 0, 0))
    par = lambda last: BS((2, groups, last), lambda c: (0, 0, 0))
    return pl.pallas_call(
        functools.partial(_s5_state_kernel, steps=steps),
        out_shape=(SDS((nc, groups, w), F32), SDS((nc, groups, w), F32)),
        grid=(nblk,),
        in_specs=[fwd, bwd, par(w), par(w), par(1)],
        out_specs=(fwd_o, bwd_o),
        scratch_shapes=[pltpu.VMEM((4, groups, w), F32)],
        compiler_params=_cp("arbitrary"),
        name="s5_state_scan",
    )(s_t, s_t, prm["lr2"], prm["li2"], prm["ls_g"])


def _s5_output_kernel(y1_ref, h0_ref, h1_ref, zc_ref, y_ref, perm_scr):
    @pl.when(jnp.logical_and(pl.program_id(0) == 0, pl.program_id(1) == 0))
    def _():
        _fill_group_perm(perm_scr, False)

    parts = []
    for j in range(S5_GB):
        z = jnp.concatenate([zc_ref[0, j], zc_ref[1, j]], axis=0)
        h = jnp.concatenate([h0_ref[:, j, :], h1_ref[:, j, :]], axis=1).astype(BF16)
        yj = y1_ref[0, j].astype(F32) + y1_ref[1, j].astype(F32) + jnp.dot(h, z, preferred_element_type=F32)
        parts.append(yj.astype(BF16))
    ynat = jnp.dot(jnp.concatenate(parts, axis=1), perm_scr[...], preferred_element_type=F32)
    gl = S5_GB * S5_GS
    for s in range(S5_T):
        y_ref[:, s, :] = ynat[:, s * gl:(s + 1) * gl]


def _s5_outputs(y1, h0, h1, zc):
    _, groups, nc, width = y1.shape
    halves = 2
    nch = nc // halves
    gl = S5_GB * S5_GS
    states = BS((nch, S5_GB, 2 * S5_P), lambda i, c: (c, i, 0))
    return pl.pallas_call(
        _s5_output_kernel,
        out_shape=SDS((nc, S5_T, groups * S5_GS), F32),
        grid=(groups // S5_GB, halves),
        in_specs=[BS((2, S5_GB, nch, width), lambda i, c: (0, i, c, 0)), states, states,
                  BS((2, S5_GB, 2 * S5_P, width), lambda i, c: (0, i, 0, 0))],
        out_specs=BS((nch, S5_T, gl), lambda i, c: (c, 0, i)),
        scratch_shapes=[pltpu.VMEM((S5_GB * width, S5_GB * width), BF16)],
        compiler_params=_cp("arbitrary", "arbitrary"),
        name="s5_outputs",
    )(y1, h0, h1, zc)


def _s5_glu_kernel(x_ref, gain_ref, sc_ref, sh_ref, ys_ref, skip_ref, wv_ref, wg_ref, bv_ref, bg_ref,
                   xo_ref, gt_ref, o_ref, z_scr):
    @pl.when(pl.program_id(1) == 0)
    def _():
        h = _norm_mod(x_ref[...], gain_ref[...], sc_ref[...], sh_ref[...])
        z_scr[...] = jax.nn.gelu(skip_ref[...] * h + ys_ref[...]).astype(BF16)

    z = z_scr[...]
    val = jnp.dot(z, wv_ref[...], preferred_element_type=F32) + bv_ref[...]
    gate = jnp.dot(z, wg_ref[...], preferred_element_type=F32) + bg_ref[...]
    o_ref[...] = xo_ref[...] + gt_ref[...] * (val * (1.0 / (1.0 + jnp.exp(-gate))))


def _s5_glu(x, gain, mod3, ys, skip, w_bf, bias, n_lat):
    np_, d = x.shape
    tn = 512
    per = d // tn
    nlt = n_lat // TM
    b2 = bias.reshape(1, 2 * d)
    return pl.pallas_call(
        _s5_glu_kernel,
        out_shape=SDS((np_, d), F32),
        grid=(np_ // TM, per),
        in_specs=[BS((TM, d), lambda i, j: (i, 0)),
                  BS((1, d), lambda i, j: (0, 0)),
                  _mod_spec_full(nlt, d, 1),
                  _mod_spec_full(nlt, d, 0),
                  BS((TM, d), lambda i, j: (i, 0)),
                  BS((1, d), lambda i, j: (0, 0)),
                  BS((d, tn), lambda i, j: (0, j)),
                  BS((d, tn), lambda i, j: (0, per + j)),
                  BS((1, tn), lambda i, j: (0, j)),
                  BS((1, tn), lambda i, j: (0, per + j)),
                  BS((TM, tn), lambda i, j: (i, j)),
                  _mod_spec(nlt, d, tn, 2)],
        out_specs=BS((TM, tn), lambda i, j: (i, j)),
        scratch_shapes=[pltpu.VMEM((TM, d), BF16)],
        compiler_params=_cp("parallel", "arbitrary"),
        name="s5_glu",
    )(x, gain.reshape(1, d), mod3, mod3, ys, skip.reshape(1, d), w_bf, w_bf, b2, b2, x, mod3)


def _norm_mod_rows_kernel(x_ref, gain_ref, sc_ref, sh_ref, o_ref):
    o_ref[...] = _norm_mod(x_ref[...], gain_ref[...], sc_ref[...], sh_ref[...])


def _norm_mod_rows(x, gain, mod3, n_lat):
    np_, d = x.shape
    nlt = n_lat // TM
    return pl.pallas_call(
        _norm_mod_rows_kernel,
        out_shape=SDS((np_, d), F32),
        grid=(np_ // TM, 1),
        in_specs=[BS((TM, d), lambda i, j: (i, 0)),
                  BS((1, d), lambda i, j: (0, 0)),
                  _mod_spec_full(nlt, d, 1),
                  _mod_spec_full(nlt, d, 0)],
        out_specs=BS((TM, d), lambda i, j: (i, 0)),
        compiler_params=_cp("parallel", "arbitrary"),
        name="norm_mod_rows",
    )(x, gain.reshape(1, d), mod3, mod3)


def _s5_layer(x, mod3, norm_gain, lam_re, lam_im, log_step, b_re, b_im, c_re, c_im, skip, w_glu, b_glu,
              n_lat, n_ctx):
    np_, d = x.shape
    groups = d // S5_GS
    n_tok = n_lat + n_ctx
    nc = n_tok // S5_T
    h = _norm_mod_rows(x, norm_gain, mod3, n_lat)
    h3 = h[:n_tok].reshape(nc, S5_T, d)
    prm = dict(
        lr_row=lam_re.reshape(2, groups, 1, S5_P), li_row=lam_im.reshape(2, groups, 1, S5_P),
        lr_col=lam_re.reshape(2, groups, S5_P, 1), li_col=lam_im.reshape(2, groups, S5_P, 1),
        ls=log_step.reshape(2, groups, 1, 1), ls_g=log_step.reshape(2, groups, 1),
        bre_t=b_re.transpose(0, 1, 3, 2), bim_t=b_im.transpose(0, 1, 3, 2),
        cre_t=c_re.transpose(0, 1, 3, 2), cim_t=c_im.transpose(0, 1, 3, 2),
        lr2=jnp.concatenate([lam_re, lam_re], axis=-1), li2=jnp.concatenate([lam_im, lam_im], axis=-1))
    y1, s, zc = _s5_chunks(h3, prm)
    h0, h1 = _s5_states(s, prm, n_lat // S5_T)
    y = _s5_outputs(y1, h0, h1, zc).reshape(n_tok, d)
    ys = jnp.concatenate([y, jnp.zeros((np_ - n_tok, d), F32)], axis=0)
    return _s5_glu(x, norm_gain, mod3, ys, skip, w_glu.astype(BF16), b_glu, n_lat)


def kernel(x, c, ctx, c_ctx, w_mod, b_mod, norm_mix, norm_ffn, router_w, router_b, w_gate_up, b_gate_up, w_down, b_down, win_w_qkv, win_w_o, win_q_gain, win_k_gain, win_sinks, full_w_qkv, full_w_o, full_q_gain, full_k_gain, hy_w_in, hy_b_in, hy_w_short, hy_b_short, hy_filt_w1, hy_filt_b1, hy_filt_w2, hy_filt_b2, hy_filt_w3, hy_filt_b3, hy_sin_freq, hy_decay, hy_skip, hy_w_out, hy_b_out, s5_lam_re, s5_lam_im, s5_log_step, s5_b_re, s5_b_im, s5_c_re, s5_c_im, s5_skip, s5_w_glu, s5_b_glu):
    batch, n_lat, d = x.shape
    n_ctx = ctx.shape[1]
    assert batch == 1 and n_lat % TM == 0 and n_ctx % TOK_DMA == 0 and (n_lat + n_ctx) % S5_T == 0
    depth = w_mod.shape[0]
    n_tok = n_lat + n_ctx
    pad = (-n_ctx) % TM
    cc = jnp.zeros((8, d), F32).at[0].set(c[0]).at[1].set(c_ctx)
    mod = _modulation(cc, w_mod, b_mod)
    xs = jnp.concatenate([x[0], ctx[0], jnp.zeros((pad, d), F32)], axis=0)
    cos2, sin2 = _rope_tables(n_lat)
    for i in range(depth):
        kind, j = i % 4, i // 4
        mod3 = mod[i, :2].reshape(2, 1, 6 * d)
        if kind == 0:
            xs = _attn_layer(xs, mod3, norm_mix[i], win_w_qkv[j], win_w_o[j], win_q_gain[j], win_k_gain[j],
                             win_sinks[j], cos2, sin2, n_lat, n_ctx, True)
        elif kind == 1:
            xs = _attn_layer(xs, mod3, norm_mix[i], full_w_qkv[j], full_w_o[j], full_q_gain[j], full_k_gain[j],
                             None, cos2, sin2, n_lat, n_ctx, False)
        elif kind == 2:
            xs = _hyena_layer(xs, mod3, norm_mix[i], hy_w_in[j], hy_b_in[j], hy_w_short[j], hy_b_short[j],
                              hy_filt_w1[j], hy_filt_b1[j], hy_filt_w2[j], hy_filt_b2[j], hy_filt_w3[j],
                              hy_filt_b3[j], hy_sin_freq[j], hy_decay[j], hy_skip[j], hy_w_out[j], hy_b_out[j],
                              n_lat, n_ctx)
        else:
            xs = _s5_layer(xs, mod3, norm_mix[i], s5_lam_re[j], s5_lam_im[j], s5_log_step[j], s5_b_re[j],
                           s5_b_im[j], s5_c_re[j], s5_c_im[j], s5_skip[j], s5_w_glu[j], s5_b_glu[j], n_lat, n_ctx)
        xs = _moe(xs, norm_ffn[i], mod3, router_w[i], router_b[i], i, w_gate_up, b_gate_up[i],
                  w_down, b_down[i], n_lat, n_tok)
    return xs[:n_lat].reshape(x.shape)
```

```python
import functools
import math

import numpy as np
import jax
import jax.numpy as jnp
from jax import lax
from jax.experimental import pallas as pl
from jax.experimental.pallas import tpu as pltpu

F32, BF16, I32 = jnp.float32, jnp.bfloat16, jnp.int32
SDS = jax.ShapeDtypeStruct
BS = pl.BlockSpec

HEAD_DIM = 128
N_KV_HEADS = 4
GROUP = 4
GRID_W = 64
WINDOW = 128
ROPE_BASE = 10000.0
TOP_K = 4
SWIGLU_LIMIT = 7.0
SWIGLU_ALPHA = 1.702
EPS = 1e-6
NEG_INF = -1e30

LANES = 128
TM = 512
TQ = 256
TK = 1024
ATT_RB = 128
ATT_UNROLL = 8
LOG2E = math.log2(math.e)
MOE_BLK = 256
TOK_DMA = 256
VMEM_LIMIT = 56 * 1024 * 1024


def _cp(*sem):
    return pltpu.CompilerParams(dimension_semantics=sem, vmem_limit_bytes=VMEM_LIMIT)


def _norm_mod(x, gain, scale, shift):
    ms = jnp.mean(x * x, axis=-1, keepdims=True)
    return (x * lax.rsqrt(ms + EPS)) * gain * (1.0 + scale) + shift


def _split3(a):
    a0 = a.astype(BF16)
    r1 = a - a0.astype(F32)
    a1 = r1.astype(BF16)
    a2 = (r1 - a1.astype(F32)).astype(BF16)
    return a0, a1, a2


def _dot_f32(a, b):
    a0, a1, a2 = _split3(a)
    b0, b1, b2 = _split3(b)
    d = lambda u, v: jnp.dot(u, v, preferred_element_type=F32)
    return (d(a0, b0) + (d(a0, b1) + d(a1, b0)) + (d(a0, b2) + d(a1, b1) + d(a2, b0)))


def _dot_select(a, e, select_on_left=False):
    a0, a1, a2 = _split3(a)
    e = e.astype(BF16)
    d = (lambda u: jnp.dot(e, u, preferred_element_type=F32)) if select_on_left else (
        lambda u: jnp.dot(u, e, preferred_element_type=F32))
    return d(a0) + (d(a1) + d(a2))


def _mod_kernel(c_ref, w_ref, b_ref, o_ref):
    cc = c_ref[...]
    s = cc * (1.0 / (1.0 + jnp.exp(-cc)))
    o_ref[...] = jnp.dot(s.astype(BF16), w_ref[...].astype(BF16), preferred_element_type=F32) + b_ref[...]


def _modulation(cc, w_mod, b_mod):
    depth, d, d6 = w_mod.shape
    tn = 1024
    return pl.pallas_call(
        _mod_kernel,
        out_shape=SDS((depth, 8, d6), F32),
        grid=(depth, d6 // tn),
        in_specs=[BS((8, d), lambda l, j: (0, 0)),
                  BS((None, d, tn), lambda l, j: (l, 0, j)),
                  BS((None, 1, tn), lambda l, j: (l, 0, j))],
        out_specs=BS((None, 8, tn), lambda l, j: (l, 0, j)),
        compiler_params=_cp("parallel", "parallel"),
        name="modulation",
    )(cc, w_mod, b_mod.reshape(depth, 1, d6))


def _mod_spec(n_lat_tiles, d, tn, chunk):
    per = d // tn
    return BS((None, 1, tn), lambda i, j: (jnp.where(i < n_lat_tiles, 0, 1), 0, chunk * per + j))


def _mod_spec_full(n_lat_tiles, d, chunk):
    return BS((None, 1, d), lambda i, j: (jnp.where(i < n_lat_tiles, 0, 1), 0, chunk))


def _qkv_kernel(x_ref, gain_ref, sc_ref, sh_ref, w_ref, qg_ref, kg_ref, cos_ref, sin_ref, o_ref, h_scr,
                *, n_lat_tiles):
    i = pl.program_id(0)
    j = pl.program_id(1)

    @pl.when(j == 0)
    def _():
        h_scr[...] = _norm_mod(x_ref[...], gain_ref[...], sc_ref[...], sh_ref[...]).astype(BF16)

    acc = jnp.dot(h_scr[...], w_ref[...], preferred_element_type=F32)

    @pl.when(j < 5)
    def _():
        gain = jnp.where(j < 4, qg_ref[...], kg_ref[...])
        post = jnp.where(j < 4, HEAD_DIM ** -0.5 * LOG2E, 1.0)
        is_lat = i < n_lat_tiles
        for hh in range(4):
            t = acc[:, hh * HEAD_DIM:(hh + 1) * HEAD_DIM]
            t = t * lax.rsqrt(jnp.mean(t * t, axis=-1, keepdims=True) + EPS) * gain
            rot = pltpu.roll(t, HEAD_DIM // 2, axis=1)
            tr = t * cos_ref[...] + rot * sin_ref[...]
            t = jnp.where(is_lat, tr, t) * post
            o_ref[:, hh * HEAD_DIM:(hh + 1) * HEAD_DIM] = t.astype(BF16)

    @pl.when(j == 5)
    def _():
        o_ref[...] = acc.astype(BF16)


def _qkv(x, gain, mod3, w_bf, q_gain, k_gain, cos2, sin2, n_lat):
    np_, d = x.shape
    nt, nlt = np_ // TM, n_lat // TM
    width = w_bf.shape[1]
    tn = 4 * HEAD_DIM
    return pl.pallas_call(
        functools.partial(_qkv_kernel, n_lat_tiles=nlt),
        out_shape=SDS((np_, width), BF16),
        grid=(nt, width // tn),
        in_specs=[BS((TM, d), lambda i, j: (i, 0)),
                  BS((1, d), lambda i, j: (0, 0)),
                  _mod_spec_full(nlt, d, 1),
                  _mod_spec_full(nlt, d, 0),
                  BS((d, tn), lambda i, j: (0, j)),
                  BS((1, HEAD_DIM), lambda i, j: (0, 0)),
                  BS((1, HEAD_DIM), lambda i, j: (0, 0)),
                  BS((TM, HEAD_DIM), lambda i, j: (jnp.minimum(i, nlt - 1), 0)),
                  BS((TM, HEAD_DIM), lambda i, j: (jnp.minimum(i, nlt - 1), 0))],
        out_specs=BS((TM, tn), lambda i, j: (i, j)),
        scratch_shapes=[pltpu.VMEM((TM, d), BF16)],
        compiler_params=_cp("parallel", "arbitrary"),
        name="qkv_proj",
    )(x, gain.reshape(1, d), mod3, mod3, w_bf, q_gain.reshape(1, HEAD_DIM), k_gain.reshape(1, HEAD_DIM),
      cos2, sin2)


def _rope_tables(n_lat):
    pairs = HEAD_DIM // 4
    t = jnp.arange(n_lat, dtype=I32)
    row = (t // GRID_W).astype(F32)
    col = (t % GRID_W).astype(F32)
    freqs = ROPE_BASE ** (-jnp.arange(pairs, dtype=F32) / pairs)
    ang = jnp.concatenate([row[:, None] * freqs, col[:, None] * freqs], axis=-1)
    cos, sin = jnp.cos(ang), jnp.sin(ang)
    return jnp.concatenate([cos, cos], axis=-1), jnp.concatenate([-sin, sin], axis=-1)


def _attn_kernel(sink_ref, q_ref, k_ref, v_ref, o_ref, q_scr, m_scr, acc_scr,
                 *, windowed, has_sink, n_lat, n_ctx, n_ctx_pad):
    kvh = pl.program_id(0)
    i = pl.program_id(1)
    is_lat = i < n_lat // TQ
    rows = GROUP * TQ
    for h in range(GROUP):
        q_scr[h * TQ:(h + 1) * TQ, :] = q_ref[:, h * HEAD_DIM:(h + 1) * HEAD_DIM]
    m_scr[...] = jnp.full((rows, LANES), NEG_INF, F32)
    acc_scr[...] = jnp.zeros((rows, 2 * HEAD_DIM), F32)

    def step(start, size, mask_fn):
        k = k_ref[pl.ds(start, size), :]
        v = jnp.concatenate([v_ref[pl.ds(start, size), :], jnp.ones((size, HEAD_DIM), BF16)], axis=1)
        for r in range(rows // ATT_RB):
            sl = slice(r * ATT_RB, (r + 1) * ATT_RB)
            s = lax.dot_general(q_scr[sl, :], k, (((1,), (1,)), ((), ())), preferred_element_type=F32)
            if mask_fn is not None:
                s = jnp.where(mask_fn(r), s, NEG_INF)
            m_old = m_scr[sl, :]
            m_new = jnp.maximum(m_old, jnp.max(s, axis=-1, keepdims=True))
            alpha = jnp.exp2(m_old - m_new)
            p = jnp.exp2(s - jnp.concatenate([m_new] * (size // LANES), axis=1))
            acc_scr[sl, :] = (jnp.concatenate([alpha, alpha], axis=1) * acc_scr[sl, :]
                              + jnp.dot(p.astype(BF16), v, preferred_element_type=F32))
            m_scr[sl, :] = m_new

    if windowed:
        wk = TQ + 2 * WINDOW

        @pl.when(is_lat)
        def _():
            start = pl.multiple_of(jnp.clip(i * TQ - WINDOW, 0, n_lat - wk), WINDOW)

            def band(r):
                q_pos = i * TQ + (r * ATT_RB) % TQ + lax.broadcasted_iota(I32, (ATT_RB, wk), 0)
                k_pos = start + lax.broadcasted_iota(I32, (ATT_RB, wk), 1)
                return jnp.abs(q_pos - k_pos) <= WINDOW

            step(start, wk, band)
    else:
        def body(c, carry):
            for half in range(ATT_UNROLL):
                step(pl.multiple_of((c * ATT_UNROLL + half) * TK, TK), TK, None)
            return carry

        trips = n_lat // (TK * ATT_UNROLL)
        lax.fori_loop(0, jnp.where(is_lat, trips, 0), body, 0)
        if n_lat // TK > trips * ATT_UNROLL:
            @pl.when(is_lat)
            def _():
                for c in range(trips * ATT_UNROLL, n_lat // TK):
                    step(c * TK, TK, None)

    ctx_mask = None
    if n_ctx != n_ctx_pad:
        ctx_mask = lambda r: lax.broadcasted_iota(I32, (ATT_RB, n_ctx_pad), 1) < n_ctx
    step(n_lat, n_ctx_pad, ctx_mask)

    for h in range(GROUP):
        sl = slice(h * TQ, (h + 1) * TQ)
        acc = acc_scr[sl, :]
        num, den = acc[:, :HEAD_DIM], acc[:, HEAD_DIM:]
        if has_sink:
            m = m_scr[sl, :]
            sink = sink_ref[kvh * GROUP + h] * LOG2E
            m_f = jnp.maximum(m, sink)
            scale = jnp.exp2(m - m_f)
            num = num * scale
            den = den * scale + jnp.exp2(sink - m_f)
        o_ref[:, h * HEAD_DIM:(h + 1) * HEAD_DIM] = (num * (1.0 / den)).astype(BF16)


def _attention(qkv, sinks, n_lat, n_ctx, windowed):
    np_ = qkv.shape[0]
    n_heads = N_KV_HEADS * GROUP
    has_sink = sinks is not None
    if sinks is None:
        sinks = jnp.zeros((n_heads,), F32)
    rows = GROUP * TQ
    kern = functools.partial(_attn_kernel, windowed=windowed, has_sink=has_sink, n_lat=n_lat, n_ctx=n_ctx,
                             n_ctx_pad=np_ - n_lat)
    return pl.pallas_call(
        kern,
        out_shape=SDS((np_, n_heads * HEAD_DIM), BF16),
        grid_spec=pltpu.PrefetchScalarGridSpec(
            num_scalar_prefetch=1,
            grid=(N_KV_HEADS, np_ // TQ),
            in_specs=[BS((TQ, GROUP * HEAD_DIM), lambda kv, i, s: (i, kv)),
                      BS((np_, HEAD_DIM), lambda kv, i, s: (0, n_heads + kv)),
                      BS((np_, HEAD_DIM), lambda kv, i, s: (0, n_heads + N_KV_HEADS + kv))],
            out_specs=BS((TQ, GROUP * HEAD_DIM), lambda kv, i, s: (i, kv)),
            scratch_shapes=[pltpu.VMEM((rows, HEAD_DIM), BF16), pltpu.VMEM((rows, LANES), F32),
                            pltpu.VMEM((rows, 2 * HEAD_DIM), F32)]),
        compiler_params=_cp("parallel", "arbitrary"),
        name="window_attention" if windowed else "full_attention",
    )(sinks.astype(F32), qkv, qkv, qkv)


def _proj_res_kernel(a_ref, w_ref, b_ref, x_ref, gt_ref, o_ref):
    y = jnp.dot(a_ref[...], w_ref[...], preferred_element_type=F32) + b_ref[...]
    o_ref[...] = x_ref[...] + gt_ref[...] * y


def _proj_res(a, w_bf, bias, x, mod3, n_lat, gate_chunk):
    np_, d = x.shape
    kdim = a.shape[1]
    tn = 1024
    nlt = n_lat // TM
    return pl.pallas_call(
        _proj_res_kernel,
        out_shape=SDS((np_, d), F32),
        grid=(np_ // TM, d // tn),
        in_specs=[BS((TM, kdim), lambda i, j: (i, 0)),
                  BS((kdim, tn), lambda i, j: (0, j)),
                  BS((1, tn), lambda i, j: (0, j)),
                  BS((TM, tn), lambda i, j: (i, j)),
                  _mod_spec(nlt, d, tn, gate_chunk)],
        out_specs=BS((TM, tn), lambda i, j: (i, j)),
        compiler_params=_cp("parallel", "parallel"),
        name="proj_residual",
    )(a, w_bf, bias.reshape(1, d), x, mod3)


def _router_kernel(x_ref, gain_ref, sc_ref, sh_ref, wr_ref, br_ref, h_ref, idx_ref, gate_ref):
    h = _norm_mod(x_ref[...], gain_ref[...], sc_ref[...], sh_ref[...])
    h_ref[...] = h
    vals = _dot_f32(h, wr_ref[...]) + br_ref[...]
    lane = lax.broadcasted_iota(I32, vals.shape, 1).astype(F32)
    tops, idxs = [], []
    for _ in range(TOP_K):
        m = jnp.max(vals, axis=-1, keepdims=True)
        sel = jnp.min(jnp.where(vals == m, lane, float(LANES)), axis=-1, keepdims=True)
        tops.append(m)
        idxs.append(sel)
        vals = jnp.where(lane == sel, 2.0 * NEG_INF, vals)
    es = [jnp.exp(t - tops[0]) for t in tops]
    inv = 1.0 / (es[0] + es[1] + es[2] + es[3])
    idx_out = jnp.zeros(vals.shape, F32)
    gate_out = jnp.zeros(vals.shape, F32)
    for k in range(TOP_K):
        idx_out = jnp.where(lane == float(k), idxs[k], idx_out)
        gate_out = jnp.where(lane == float(k), es[k] * inv, gate_out)
    idx_ref[...] = idx_out.astype(I32)
    gate_ref[...] = gate_out


def _router(x, gain, mod3, wr_pad, br_pad, n_lat):
    np_, d = x.shape
    nlt = n_lat // TM
    return pl.pallas_call(
        _router_kernel,
        out_shape=(SDS((np_, d), F32), SDS((np_, LANES), I32), SDS((np_, LANES), F32)),
        grid=(np_ // TM, 1),
        in_specs=[BS((TM, d), lambda i, j: (i, 0)),
                  BS((1, d), lambda i, j: (0, 0)),
                  _mod_spec_full(nlt, d, 4),
                  _mod_spec_full(nlt, d, 3),
                  BS((d, LANES), lambda i, j: (0, 0)),
                  BS((1, LANES), lambda i, j: (0, 0))],
        out_specs=(BS((TM, d), lambda i, j: (i, 0)),
                   BS((TM, LANES), lambda i, j: (i, 0)),
                   BS((TM, LANES), lambda i, j: (i, 0))),
        compiler_params=_cp("parallel", "arbitrary"),
        name="router_top4",
    )(x, gain.reshape(1, d), mod3, mod3, wr_pad, br_pad)


def _rank_kernel(idx_ref, rank_ref, cnt_ref, base_scr, *, n_tok):
    i = pl.program_id(0)

    @pl.when(i == 0)
    def _():
        base_scr[...] = jnp.zeros(base_scr.shape, F32)

    idx = idx_ref[...]
    lane = lax.broadcasted_iota(I32, idx.shape, 1)
    row = i * TM + lax.broadcasted_iota(I32, idx.shape, 0)
    valid = row < n_tok
    hots = [jnp.where(valid & (lane == idx[:, k:k + 1]), 1.0, 0.0) for k in range(TOP_K)]
    cnt = hots[0] + hots[1] + hots[2] + hots[3]
    tri = jnp.where(lax.broadcasted_iota(I32, (TM, TM), 0) > lax.broadcasted_iota(I32, (TM, TM), 1), 1.0, 0.0)
    before = base_scr[...] + jnp.dot(tri.astype(BF16), cnt.astype(BF16), preferred_element_type=F32)
    out = jnp.zeros(idx.shape, F32)
    for k in range(TOP_K):
        out = jnp.where(lane == k, jnp.sum(hots[k] * before, axis=-1, keepdims=True), out)
    rank_ref[...] = out.astype(I32)
    base_scr[...] = base_scr[...] + jnp.sum(cnt, axis=0, keepdims=True)
    cnt_ref[...] = base_scr[...].astype(I32)


def _expert_ranks(idx, n_tok):
    np_ = idx.shape[0]
    return pl.pallas_call(
        functools.partial(_rank_kernel, n_tok=n_tok),
        out_shape=(SDS((np_, LANES), I32), SDS((1, LANES), I32)),
        grid=(np_ // TM,),
        in_specs=[BS((TM, LANES), lambda i: (i, 0))],
        out_specs=(BS((TM, LANES), lambda i: (i, 0)), BS((1, LANES), lambda i: (0, 0))),
        scratch_shapes=[pltpu.VMEM((1, LANES), F32)],
        compiler_params=_cp("arbitrary"),
        name="expert_ranks",
    )(idx)


def _row_copy(src, dst, sem):
    return pltpu.make_async_copy(src, dst, sem)


def _dispatch_kernel(dest_ref, h_ref, xs_init, xs_hbm, sem):
    del xs_init

    def issue(t, carry):
        src = h_ref.at[pl.ds(t, 1)]
        for k in range(TOP_K):
            _row_copy(src, xs_hbm.at[pl.ds(dest_ref[t * TOP_K + k], 1)], sem).start(priority=k % 2)
        return carry

    lax.fori_loop(0, TOK_DMA, issue, 0)

    def drain(t, carry):
        for k in range(TOP_K):
            _row_copy(h_ref.at[pl.ds(0, 1)], xs_hbm.at[pl.ds(0, 1)], sem).wait()
        return carry

    lax.fori_loop(0, TOK_DMA, drain, 0)


def _dispatch(dest_flat, h, n_tok, n_rows):
    d = h.shape[1]
    xs0 = jnp.zeros((n_rows, d), F32)
    return pl.pallas_call(
        _dispatch_kernel,
        out_shape=SDS((n_rows, d), F32),
        grid=(n_tok // TOK_DMA,),
        in_specs=[BS((TOK_DMA * TOP_K,), lambda i: (i,), memory_space=pltpu.SMEM),
                  BS((TOK_DMA, d), lambda i: (i, 0)),
                  BS(memory_space=pl.ANY)],
        out_specs=BS(memory_space=pl.ANY),
        scratch_shapes=[pltpu.SemaphoreType.DMA(())],
        input_output_aliases={2: 0},
        compiler_params=pltpu.CompilerParams(dimension_semantics=("arbitrary",), has_side_effects=True),
        name="moe_dispatch",
    )(dest_flat, h, xs0)


def _ffn_kernel(be_ref, nu_ref, x_ref, wgu_ref, bgu_ref, wdn_ref, bdn_ref, o_ref, *, ff):
    b = pl.program_id(0)

    @pl.when(b < nu_ref[0])
    def _():
        gu = jnp.dot(x_ref[...].astype(BF16), wgu_ref[...].astype(BF16), preferred_element_type=F32) + bgu_ref[...]
        glu = jnp.minimum(gu[:, :ff], SWIGLU_LIMIT)
        lin = jnp.clip(gu[:, ff:], -SWIGLU_LIMIT, SWIGLU_LIMIT)
        act = glu * (1.0 / (1.0 + jnp.exp(-SWIGLU_ALPHA * glu))) * (lin + 1.0)
        o_ref[...] = jnp.dot(act.astype(BF16), wdn_ref[...].astype(BF16), preferred_element_type=F32) + bdn_ref[...]

    @pl.when(b >= nu_ref[0])
    def _():
        o_ref[...] = jnp.zeros(o_ref.shape, F32)


def _expert_ffn(blk_expert, n_used, xs, layer, w_gu, bgu, w_dn, bdn):
    n_rows, d = xs.shape
    _, n_exp, _, ff2 = w_gu.shape
    ff = ff2 // 2
    return pl.pallas_call(
        functools.partial(_ffn_kernel, ff=ff),
        out_shape=SDS((n_rows, d), F32),
        grid_spec=pltpu.PrefetchScalarGridSpec(
            num_scalar_prefetch=2,
            grid=(n_rows // MOE_BLK,),
            in_specs=[BS((MOE_BLK, d), lambda b, be, nu: (b, 0)),
                      BS((None, None, d, ff2), lambda b, be, nu: (layer, be[b], 0, 0)),
                      BS((None, 1, ff2), lambda b, be, nu: (be[b], 0, 0)),
                      BS((None, None, ff, d), lambda b, be, nu: (layer, be[b], 0, 0)),
                      BS((None, 1, d), lambda b, be, nu: (be[b], 0, 0))],
            out_specs=BS((MOE_BLK, d), lambda b, be, nu: (b, 0))),
        compiler_params=_cp("arbitrary"),
        name="expert_ffn",
    )(blk_expert, n_used, xs, w_gu, bgu.reshape(n_exp, 1, ff2), w_dn, bdn.reshape(n_exp, 1, d))


def _combine_kernel(dest_ref, g_ref, x_ref, gt_ref, y_hbm, o_ref, buf, sem):
    def issue(t, carry):
        for k in range(TOP_K):
            _row_copy(y_hbm.at[pl.ds(dest_ref[t * TOP_K + k], 1)], buf.at[k, pl.ds(t, 1)], sem).start(
                priority=k % 2)
        return carry

    lax.fori_loop(0, TOK_DMA, issue, 0)

    def drain(t, carry):
        for k in range(TOP_K):
            _row_copy(y_hbm.at[pl.ds(0, 1)], buf.at[k, pl.ds(0, 1)], sem).wait()
        return carry

    lax.fori_loop(0, TOK_DMA, drain, 0)
    g = g_ref[...]
    acc = g[:, 0:1] * buf[0]
    for k in range(1, TOP_K):
        acc = acc + g[:, k:k + 1] * buf[k]
    o_ref[...] = x_ref[...] + gt_ref[...] * acc


def _combine(dest_flat, gates, x, mod3, y, n_lat):
    np_, d = x.shape
    nlt = n_lat // TOK_DMA
    return pl.pallas_call(
        _combine_kernel,
        out_shape=SDS((np_, d), F32),
        grid=(np_ // TOK_DMA,),
        in_specs=[BS((TOK_DMA * TOP_K,), lambda i: (i,), memory_space=pltpu.SMEM),
                  BS((TOK_DMA, LANES), lambda i: (i, 0)),
                  BS((TOK_DMA, d), lambda i: (i, 0)),
                  BS((None, 1, d), lambda i: (jnp.where(i < nlt, 0, 1), 0, 5)),
                  BS(memory_space=pl.ANY)],
        out_specs=BS((TOK_DMA, d), lambda i: (i, 0)),
        scratch_shapes=[pltpu.VMEM((TOP_K, TOK_DMA, d), F32), pltpu.SemaphoreType.DMA(())],
        compiler_params=_cp("arbitrary"),
        name="moe_combine",
    )(dest_flat, gates, x, mod3, y)


def _moe(x, gain, mod3, router_w, router_b, layer, w_gu, bgu, w_dn, bdn, n_lat, n_tok):
    np_, d = x.shape
    n_exp = router_w.shape[1]
    wr_pad = jnp.zeros((d, LANES), F32).at[:, :n_exp].set(router_w)
    br_pad = jnp.full((1, LANES), NEG_INF, F32).at[0, :n_exp].set(router_b)
    h, idx, gates = _router(x, gain, mod3, wr_pad, br_pad, n_lat)
    rank, counts = _expert_ranks(idx, n_tok)
    counts = counts[0, :n_exp]
    padded = (counts + MOE_BLK - 1) // MOE_BLK * MOE_BLK
    pad_end = jnp.cumsum(padded)
    pad_start = pad_end - padded
    dest = pad_start[idx[:, :TOP_K]] + rank[:, :TOP_K]
    dest = jnp.where(jnp.arange(np_)[:, None] < n_tok, dest, 0).astype(I32).reshape(-1)
    n_blocks = -(-(n_tok * TOP_K + n_exp * (MOE_BLK - 1)) // MOE_BLK)
    blk_row = jnp.arange(n_blocks, dtype=I32) * MOE_BLK
    blk_expert = jnp.minimum(jnp.sum(pad_end[None, :] <= blk_row[:, None], axis=1), n_exp - 1).astype(I32)
    n_used = (pad_end[-1:] // MOE_BLK).astype(I32)
    xs = _dispatch(dest, h, n_tok, n_blocks * MOE_BLK)
    y = _expert_ffn(blk_expert, n_used, xs, layer, w_gu, bgu, w_dn, bdn)
    return _combine(dest, gates, x, mod3, y, n_lat)


def _attn_layer(x, mod3, norm_gain, w_qkv, w_o, q_gain, k_gain, sinks, cos2, sin2, n_lat, n_ctx, windowed):
    d = x.shape[1]
    qkv = _qkv(x, norm_gain, mod3, w_qkv.astype(BF16), q_gain, k_gain, cos2, sin2, n_lat)
    o = _attention(qkv, sinks, n_lat, n_ctx, windowed)
    return _proj_res(o, w_o.astype(BF16), jnp.zeros((d,), F32), x, mod3, n_lat, 2)


def _linear_norm_kernel(x_ref, gain_ref, sc_ref, sh_ref, w_ref, b_ref, o_ref, h_scr):
    @pl.when(pl.program_id(1) == 0)
    def _():
        h_scr[...] = _norm_mod(x_ref[...], gain_ref[...], sc_ref[...], sh_ref[...]).astype(BF16)

    o_ref[...] = (jnp.dot(h_scr[...], w_ref[...], preferred_element_type=F32) + b_ref[...]).astype(o_ref.dtype)


def _linear_norm(x, gain, mod3, w_bf, bias, n_lat, out_dtype):
    np_, d = x.shape
    width = w_bf.shape[1]
    tn = 1024
    nlt = n_lat // TM
    return pl.pallas_call(
        _linear_norm_kernel,
        out_shape=SDS((np_, width), out_dtype),
        grid=(np_ // TM, width // tn),
        in_specs=[BS((TM, d), lambda i, j: (i, 0)),
                  BS((1, d), lambda i, j: (0, 0)),
                  _mod_spec_full(nlt, d, 1),
                  _mod_spec_full(nlt, d, 0),
                  BS((d, tn), lambda i, j: (0, j)),
                  BS((1, tn), lambda i, j: (0, j))],
        out_specs=BS((TM, tn), lambda i, j: (i, j)),
        scratch_shapes=[pltpu.VMEM((TM, d), BF16)],
        compiler_params=_cp("parallel", "arbitrary"),
        name="linear_norm",
    )(x, gain.reshape(1, d), mod3, mod3, w_bf, bias.reshape(1, width))


HY_NB = 128
HY_SUB = 8
HY_BANDS = 8
HY_MOD_SHIFT = 0.05
HALO = 16


def _short_conv_kernel(z0c, z0p, z0n, z1c, z1p, z1n, z2c, z2p, z2n, w_ref, b_ref, x0_ref, u_ref,
                       *, n_lat_tiles, n_ctx):
    i = pl.program_id(0)
    row = lax.broadcasted_iota(I32, (TM, 1), 0)
    is_ctx = i >= n_lat_tiles
    has_prev = jnp.logical_and(i > 0, i != n_lat_tiles)
    has_next = i < n_lat_tiles - 1
    keep = jnp.logical_or(jnp.logical_not(is_ctx), row < n_ctx)

    def conv(cur, prv, nxt, part):
        zc = jnp.where(keep, cur[...].astype(F32), 0.0)
        first = jnp.where(has_prev, prv[HALO - 1:HALO, :].astype(F32), 0.0)
        last = jnp.where(has_next, nxt[0:1, :].astype(F32), 0.0)
        up = jnp.where(row == 0, first, pltpu.roll(zc, 1, axis=0))
        dn = jnp.where(row == TM - 1, last, pltpu.roll(zc, TM - 1, axis=0))
        w = w_ref[part]
        return up * w[0:1, :] + zc * w[1:2, :] + dn * w[2:3, :] + b_ref[part]

    x0 = conv(z0c, z0p, z0n, 0)
    x1 = conv(z1c, z1p, z1n, 1)
    v = conv(z2c, z2p, z2n, 2)
    x0_ref[...] = x0.astype(BF16)
    u_ref[...] = x1 * v


def _short_conv(z, w_short, b_short, n_lat, n_ctx):
    np_, d3 = z.shape
    d = d3 // 3
    tn = 512
    per = d // tn
    nlt = n_lat // TM
    hb = TM // HALO
    last_halo = np_ // HALO - 1
    specs = []
    for part in range(3):
        specs.append(BS((TM, tn), lambda i, j, part=part: (i, part * per + j)))
        specs.append(BS((HALO, tn), lambda i, j, part=part: (jnp.maximum(i * hb - 1, 0), part * per + j)))
        specs.append(BS((HALO, tn), lambda i, j, part=part: (jnp.minimum((i + 1) * hb, last_halo), part * per + j)))
    w3 = w_short.reshape(3, 3, d).transpose(1, 0, 2)
    b3 = b_short.reshape(3, 1, d)
    return pl.pallas_call(
        functools.partial(_short_conv_kernel, n_lat_tiles=nlt, n_ctx=n_ctx),
        out_shape=(SDS((np_, d), BF16), SDS((np_, d), F32)),
        grid=(np_ // TM, per),
        in_specs=specs + [BS((3, 3, tn), lambda i, j: (0, 0, j)), BS((3, 1, tn), lambda i, j: (0, 0, j))],
        out_specs=(BS((TM, tn), lambda i, j: (i, j)), BS((TM, tn), lambda i, j: (i, j))),
        compiler_params=_cp("parallel", "parallel"),
        name="hyena_short_conv",
    )(z, z, z, z, z, z, z, z, z, w3, b3)


def _filter_kernel(w1_ref, b1_ref, w2_ref, b2_ref, sf_ref, w3f_ref, b3f_ref, w3b_ref, b3b_ref, dec_ref, o_ref,
                   hid_scr, *, n, rows, nsub, s_mul, s_sub, s_blk):
    blk = pl.program_id(0)
    a = lax.broadcasted_iota(I32, (rows, 1), 0)
    lane = lax.broadcasted_iota(I32, (rows, LANES), 1)

    def positions(sub):
        s = s_mul * a + (s_sub * sub + s_blk * blk)
        lag = jnp.where(s < n, s, 2 * n - s)
        return s, lag.astype(F32) / float(n)

    @pl.when(pl.program_id(1) == 0)
    def _():
        band = jnp.where(lane <= HY_BANDS, lane, HY_BANDS - lane).astype(F32)
        phase = jnp.where(lane <= HY_BANDS, 0.5 * jnp.pi, 0.0)
        for sub in range(nsub):
            _, t = positions(sub)
            trig = jnp.sin(2.0 * jnp.pi * t * band + phase)
            feats = jnp.where(lane == 0, t, jnp.where(lane <= 2 * HY_BANDS, trig, 0.0))
            hid = jnp.sin(sf_ref[0:1, :] * (_dot_f32(feats, w1_ref[...]) + b1_ref[...]))
            hid_scr[sub] = jnp.sin(sf_ref[1:2, :] * (_dot_f32(hid, w2_ref[...]) + b2_ref[...]))

    w3f, w3b = w3f_ref[...].astype(BF16), w3b_ref[...].astype(BF16)
    for sub in range(nsub):
        s, t = positions(sub)
        hid = hid_scr[sub].astype(BF16)
        fwd = jnp.dot(hid, w3f, preferred_element_type=F32) + b3f_ref[...]
        bwd = jnp.dot(hid, w3b, preferred_element_type=F32) + b3b_ref[...]
        window = jnp.exp(-t * jnp.abs(dec_ref[...])) + HY_MOD_SHIFT
        g = jnp.where(s < n, fwd, jnp.where(s == n, 0.0, bwd)) * window
        if nsub == 1:
            o_ref[...] = g
        else:
            o_ref[:, sub, :] = g


def _hyena_filter(n, fw1, fb1, fw2, fb2, fw3, fb3, sin_freq, decay, two_d):
    width = fw1.shape[1]
    d = decay.shape[0]
    tn = 512
    per = d // tn
    pad = lambda m, r, c: jnp.zeros((r, c), F32).at[:m.shape[0], :m.shape[1]].set(m)
    w1p = pad(fw1, LANES, LANES)
    b1p = pad(fb1.reshape(1, width), 1, LANES)
    w2p = pad(fw2, LANES, LANES)
    b2p = pad(fb2.reshape(1, width), 1, LANES)
    sfp = pad(sin_freq, 2, LANES)
    w3p = pad(fw3, LANES, 2 * d)
    b3 = fb3.reshape(1, 2 * d)
    if two_d:
        rows, nsub, nblk = 2 * n // HY_NB, HY_SUB, HY_NB // HY_SUB
        s_mul, s_sub, s_blk = HY_NB, 1, HY_SUB
        out_shape = SDS((rows, HY_NB, d), F32)
        out_spec = BS((rows, HY_SUB, tn), lambda b, j: (0, b, j))
    else:
        rows, nsub = min(2 * n, 256), 1
        nblk, s_mul, s_sub, s_blk = 2 * n // rows, 1, 0, rows
        out_shape = SDS((2 * n, d), F32)
        out_spec = BS((rows, tn), lambda b, j: (b, j))
    const = lambda r, c: BS((r, c), lambda b, j: (0, 0))
    return pl.pallas_call(
        functools.partial(_filter_kernel, n=n, rows=rows, nsub=nsub, s_mul=s_mul, s_sub=s_sub, s_blk=s_blk),
        out_shape=out_shape,
        grid=(nblk, per),
        in_specs=[const(LANES, LANES), const(1, LANES), const(LANES, LANES), const(1, LANES), const(2, LANES),
                  BS((LANES, tn), lambda b, j: (0, j)), BS((1, tn), lambda b, j: (0, j)),
                  BS((LANES, tn), lambda b, j: (0, per + j)), BS((1, tn), lambda b, j: (0, per + j)),
                  BS((1, tn), lambda b, j: (0, j))],
        out_specs=out_spec,
        scratch_shapes=[pltpu.VMEM((nsub, rows, LANES), F32)],
        compiler_params=_cp("parallel", "arbitrary"),
        name="hyena_filter",
    )(w1p, b1p, w2p, b2p, sfp, w3p, b3, w3p, b3, decay.reshape(1, d))


def _dft_tables(n_total):
    nb = HY_NB
    na = n_total // nb
    ka = np.arange(na)[:, None]
    a = np.arange(na)[None, :]
    ang_a = 2.0 * np.pi * ((ka * a) % na) / na
    fa = np.concatenate([np.cos(ang_a), -np.sin(ang_a)], axis=0)
    kb = np.arange(nb)[:, None]
    b = np.arange(nb)[None, :]
    ang_b = 2.0 * np.pi * ((kb * b) % nb) / nb
    c, s = np.cos(ang_b), np.sin(ang_b)
    fwd_b = np.block([[c, s], [-s, c]])
    inv_b = np.block([[c, -s], [s, c]]) / float(n_total)
    ang_t = 2.0 * np.pi * ((np.arange(nb)[:, None] * np.arange(na)[None, :]) % n_total) / n_total
    tw = (np.cos(ang_t), -np.sin(ang_t))
    inv_a = np.concatenate([np.cos(ang_a[:na // 2]), -np.sin(ang_a[:na // 2])], axis=1)
    f = lambda m: jnp.asarray(m, F32)
    return dict(na=na, fa=f(fa), fwd_b=f(fwd_b), inv_b=f(inv_b), tw_re=f(tw[0]), tw_im=f(tw[1]), inv_a=f(inv_a))


def _dft_a_kernel(f_ref, x_ref, twr_ref, twi_ref, re_ref, im_ref, *, na):
    f = f_ref[...]
    for sub in range(HY_SUB):
        pq = jnp.dot(f, x_ref[:, sub, :].astype(BF16), preferred_element_type=F32)
        p, q = pq[:na], pq[na:]
        twr, twi = twr_ref[sub], twi_ref[sub]
        re_ref[:, sub, :] = p * twr - q * twi
        im_ref[:, sub, :] = p * twi + q * twr


def _dft_a(x3, f_bf, tw_re, tw_im):
    ka_rows, nb, d = x3.shape
    na = f_bf.shape[0] // 2
    tn = min(d, 512)
    twr = tw_re.reshape(nb, na, 1)
    twi = tw_im.reshape(nb, na, 1)
    tws = BS((HY_SUB, na, 1), lambda b, j: (b, 0, 0))
    out = BS((na, HY_SUB, tn), lambda b, j: (0, b, j))
    return pl.pallas_call(
        functools.partial(_dft_a_kernel, na=na),
        out_shape=(SDS((na, nb, d), F32), SDS((na, nb, d), F32)),
        grid=(nb // HY_SUB, d // tn),
        in_specs=[BS((2 * na, ka_rows), lambda b, j: (0, 0)),
                  BS((ka_rows, HY_SUB, tn), lambda b, j: (0, b, j)),
                  tws, tws],
        out_specs=(out, out),
        compiler_params=_cp("parallel", "parallel"),
        name="dft_stage_a",
    )(f_bf, x3, twr, twi)


def _dft_b_kernel(fwd_ref, inv_ref, ur_ref, ui_ref, gr_ref, gi_ref, twr_ref, twi_ref, br_ref, bi_ref):
    nb = HY_NB
    fwd = fwd_ref[...]
    stack = lambda re, im: jnp.concatenate([re[...], im[...]], axis=0).astype(BF16)
    us = jnp.dot(fwd, stack(ur_ref, ui_ref), preferred_element_type=F32)
    gs = jnp.dot(fwd, stack(gr_ref, gi_ref), preferred_element_type=F32)
    ure, uim, gre, gim = us[:nb], us[nb:], gs[:nb], gs[nb:]
    yre = ure * gre - uim * gim
    yim = ure * gim + uim * gre
    ys = jnp.concatenate([yre, yim], axis=0).astype(BF16)
    bs = jnp.dot(inv_ref[...], ys, preferred_element_type=F32)
    bre, bim = bs[:nb], bs[nb:]
    twr, twi = twr_ref[...], twi_ref[...]
    br_ref[...] = bre * twr + bim * twi
    bi_ref[...] = bim * twr - bre * twi


def _dft_b(ur, ui, gr, gi, fwd_bf, inv_bf, tw_re, tw_im):
    na, nb, d = ur.shape
    tn = min(d, 2048)
    twr = tw_re.T.reshape(na, nb, 1)
    twi = tw_im.T.reshape(na, nb, 1)
    data = BS((None, nb, tn), lambda k, j: (k, 0, j))
    tws = BS((None, nb, 1), lambda k, j: (k, 0, 0))
    return pl.pallas_call(
        _dft_b_kernel,
        out_shape=(SDS((na, nb, d), F32), SDS((na, nb, d), F32)),
        grid=(na, d // tn),
        in_specs=[BS((2 * nb, 2 * nb), lambda k, j: (0, 0)), BS((2 * nb, 2 * nb), lambda k, j: (0, 0)),
                  data, data, data, data, tws, tws],
        out_specs=(data, data),
        compiler_params=_cp("parallel", "parallel"),
        name="dft_stage_b",
    )(fwd_bf, inv_bf, ur, ui, gr, gi, twr, twi)


def _idft_a_kernel(f_ref, br_ref, bi_ref, o_ref):
    f = f_ref[...]
    for sub in range(HY_SUB):
        bs = jnp.concatenate([br_ref[:, sub, :], bi_ref[:, sub, :]], axis=0).astype(BF16)
        o_ref[:, sub, :] = jnp.dot(f, bs, preferred_element_type=F32)


def _idft_a(br, bi, inv_a_bf):
    na, nb, d = br.shape
    tn = min(d, 512)
    data = BS((na, HY_SUB, tn), lambda b, j: (0, b, j))
    return pl.pallas_call(
        _idft_a_kernel,
        out_shape=SDS((na // 2, nb, d), F32),
        grid=(nb // HY_SUB, d // tn),
        in_specs=[BS((na // 2, 2 * na), lambda b, j: (0, 0)), data, data],
        out_specs=BS((na // 2, HY_SUB, tn), lambda b, j: (0, b, j)),
        compiler_params=_cp("parallel", "parallel"),
        name="idft_stage_a",
    )(inv_a_bf, br, bi)


def _ctx_conv_kernel(g_ref, u_ref, o_ref, gg_scr, *, n):
    g = g_ref[...]
    gg_scr[0:2 * n, :] = g
    gg_scr[2 * n:4 * n, :] = g

    def body(s, acc):
        return acc + gg_scr[pl.ds(2 * n - s, n), :] * u_ref[pl.ds(s, 1), :]

    o_ref[...] = lax.fori_loop(0, n, body, jnp.zeros((n, LANES), F32))


def _ctx_conv(g, u):
    n2, d = g.shape
    n = n2 // 2
    return pl.pallas_call(
        functools.partial(_ctx_conv_kernel, n=n),
        out_shape=SDS((n, d), F32),
        grid=(d // LANES,),
        in_specs=[BS((n2, LANES), lambda j: (0, j)), BS((n, LANES), lambda j: (0, j))],
        out_specs=BS((n, LANES), lambda j: (0, j)),
        scratch_shapes=[pltpu.VMEM((2 * n2, LANES), F32)],
        compiler_params=_cp("parallel"),
        name="hyena_ctx_conv",
    )(g, u)


def _gated_proj_res_kernel(a_ref, y_ref, u_ref, skip_ref, w_ref, b_ref, x_ref, gt_ref, o_ref):
    a = (a_ref[...].astype(F32) * (y_ref[...] + u_ref[...] * skip_ref[...])).astype(BF16)
    y = jnp.dot(a, w_ref[...], preferred_element_type=F32) + b_ref[...]
    o_ref[...] = x_ref[...] + gt_ref[...] * y


def _gated_proj_res(a, y, u, skip, w_bf, bias, x, mod3, n_lat, gate_chunk):
    np_, d = x.shape
    kdim = a.shape[1]
    tn = 1024
    nlt = n_lat // TM
    rows = BS((TM, kdim), lambda i, j: (i, 0))
    return pl.pallas_call(
        _gated_proj_res_kernel,
        out_shape=SDS((np_, d), F32),
        grid=(np_ // TM, d // tn),
        in_specs=[rows, rows, rows,
                  BS((1, kdim), lambda i, j: (0, 0)),
                  BS((kdim, tn), lambda i, j: (0, j)),
                  BS((1, tn), lambda i, j: (0, j)),
                  BS((TM, tn), lambda i, j: (i, j)),
                  _mod_spec(nlt, d, tn, gate_chunk)],
        out_specs=BS((TM, tn), lambda i, j: (i, j)),
        compiler_params=_cp("parallel", "parallel"),
        name="gated_proj_residual",
    )(a, y, u, skip.reshape(1, kdim), w_bf, bias.reshape(1, d), x, mod3)


def _hyena_layer(x, mod3, norm_gain, w_in, b_in, w_short, b_short, fw1, fb1, fw2, fb2, fw3, fb3, sin_freq, decay,
                 skip, w_out, b_out, n_lat, n_ctx):
    np_, d = x.shape
    z = _linear_norm(x, norm_gain, mod3, w_in.astype(BF16), b_in, n_lat, BF16)
    x0, u = _short_conv(z, w_short, b_short, n_lat, n_ctx)
    tabs = _dft_tables(2 * n_lat)
    na = tabs["na"]
    g3 = _hyena_filter(n_lat, fw1, fb1, fw2, fb2, fw3, fb3, sin_freq, decay, True)
    u3 = u[:n_lat].reshape(na // 2, HY_NB, d)
    fa_bf = tabs["fa"].astype(BF16)
    ur, ui = _dft_a(u3, fa_bf[:, :na // 2], tabs["tw_re"], tabs["tw_im"])
    gr, gi = _dft_a(g3, fa_bf, tabs["tw_re"], tabs["tw_im"])
    br, bi = _dft_b(ur, ui, gr, gi, tabs["fwd_b"].astype(BF16), tabs["inv_b"].astype(BF16),
                    tabs["tw_re"], tabs["tw_im"])
    y_lat = _idft_a(br, bi, tabs["inv_a"].astype(BF16)).reshape(n_lat, d)
    g_ctx = _hyena_filter(n_ctx, fw1, fb1, fw2, fb2, fw3, fb3, sin_freq, decay, False)
    y_ctx = _ctx_conv(g_ctx, u[n_lat:n_lat + n_ctx])
    y = jnp.concatenate([y_lat, y_ctx, jnp.zeros((np_ - n_lat - n_ctx, d), F32)], axis=0)
    return _gated_proj_res(x0, y, u, skip, w_out.astype(BF16), b_out, x, mod3, n_lat, 2)


S5_T = 16
S5_GS = 16
S5_P = 64
S5_GB = 8


def _s5_discretize(lr_row, li_row, dt):
    mag = jnp.exp(lr_row * dt)
    a_re, a_im = mag * jnp.cos(li_row * dt), mag * jnp.sin(li_row * dt)
    xr, xi = a_re - 1.0, a_im
    den = 1.0 / (lr_row * lr_row + li_row * li_row)
    return (xr * lr_row + xi * li_row) * den, (xi * lr_row - xr * li_row) * den


def _s5_power_c(lr_col, li_col, dt, cre_t, cim_t, tau):
    width = S5_T * S5_GS
    lane = lax.broadcasted_iota(I32, (S5_GS, width), 1)
    expand = jnp.where(lane % S5_GS == lax.broadcasted_iota(I32, (S5_GS, width), 0), 1.0, 0.0)
    c_re = _dot_select(cre_t, expand)
    c_im = _dot_select(cim_t, expand)
    mag = jnp.exp(tau * (lr_col * dt))
    p_re, p_im = mag * jnp.cos(tau * (li_col * dt)), mag * jnp.sin(tau * (li_col * dt))
    z_re = p_re * c_re - p_im * c_im
    z_im = p_re * c_im + p_im * c_re
    return jnp.concatenate([z_re, -z_im], axis=0)


def _fill_group_perm(perm_scr, transposed):
    n = S5_GB * S5_T * S5_GS
    rows_per = 256
    for b in range(n // rows_per):
        r = b * rows_per + lax.broadcasted_iota(I32, (rows_per, n), 0)
        q = lax.broadcasted_iota(I32, (rows_per, n), 1)
        grouped, stream = (q, r) if transposed else (r, q)
        j = grouped // (S5_T * S5_GS)
        s = (grouped // S5_GS) % S5_T
        c = grouped % S5_GS
        hit = stream == s * (S5_GB * S5_GS) + j * S5_GS + c
        perm_scr[b * rows_per:(b + 1) * rows_per, :] = jnp.where(hit, 1.0, 0.0).astype(BF16)


def _s5_chunk_kernel(h_ref, lr_row, li_row, lr_col, li_col, ls_ref, bre_t, bim_t, cre_t, cim_t,
                     y_ref, s_ref, zc_ref, mt_scr, u_scr, perm_scr):
    dd = pl.program_id(1)
    width = S5_T * S5_GS

    @pl.when(jnp.logical_and(pl.program_id(0) == 0, dd == 0))
    def _():
        _fill_group_perm(perm_scr, True)

    @pl.when(dd == 0)
    def _():
        hcat = jnp.concatenate([h_ref[:, s, :].astype(BF16) for s in range(S5_T)], axis=1)
        u_scr[...] = jnp.dot(hcat, perm_scr[...], preferred_element_type=F32).astype(BF16)

    lane = lax.broadcasted_iota(I32, (S5_GS, width), 1)
    t_lane = lax.broadcasted_iota(I32, (1, width), 1) // S5_GS
    t_row = lax.broadcasted_iota(I32, (width, 1), 0) // S5_GS
    row = lax.broadcasted_iota(I32, (width, S5_GS), 0)
    expand_t = jnp.where(row % S5_GS == lax.broadcasted_iota(I32, (width, S5_GS), 1), 1.0, 0.0)
    tau_k = jnp.where(dd == 0, t_lane, S5_T - 1 - t_lane).astype(F32)
    tau_g = jnp.where(dd == 0, S5_T - 1 - t_row, t_row).astype(F32)
    for j in range(S5_GB):
        dt = jnp.exp(ls_ref[j])
        q_re, q_im = _s5_discretize(lr_row[j], li_row[j], dt)
        bb_re = q_re * bre_t[j] - q_im * bim_t[j]
        bb_im = q_re * bim_t[j] + q_im * bre_t[j]
        z = _s5_power_c(lr_col[j], li_col[j], dt, cre_t[j], cim_t[j], tau_k)
        k0 = _dot_f32(jnp.concatenate([bb_re, bb_im], axis=1), z)
        a_mag = jnp.exp(lr_col[j] * dt)
        a_re, a_im = a_mag * jnp.cos(li_col[j] * dt), a_mag * jnp.sin(li_col[j] * dt)
        z_re, z_nim = z[:S5_P], z[S5_P:]
        zc_ref[j] = jnp.concatenate([a_re * z_re + a_im * z_nim, a_re * z_nim - a_im * z_re], axis=0).astype(BF16)

        @pl.when(dd == 0)
        def _():
            for s in range(S5_T):
                blk = k0 if s == 0 else jnp.where(lane >= s * S5_GS, pltpu.roll(k0, s * S5_GS, axis=1), 0.0)
                mt_scr[s * S5_GS:(s + 1) * S5_GS, :] = blk.astype(BF16)

        @pl.when(dd == 1)
        def _():
            for s in range(S5_T):
                sh = (S5_T - 1 - s) * S5_GS
                blk = k0 if sh == 0 else jnp.where(lane < (s + 1) * S5_GS, pltpu.roll(k0, width - sh, axis=1), 0.0)
                mt_scr[s * S5_GS:(s + 1) * S5_GS, :] = blk.astype(BF16)

        u = u_scr[:, j * width:(j + 1) * width]
        y_ref[j] = jnp.dot(u, mt_scr[...], preferred_element_type=F32).astype(BF16)
        be_re = _dot_select(bb_re, expand_t, select_on_left=True)
        be_im = _dot_select(bb_im, expand_t, select_on_left=True)
        mag = jnp.exp(tau_g * (lr_row[j] * dt))
        p_re, p_im = mag * jnp.cos(tau_g * (li_row[j] * dt)), mag * jnp.sin(tau_g * (li_row[j] * dt))
        g = jnp.concatenate([p_re * be_re - p_im * be_im, p_re * be_im + p_im * be_re], axis=1)
        s_ref[:, j, :] = jnp.dot(u, g.astype(BF16), preferred_element_type=F32)


def _s5_chunks(h3, prm):
    nc, _, d = h3.shape
    groups, width = d // S5_GS, S5_T * S5_GS
    gl = S5_GB * S5_GS
    par = lambda r, c: BS((None, S5_GB, r, c), lambda i, dd: (dd, i, 0, 0))
    row, col, one = par(1, S5_P), par(S5_P, 1), par(1, 1)
    bt, ct = par(S5_GS, S5_P), par(S5_P, S5_GS)
    return pl.pallas_call(
        _s5_chunk_kernel,
        out_shape=(SDS((2, groups, nc, width), BF16), SDS((2, nc, groups, 2 * S5_P), F32),
                   SDS((2, groups, 2 * S5_P, width), BF16)),
        grid=(groups // S5_GB, 2),
        in_specs=[BS((nc, S5_T, gl), lambda i, dd: (0, 0, i)), row, row, col, col, one, bt, bt, ct, ct],
        out_specs=(BS((None, S5_GB, nc, width), lambda i, dd: (dd, i, 0, 0)),
                   BS((None, nc, S5_GB, 2 * S5_P), lambda i, dd: (dd, 0, i, 0)),
                   BS((None, S5_GB, 2 * S5_P, width), lambda i, dd: (dd, i, 0, 0))),
        scratch_shapes=[pltpu.VMEM((width, width), BF16), pltpu.VMEM((nc, S5_GB * width), BF16),
                        pltpu.VMEM((S5_GB * width, S5_GB * width), BF16)],
        compiler_params=_cp("arbitrary", "arbitrary"),
        name="s5_chunks",
    )(h3, prm["lr_row"], prm["li_row"], prm["lr_col"], prm["li_col"], prm["ls"], prm["bre_t"], prm["bim_t"],
      prm["cre_t"], prm["cim_t"])


def _s5_state_kernel(s0_ref, s1_ref, lr_ref, li_ref, ls_ref, h0_ref, h1_ref, st_scr, *, steps):
    @pl.when(pl.program_id(0) == 0)
    def _():
        st_scr[...] = jnp.zeros(st_scr.shape, F32)

    def coeffs(dd):
        dt = jnp.exp(ls_ref[dd])
        mag = jnp.exp(float(S5_T) * lr_ref[dd] * dt)
        ang = float(S5_T) * li_ref[dd] * dt
        a_re, a_im = mag * jnp.cos(ang), mag * jnp.sin(ang)
        lane = lax.broadcasted_iota(I32, a_im.shape, 1)
        a_sw = jnp.where(lane < S5_P, -a_im, a_im)
        return a_re, a_sw

    def run(dd, s_ref, h_ref, order):
        a_re, a_sw = coeffs(dd)
        h, hs = st_scr[2 * dd], st_scr[2 * dd + 1]
        for c in order:
            h_ref[c] = h
            s = s_ref[c]
            s_swapped = pltpu.roll(s, S5_P, axis=1)
            h, hs = a_re * h + a_sw * hs + s, a_re * hs - a_sw * h + s_swapped
        st_scr[2 * dd] = h
        st_scr[2 * dd + 1] = hs

    run(0, s0_ref, h0_ref, range(steps))
    run(1, s1_ref, h1_ref, reversed(range(steps)))


def _s5_states(s_t, prm, first_chunk):
    _, nc, groups, w = s_t.shape
    steps = 8
    nblk = nc // steps
    off = first_chunk // steps
    fwd = BS((None, steps, groups, w), lambda c: (0, (c + off) % nblk, 0, 0))
    bwd = BS((None, steps, groups, w), lambda c: (1, nblk - 1 - c, 0, 0))
    fwd_o = BS((steps, groups, w), lambda c: ((c + off) % nblk, 0, 0))
    bwd_o = BS((steps, groups, w), lambda c: (nblk - 1 - c, 0, 0))
    par = lambda last: BS((2, groups, last), lambda c: (0, 0, 0))
    return pl.pallas_call(
        functools.partial(_s5_state_kernel, steps=steps),
        out_shape=(SDS((nc, groups, w), F32), SDS((nc, groups, w), F32)),
        grid=(nblk,),
        in_specs=[fwd, bwd, par(w), par(w), par(1)],
        out_specs=(fwd_o, bwd_o),
        scratch_shapes=[pltpu.VMEM((4, groups, w), F32)],
        compiler_params=_cp("arbitrary"),
        name="s5_state_scan",
    )(s_t, s_t, prm["lr2"], prm["li2"], prm["ls_g"])


def _s5_output_kernel(y1_ref, h0_ref, h1_ref, zc_ref, y_ref, perm_scr):
    @pl.when(jnp.logical_and(pl.program_id(0) == 0, pl.program_id(1) == 0))
    def _():
        _fill_group_perm(perm_scr, False)

    parts = []
    for j in range(S5_GB):
        z = jnp.concatenate([zc_ref[0, j], zc_ref[1, j]], axis=0)
        h = jnp.concatenate([h0_ref[:, j, :], h1_ref[:, j, :]], axis=1).astype(BF16)
        yj = y1_ref[0, j].astype(F32) + y1_ref[1, j].astype(F32) + jnp.dot(h, z, preferred_element_type=F32)
        parts.append(yj.astype(BF16))
    ynat = jnp.dot(jnp.concatenate(parts, axis=1), perm_scr[...], preferred_element_type=F32)
    gl = S5_GB * S5_GS
    for s in range(S5_T):
        y_ref[:, s, :] = ynat[:, s * gl:(s + 1) * gl]


def _s5_outputs(y1, h0, h1, zc):
    _, groups, nc, width = y1.shape
    halves = 2
    nch = nc // halves
    gl = S5_GB * S5_GS
    states = BS((nch, S5_GB, 2 * S5_P), lambda i, c: (c, i, 0))
    return pl.pallas_call(
        _s5_output_kernel,
        out_shape=SDS((nc, S5_T, groups * S5_GS), F32),
        grid=(groups // S5_GB, halves),
        in_specs=[BS((2, S5_GB, nch, width), lambda i, c: (0, i, c, 0)), states, states,
                  BS((2, S5_GB, 2 * S5_P, width), lambda i, c: (0, i, 0, 0))],
        out_specs=BS((nch, S5_T, gl), lambda i, c: (c, 0, i)),
        scratch_shapes=[pltpu.VMEM((S5_GB * width, S5_GB * width), BF16)],
        compiler_params=_cp("arbitrary", "arbitrary"),
        name="s5_outputs",
    )(y1, h0, h1, zc)


def _s5_glu_kernel(x_ref, gain_ref, sc_ref, sh_ref, ys_ref, skip_ref, wv_ref, wg_ref, bv_ref, bg_ref,
                   xo_ref, gt_ref, o_ref, z_scr):
    @pl.when(pl.program_id(1) == 0)
    def _():
        h = _norm_mod(x_ref[...], gain_ref[...], sc_ref[...], sh_ref[...])
        z_scr[...] = jax.nn.gelu(skip_ref[...] * h + ys_ref[...]).astype(BF16)

    z = z_scr[...]
    val = jnp.dot(z, wv_ref[...], preferred_element_type=F32) + bv_ref[...]
    gate = jnp.dot(z, wg_ref[...], preferred_element_type=F32) + bg_ref[...]
    o_ref[...] = xo_ref[...] + gt_ref[...] * (val * (1.0 / (1.0 + jnp.exp(-gate))))


def _s5_glu(x, gain, mod3, ys, skip, w_bf, bias, n_lat):
    np_, d = x.shape
    tn = 512
    per = d // tn
    nlt = n_lat // TM
    b2 = bias.reshape(1, 2 * d)
    return pl.pallas_call(
        _s5_glu_kernel,
        out_shape=SDS((np_, d), F32),
        grid=(np_ // TM, per),
        in_specs=[BS((TM, d), lambda i, j: (i, 0)),
                  BS((1, d), lambda i, j: (0, 0)),
                  _mod_spec_full(nlt, d, 1),
                  _mod_spec_full(nlt, d, 0),
                  BS((TM, d), lambda i, j: (i, 0)),
                  BS((1, d), lambda i, j: (0, 0)),
                  BS((d, tn), lambda i, j: (0, j)),
                  BS((d, tn), lambda i, j: (0, per + j)),
                  BS((1, tn), lambda i, j: (0, j)),
                  BS((1, tn), lambda i, j: (0, per + j)),
                  BS((TM, tn), lambda i, j: (i, j)),
                  _mod_spec(nlt, d, tn, 2)],
        out_specs=BS((TM, tn), lambda i, j: (i, j)),
        scratch_shapes=[pltpu.VMEM((TM, d), BF16)],
        compiler_params=_cp("parallel", "arbitrary"),
        name="s5_glu",
    )(x, gain.reshape(1, d), mod3, mod3, ys, skip.reshape(1, d), w_bf, w_bf, b2, b2, x, mod3)


def _norm_mod_rows_kernel(x_ref, gain_ref, sc_ref, sh_ref, o_ref):
    o_ref[...] = _norm_mod(x_ref[...], gain_ref[...], sc_ref[...], sh_ref[...])


def _norm_mod_rows(x, gain, mod3, n_lat):
    np_, d = x.shape
    nlt = n_lat // TM
    return pl.pallas_call(
        _norm_mod_rows_kernel,
        out_shape=SDS((np_, d), F32),
        grid=(np_ // TM, 1),
        in_specs=[BS((TM, d), lambda i, j: (i, 0)),
                  BS((1, d), lambda i, j: (0, 0)),
                  _mod_spec_full(nlt, d, 1),
                  _mod_spec_full(nlt, d, 0)],
        out_specs=BS((TM, d), lambda i, j: (i, 0)),
        compiler_params=_cp("parallel", "arbitrary"),
        name="norm_mod_rows",
    )(x, gain.reshape(1, d), mod3, mod3)


def _s5_layer(x, mod3, norm_gain, lam_re, lam_im, log_step, b_re, b_im, c_re, c_im, skip, w_glu, b_glu,
              n_lat, n_ctx):
    np_, d = x.shape
    groups = d // S5_GS
    n_tok = n_lat + n_ctx
    nc = n_tok // S5_T
    h = _norm_mod_rows(x, norm_gain, mod3, n_lat)
    h3 = h[:n_tok].reshape(nc, S5_T, d)
    prm = dict(
        lr_row=lam_re.reshape(2, groups, 1, S5_P), li_row=lam_im.reshape(2, groups, 1, S5_P),
        lr_col=lam_re.reshape(2, groups, S5_P, 1), li_col=lam_im.reshape(2, groups, S5_P, 1),
        ls=log_step.reshape(2, groups, 1, 1), ls_g=log_step.reshape(2, groups, 1),
        bre_t=b_re.transpose(0, 1, 3, 2), bim_t=b_im.transpose(0, 1, 3, 2),
        cre_t=c_re.transpose(0, 1, 3, 2), cim_t=c_im.transpose(0, 1, 3, 2),
        lr2=jnp.concatenate([lam_re, lam_re], axis=-1), li2=jnp.concatenate([lam_im, lam_im], axis=-1))
    y1, s, zc = _s5_chunks(h3, prm)
    h0, h1 = _s5_states(s, prm, n_lat // S5_T)
    y = _s5_outputs(y1, h0, h1, zc).reshape(n_tok, d)
    ys = jnp.concatenate([y, jnp.zeros((np_ - n_tok, d), F32)], axis=0)
    return _s5_glu(x, norm_gain, mod3, ys, skip, w_glu.astype(BF16), b_glu, n_lat)


def kernel(x, c, ctx, c_ctx, w_mod, b_mod, norm_mix, norm_ffn, router_w, router_b, w_gate_up, b_gate_up, w_down, b_down, win_w_qkv, win_w_o, win_q_gain, win_k_gain, win_sinks, full_w_qkv, full_w_o, full_q_gain, full_k_gain, hy_w_in, hy_b_in, hy_w_short, hy_b_short, hy_filt_w1, hy_filt_b1, hy_filt_w2, hy_filt_b2, hy_filt_w3, hy_filt_b3, hy_sin_freq, hy_decay, hy_skip, hy_w_out, hy_b_out, s5_lam_re, s5_lam_im, s5_log_step, s5_b_re, s5_b_im, s5_c_re, s5_c_im, s5_skip, s5_w_glu, s5_b_glu):
    batch, n_lat, d = x.shape
    n_ctx = ctx.shape[1]
    assert batch == 1 and n_lat % TM == 0 and n_ctx % TOK_DMA == 0 and (n_lat + n_ctx) % S5_T == 0
    depth = w_mod.shape[0]
    n_tok = n_lat + n_ctx
    pad = (-n_ctx) % TM
    cc = jnp.zeros((8, d), F32).at[0].set(c[0]).at[1].set(c_ctx)
    mod = _modulation(cc, w_mod, b_mod)
    xs = jnp.concatenate([x[0], ctx[0], jnp.zeros((pad, d), F32)], axis=0)
    cos2, sin2 = _rope_tables(n_lat)
    for i in range(depth):
        kind, j = i % 4, i // 4
        mod3 = mod[i, :2].reshape(2, 1, 6 * d)
        if kind == 0:
            xs = _attn_layer(xs, mod3, norm_mix[i], win_w_qkv[j], win_w_o[j], win_q_gain[j], win_k_gain[j],
                             win_sinks[j], cos2, sin2, n_lat, n_ctx, True)
        elif kind == 1:
            xs = _attn_layer(xs, mod3, norm_mix[i], full_w_qkv[j], full_w_o[j], full_q_gain[j], full_k_gain[j],
                             None, cos2, sin2, n_lat, n_ctx, False)
        elif kind == 2:
            xs = _hyena_layer(xs, mod3, norm_mix[i], hy_w_in[j], hy_b_in[j], hy_w_short[j], hy_b_short[j],
                              hy_filt_w1[j], hy_filt_b1[j], hy_filt_w2[j], hy_filt_b2[j], hy_filt_w3[j],
                              hy_filt_b3[j], hy_sin_freq[j], hy_decay[j], hy_skip[j], hy_w_out[j], hy_b_out[j],
                              n_lat, n_ctx)
        else:
            xs = _s5_layer(xs, mod3, norm_mix[i], s5_lam_re[j], s5_lam_im[j], s5_log_step[j], s5_b_re[j],
                           s5_b_im[j], s5_c_re[j], s5_c_im[j], s5_skip[j], s5_w_glu[j], s5_b_glu[j], n_lat, n_ctx)
        xs = _moe(xs, norm_ffn[i], mod3, router_w[i], router_b[i], i, w_gate_up, b_gate_up[i],
                  w_down, b_down[i], n_lat, n_tok)
    return xs[:n_lat].reshape(x.shape)
```

```python
import functools
import math

import numpy as np
import jax
import jax.numpy as jnp
from jax import lax
from jax.experimental import pallas as pl
from jax.experimental.pallas import tpu as pltpu

F32, BF16, I32 = jnp.float32, jnp.bfloat16, jnp.int32
SDS = jax.ShapeDtypeStruct
BS = pl.BlockSpec

HEAD_DIM = 128
N_KV_HEADS = 4
GROUP = 4
GRID_W = 64
WINDOW = 128
ROPE_BASE = 10000.0
TOP_K = 4
SWIGLU_LIMIT = 7.0
SWIGLU_ALPHA = 1.702
EPS = 1e-6
NEG_INF = -1e30

LANES = 128
TM = 512
TQ = 256
TK = 1024
ATT_RB = 128
ATT_UNROLL = 8
LOG2E = math.log2(math.e)
MOE_BLK = 256
TOK_DMA = 256
VMEM_LIMIT = 56 * 1024 * 1024


def _cp(*sem):
    return pltpu.CompilerParams(dimension_semantics=sem, vmem_limit_bytes=VMEM_LIMIT)


def _norm_mod(x, gain, scale, shift):
    ms = jnp.mean(x * x, axis=-1, keepdims=True)
    return (x * lax.rsqrt(ms + EPS)) * gain * (1.0 + scale) + shift


def _split3(a):
    a0 = a.astype(BF16)
    r1 = a - a0.astype(F32)
    a1 = r1.astype(BF16)
    a2 = (r1 - a1.astype(F32)).astype(BF16)
    return a0, a1, a2


def _dot_f32(a, b):
    a0, a1, a2 = _split3(a)
    b0, b1, b2 = _split3(b)
    d = lambda u, v: jnp.dot(u, v, preferred_element_type=F32)
    return (d(a0, b0) + (d(a0, b1) + d(a1, b0)) + (d(a0, b2) + d(a1, b1) + d(a2, b0)))


def _dot_select(a, e, select_on_left=False):
    a0, a1, a2 = _split3(a)
    e = e.astype(BF16)
    d = (lambda u: jnp.dot(e, u, preferred_element_type=F32)) if select_on_left else (
        lambda u: jnp.dot(u, e, preferred_element_type=F32))
    return d(a0) + (d(a1) + d(a2))


def _mod_kernel(c_ref, w_ref, b_ref, o_ref):
    cc = c_ref[...]
    s = cc * (1.0 / (1.0 + jnp.exp(-cc)))
    o_ref[...] = jnp.dot(s.astype(BF16), w_ref[...].astype(BF16), preferred_element_type=F32) + b_ref[...]


def _modulation(cc, w_mod, b_mod):
    depth, d, d6 = w_mod.shape
    tn = 1024
    return pl.pallas_call(
        _mod_kernel,
        out_shape=SDS((depth, 8, d6), F32),
        grid=(depth, d6 // tn),
        in_specs=[BS((8, d), lambda l, j: (0, 0)),
                  BS((None, d, tn), lambda l, j: (l, 0, j)),
                  BS((None, 1, tn), lambda l, j: (l, 0, j))],
        out_specs=BS((None, 8, tn), lambda l, j: (l, 0, j)),
        compiler_params=_cp("parallel", "parallel"),
        name="modulation",
    )(cc, w_mod, b_mod.reshape(depth, 1, d6))


def _mod_spec(n_lat_tiles, d, tn, chunk):
    per = d // tn
    return BS((None, 1, tn), lambda i, j: (jnp.where(i < n_lat_tiles, 0, 1), 0, chunk * per + j))


def _mod_spec_full(n_lat_tiles, d, chunk):
    return BS((None, 1, d), lambda i, j: (jnp.where(i < n_lat_tiles, 0, 1), 0, chunk))


def _qkv_kernel(x_ref, gain_ref, sc_ref, sh_ref, w_ref, qg_ref, kg_ref, cos_ref, sin_ref, o_ref, h_scr,
                *, n_lat_tiles):
    i = pl.program_id(0)
    j = pl.program_id(1)

    @pl.when(j == 0)
    def _():
        h_scr[...] = _norm_mod(x_ref[...], gain_ref[...], sc_ref[...], sh_ref[...]).astype(BF16)

    acc = jnp.dot(h_scr[...], w_ref[...], preferred_element_type=F32)

    @pl.when(j < 5)
    def _():
        gain = jnp.where(j < 4, qg_ref[...], kg_ref[...])
        post = jnp.where(j < 4, HEAD_DIM ** -0.5 * LOG2E, 1.0)
        is_lat = i < n_lat_tiles
        for hh in range(4):
            t = acc[:, hh * HEAD_DIM:(hh + 1) * HEAD_DIM]
            t = t * lax.rsqrt(jnp.mean(t * t, axis=-1, keepdims=True) + EPS) * gain
            rot = pltpu.roll(t, HEAD_DIM // 2, axis=1)
            tr = t * cos_ref[...] + rot * sin_ref[...]
            t = jnp.where(is_lat, tr, t) * post
            o_ref[:, hh * HEAD_DIM:(hh + 1) * HEAD_DIM] = t.astype(BF16)

    @pl.when(j == 5)
    def _():
        o_ref[...] = acc.astype(BF16)


def _qkv(x, gain, mod3, w_bf, q_gain, k_gain, cos2, sin2, n_lat):
    np_, d = x.shape
    nt, nlt = np_ // TM, n_lat // TM
    width = w_bf.shape[1]
    tn = 4 * HEAD_DIM
    return pl.pallas_call(
        functools.partial(_qkv_kernel, n_lat_tiles=nlt),
        out_shape=SDS((np_, width), BF16),
        grid=(nt, width // tn),
        in_specs=[BS((TM, d), lambda i, j: (i, 0)),
                  BS((1, d), lambda i, j: (0, 0)),
                  _mod_spec_full(nlt, d, 1),
                  _mod_spec_full(nlt, d, 0),
                  BS((d, tn), lambda i, j: (0, j)),
                  BS((1, HEAD_DIM), lambda i, j: (0, 0)),
                  BS((1, HEAD_DIM), lambda i, j: (0, 0)),
                  BS((TM, HEAD_DIM), lambda i, j: (jnp.minimum(i, nlt - 1), 0)),
                  BS((TM, HEAD_DIM), lambda i, j: (jnp.minimum(i, nlt - 1), 0))],
        out_specs=BS((TM, tn), lambda i, j: (i, j)),
        scratch_shapes=[pltpu.VMEM((TM, d), BF16)],
        compiler_params=_cp("parallel", "arbitrary"),
        name="qkv_proj",
    )(x, gain.reshape(1, d), mod3, mod3, w_bf, q_gain.reshape(1, HEAD_DIM), k_gain.reshape(1, HEAD_DIM),
      cos2, sin2)


def _rope_tables(n_lat):
    pairs = HEAD_DIM // 4
    t = jnp.arange(n_lat, dtype=I32)
    row = (t // GRID_W).astype(F32)
    col = (t % GRID_W).astype(F32)
    freqs = ROPE_BASE ** (-jnp.arange(pairs, dtype=F32) / pairs)
    ang = jnp.concatenate([row[:, None] * freqs, col[:, None] * freqs], axis=-1)
    cos, sin = jnp.cos(ang), jnp.sin(ang)
    return jnp.concatenate([cos, cos], axis=-1), jnp.concatenate([-sin, sin], axis=-1)


def _attn_kernel(sink_ref, q_ref, k_ref, v_ref, o_ref, q_scr, m_scr, acc_scr,
                 *, windowed, has_sink, n_lat, n_ctx, n_ctx_pad):
    kvh = pl.program_id(0)
    i = pl.program_id(1)
    is_lat = i < n_lat // TQ
    rows = GROUP * TQ
    for h in range(GROUP):
        q_scr[h * TQ:(h + 1) * TQ, :] = q_ref[:, h * HEAD_DIM:(h + 1) * HEAD_DIM]
    m_scr[...] = jnp.full((rows, LANES), NEG_INF, F32)
    acc_scr[...] = jnp.zeros((rows, 2 * HEAD_DIM), F32)

    def step(start, size, mask_fn):
        k = k_ref[pl.ds(start, size), :]
        v = jnp.concatenate([v_ref[pl.ds(start, size), :], jnp.ones((size, HEAD_DIM), BF16)], axis=1)
        for r in range(rows // ATT_RB):
            sl = slice(r * ATT_RB, (r + 1) * ATT_RB)
            s = lax.dot_general(q_scr[sl, :], k, (((1,), (1,)), ((), ())), preferred_element_type=F32)
            if mask_fn is not None:
                s = jnp.where(mask_fn(r), s, NEG_INF)
            m_old = m_scr[sl, :]
            m_new = jnp.maximum(m_old, jnp.max(s, axis=-1, keepdims=True))
            alpha = jnp.exp2(m_old - m_new)
            p = jnp.exp2(s - jnp.concatenate([m_new] * (size // LANES), axis=1))
            acc_scr[sl, :] = (jnp.concatenate([alpha, alpha], axis=1) * acc_scr[sl, :]
                              + jnp.dot(p.astype(BF16), v, preferred_element_type=F32))
            m_scr[sl, :] = m_new

    if windowed:
        wk = TQ + 2 * WINDOW

        @pl.when(is_lat)
        def _():
            start = pl.multiple_of(jnp.clip(i * TQ - WINDOW, 0, n_lat - wk), WINDOW)

            def band(r):
                q_pos = i * TQ + (r * ATT_RB) % TQ + lax.broadcasted_iota(I32, (ATT_RB, wk), 0)
                k_pos = start + lax.broadcasted_iota(I32, (ATT_RB, wk), 1)
                return jnp.abs(q_pos - k_pos) <= WINDOW

            step(start, wk, band)
    else:
        def body(c, carry):
            for half in range(ATT_UNROLL):
                step(pl.multiple_of((c * ATT_UNROLL + half) * TK, TK), TK, None)
            return carry

        trips = n_lat // (TK * ATT_UNROLL)
        lax.fori_loop(0, jnp.where(is_lat, trips, 0), body, 0)
        if n_lat // TK > trips * ATT_UNROLL:
            @pl.when(is_lat)
            def _():
                for c in range(trips * ATT_UNROLL, n_lat // TK):
                    step(c * TK, TK, None)

    ctx_mask = None
    if n_ctx != n_ctx_pad:
        ctx_mask = lambda r: lax.broadcasted_iota(I32, (ATT_RB, n_ctx_pad), 1) < n_ctx
    step(n_lat, n_ctx_pad, ctx_mask)

    for h in range(GROUP):
        sl = slice(h * TQ, (h + 1) * TQ)
        acc = acc_scr[sl, :]
        num, den = acc[:, :HEAD_DIM], acc[:, HEAD_DIM:]
        if has_sink:
            m = m_scr[sl, :]
            sink = sink_ref[kvh * GROUP + h] * LOG2E
            m_f = jnp.maximum(m, sink)
            scale = jnp.exp2(m - m_f)
            num = num * scale
            den = den * scale + jnp.exp2(sink - m_f)
        o_ref[:, h * HEAD_DIM:(h + 1) * HEAD_DIM] = (num * (1.0 / den)).astype(BF16)


def _attention(qkv, sinks, n_lat, n_ctx, windowed):
    np_ = qkv.shape[0]
    n_heads = N_KV_HEADS * GROUP
    has_sink = sinks is not None
    if sinks is None:
        sinks = jnp.zeros((n_heads,), F32)
    rows = GROUP * TQ
    kern = functools.partial(_attn_kernel, windowed=windowed, has_sink=has_sink, n_lat=n_lat, n_ctx=n_ctx,
                             n_ctx_pad=np_ - n_lat)
    return pl.pallas_call(
        kern,
        out_shape=SDS((np_, n_heads * HEAD_DIM), BF16),
        grid_spec=pltpu.PrefetchScalarGridSpec(
            num_scalar_prefetch=1,
            grid=(N_KV_HEADS, np_ // TQ),
            in_specs=[BS((TQ, GROUP * HEAD_DIM), lambda kv, i, s: (i, kv)),
                      BS((np_, HEAD_DIM), lambda kv, i, s: (0, n_heads + kv)),
                      BS((np_, HEAD_DIM), lambda kv, i, s: (0, n_heads + N_KV_HEADS + kv))],
            out_specs=BS((TQ, GROUP * HEAD_DIM), lambda kv, i, s: (i, kv)),
            scratch_shapes=[pltpu.VMEM((rows, HEAD_DIM), BF16), pltpu.VMEM((rows, LANES), F32),
                            pltpu.VMEM((rows, 2 * HEAD_DIM), F32)]),
        compiler_params=_cp("parallel", "arbitrary"),
        name="window_attention" if windowed else "full_attention",
    )(sinks.astype(F32), qkv, qkv, qkv)


def _proj_res_kernel(a_ref, w_ref, b_ref, x_ref, gt_ref, o_ref):
    y = jnp.dot(a_ref[...], w_ref[...], preferred_element_type=F32) + b_ref[...]
    o_ref[...] = x_ref[...] + gt_ref[...] * y


def _proj_res(a, w_bf, bias, x, mod3, n_lat, gate_chunk):
    np_, d = x.shape
    kdim = a.shape[1]
    tn = 1024
    nlt = n_lat // TM
    return pl.pallas_call(
        _proj_res_kernel,
        out_shape=SDS((np_, d), F32),
        grid=(np_ // TM, d // tn),
        in_specs=[BS((TM, kdim), lambda i, j: (i, 0)),
                  BS((kdim, tn), lambda i, j: (0, j)),
                  BS((1, tn), lambda i, j: (0, j)),
                  BS((TM, tn), lambda i, j: (i, j)),
                  _mod_spec(nlt, d, tn, gate_chunk)],
        out_specs=BS((TM, tn), lambda i, j: (i, j)),
        compiler_params=_cp("parallel", "parallel"),
        name="proj_residual",
    )(a, w_bf, bias.reshape(1, d), x, mod3)


def _router_kernel(x_ref, gain_ref, sc_ref, sh_ref, wr_ref, br_ref, h_ref, idx_ref, gate_ref):
    h = _norm_mod(x_ref[...], gain_ref[...], sc_ref[...], sh_ref[...])
    h_ref[...] = h
    vals = _dot_f32(h, wr_ref[...]) + br_ref[...]
    lane = lax.broadcasted_iota(I32, vals.shape, 1).astype(F32)
    tops, idxs = [], []
    for _ in range(TOP_K):
        m = jnp.max(vals, axis=-1, keepdims=True)
        sel = jnp.min(jnp.where(vals == m, lane, float(LANES)), axis=-1, keepdims=True)
        tops.append(m)
        idxs.append(sel)
        vals = jnp.where(lane == sel, 2.0 * NEG_INF, vals)
    es = [jnp.exp(t - tops[0]) for t in tops]
    inv = 1.0 / (es[0] + es[1] + es[2] + es[3])
    idx_out = jnp.zeros(vals.shape, F32)
    gate_out = jnp.zeros(vals.shape, F32)
    for k in range(TOP_K):
        idx_out = jnp.where(lane == float(k), idxs[k], idx_out)
        gate_out = jnp.where(lane == float(k), es[k] * inv, gate_out)
    idx_ref[...] = idx_out.astype(I32)
    gate_ref[...] = gate_out


def _router(x, gain, mod3, wr_pad, br_pad, n_lat):
    np_, d = x.shape
    nlt = n_lat // TM
    return pl.pallas_call(
        _router_kernel,
        out_shape=(SDS((np_, d), F32), SDS((np_, LANES), I32), SDS((np_, LANES), F32)),
        grid=(np_ // TM, 1),
        in_specs=[BS((TM, d), lambda i, j: (i, 0)),
                  BS((1, d), lambda i, j: (0, 0)),
                  _mod_spec_full(nlt, d, 4),
                  _mod_spec_full(nlt, d, 3),
                  BS((d, LANES), lambda i, j: (0, 0)),
                  BS((1, LANES), lambda i, j: (0, 0))],
        out_specs=(BS((TM, d), lambda i, j: (i, 0)),
                   BS((TM, LANES), lambda i, j: (i, 0)),
                   BS((TM, LANES), lambda i, j: (i, 0))),
        compiler_params=_cp("parallel", "arbitrary"),
        name="router_top4",
    )(x, gain.reshape(1, d), mod3, mod3, wr_pad, br_pad)


def _rank_kernel(idx_ref, rank_ref, cnt_ref, base_scr, *, n_tok):
    i = pl.program_id(0)

    @pl.when(i == 0)
    def _():
        base_scr[...] = jnp.zeros(base_scr.shape, F32)

    idx = idx_ref[...]
    lane = lax.broadcasted_iota(I32, idx.shape, 1)
    row = i * TM + lax.broadcasted_iota(I32, idx.shape, 0)
    valid = row < n_tok
    hots = [jnp.where(valid & (lane == idx[:, k:k + 1]), 1.0, 0.0) for k in range(TOP_K)]
    cnt = hots[0] + hots[1] + hots[2] + hots[3]
    tri = jnp.where(lax.broadcasted_iota(I32, (TM, TM), 0) > lax.broadcasted_iota(I32, (TM, TM), 1), 1.0, 0.0)
    before = base_scr[...] + jnp.dot(tri.astype(BF16), cnt.astype(BF16), preferred_element_type=F32)
    out = jnp.zeros(idx.shape, F32)
    for k in range(TOP_K):
        out = jnp.where(lane == k, jnp.sum(hots[k] * before, axis=-1, keepdims=True), out)
    rank_ref[...] = out.astype(I32)
    base_scr[...] = base_scr[...] + jnp.sum(cnt, axis=0, keepdims=True)
    cnt_ref[...] = base_scr[...].astype(I32)


def _expert_ranks(idx, n_tok):
    np_ = idx.shape[0]
    return pl.pallas_call(
        functools.partial(_rank_kernel, n_tok=n_tok),
        out_shape=(SDS((np_, LANES), I32), SDS((1, LANES), I32)),
        grid=(np_ // TM,),
        in_specs=[BS((TM, LANES), lambda i: (i, 0))],
        out_specs=(BS((TM, LANES), lambda i: (i, 0)), BS((1, LANES), lambda i: (0, 0))),
        scratch_shapes=[pltpu.VMEM((1, LANES), F32)],
        compiler_params=_cp("arbitrary"),
        name="expert_ranks",
    )(idx)


def _row_copy(src, dst, sem):
    return pltpu.make_async_copy(src, dst, sem)


def _dispatch_kernel(dest_ref, h_ref, xs_init, xs_hbm, sem):
    del xs_init

    def issue(t, carry):
        src = h_ref.at[pl.ds(t, 1)]
        for k in range(TOP_K):
            _row_copy(src, xs_hbm.at[pl.ds(dest_ref[t * TOP_K + k], 1)], sem).start(priority=k % 2)
        return carry

    lax.fori_loop(0, TOK_DMA, issue, 0)

    def drain(t, carry):
        for k in range(TOP_K):
            _row_copy(h_ref.at[pl.ds(0, 1)], xs_hbm.at[pl.ds(0, 1)], sem).wait()
        return carry

    lax.fori_loop(0, TOK_DMA, drain, 0)


def _dispatch(dest_flat, h, n_tok, n_rows):
    d = h.shape[1]
    xs0 = jnp.zeros((n_rows, d), F32)
    return pl.pallas_call(
        _dispatch_kernel,
        out_shape=SDS((n_rows, d), F32),
        grid=(n_tok // TOK_DMA,),
        in_specs=[BS((TOK_DMA * TOP_K,), lambda i: (i,), memory_space=pltpu.SMEM),
                  BS((TOK_DMA, d), lambda i: (i, 0)),
                  BS(memory_space=pl.ANY)],
        out_specs=BS(memory_space=pl.ANY),
        scratch_shapes=[pltpu.SemaphoreType.DMA(())],
        input_output_aliases={2: 0},
        compiler_params=pltpu.CompilerParams(dimension_semantics=("arbitrary",), has_side_effects=True),
        name="moe_dispatch",
    )(dest_flat, h, xs0)


def _ffn_kernel(be_ref, nu_ref, x_ref, wgu_ref, bgu_ref, wdn_ref, bdn_ref, o_ref, *, ff):
    b = pl.program_id(0)

    @pl.when(b < nu_ref[0])
    def _():
        gu = jnp.dot(x_ref[...].astype(BF16), wgu_ref[...].astype(BF16), preferred_element_type=F32) + bgu_ref[...]
        glu = jnp.minimum(gu[:, :ff], SWIGLU_LIMIT)
        lin = jnp.clip(gu[:, ff:], -SWIGLU_LIMIT, SWIGLU_LIMIT)
        act = glu * (1.0 / (1.0 + jnp.exp(-SWIGLU_ALPHA * glu))) * (lin + 1.0)
        o_ref[...] = jnp.dot(act.astype(BF16), wdn_ref[...].astype(BF16), preferred_element_type=F32) + bdn_ref[...]

    @pl.when(b >= nu_ref[0])
    def _():
        o_ref[...] = jnp.zeros(o_ref.shape, F32)


def _expert_ffn(blk_expert, n_used, xs, layer, w_gu, bgu, w_dn, bdn):
    n_rows, d = xs.shape
    _, n_exp, _, ff2 = w_gu.shape
    ff = ff2 // 2
    return pl.pallas_call(
        functools.partial(_ffn_kernel, ff=ff),
        out_shape=SDS((n_rows, d), F32),
        grid_spec=pltpu.PrefetchScalarGridSpec(
            num_scalar_prefetch=2,
            grid=(n_rows // MOE_BLK,),
            in_specs=[BS((MOE_BLK, d), lambda b, be, nu: (b, 0)),
                      BS((None, None, d, ff2), lambda b, be, nu: (layer, be[b], 0, 0)),
                      BS((None, 1, ff2), lambda b, be, nu: (be[b], 0, 0)),
                      BS((None, None, ff, d), lambda b, be, nu: (layer, be[b], 0, 0)),
                      BS((None, 1, d), lambda b, be, nu: (be[b], 0, 0))],
            out_specs=BS((MOE_BLK, d), lambda b, be, nu: (b, 0))),
        compiler_params=_cp("arbitrary"),
        name="expert_ffn",
    )(blk_expert, n_used, xs, w_gu, bgu.reshape(n_exp, 1, ff2), w_dn, bdn.reshape(n_exp, 1, d))


def _combine_kernel(dest_ref, next_ref, g_ref, x_ref, gt_ref, y_hbm, o_ref, buf, sems):
    i = pl.program_id(0)
    slot = i % 2

    def issue_all(idx_ref, s):
        def issue(t, carry):
            for k in range(TOP_K):
                _row_copy(y_hbm.at[pl.ds(idx_ref[t * TOP_K + k], 1)], buf.at[s, k, pl.ds(t, 1)],
                          sems.at[s]).start()
            return carry

        lax.fori_loop(0, TOK_DMA, issue, 0)

    @pl.when(i == 0)
    def _():
        issue_all(dest_ref, 0)

    @pl.when(i + 1 < pl.num_programs(0))
    def _():
        issue_all(next_ref, 1 - slot)

    def drain(t, carry):
        for k in range(TOP_K):
            _row_copy(y_hbm.at[pl.ds(0, 1)], buf.at[slot, k, pl.ds(0, 1)], sems.at[slot]).wait()
        return carry

    lax.fori_loop(0, TOK_DMA, drain, 0)
    g = g_ref[...]
    acc = g[:, 0:1] * buf[slot, 0]
    for k in range(1, TOP_K):
        acc = acc + g[:, k:k + 1] * buf[slot, k]
    o_ref[...] = x_ref[...] + gt_ref[...] * acc


def _combine(dest_flat, gates, x, mod3, y, n_lat):
    np_, d = x.shape
    nlt = n_lat // TOK_DMA
    steps = np_ // TOK_DMA
    return pl.pallas_call(
        _combine_kernel,
        out_shape=SDS((np_, d), F32),
        grid=(steps,),
        in_specs=[BS((TOK_DMA * TOP_K,), lambda i: (i,), memory_space=pltpu.SMEM),
                  BS((TOK_DMA * TOP_K,), lambda i: (jnp.minimum(i + 1, steps - 1),), memory_space=pltpu.SMEM),
                  BS((TOK_DMA, LANES), lambda i: (i, 0)),
                  BS((TOK_DMA, d), lambda i: (i, 0)),
                  BS((None, 1, d), lambda i: (jnp.where(i < nlt, 0, 1), 0, 5)),
                  BS(memory_space=pl.ANY)],
        out_specs=BS((TOK_DMA, d), lambda i: (i, 0)),
        scratch_shapes=[pltpu.VMEM((2, TOP_K, TOK_DMA, d), F32), pltpu.SemaphoreType.DMA((2,))],
        compiler_params=_cp("arbitrary"),
        name="moe_combine",
    )(dest_flat, dest_flat, gates, x, mod3, y)


def _moe(x, gain, mod3, router_w, router_b, layer, w_gu, bgu, w_dn, bdn, n_lat, n_tok):
    np_, d = x.shape
    n_exp = router_w.shape[1]
    wr_pad = jnp.zeros((d, LANES), F32).at[:, :n_exp].set(router_w)
    br_pad = jnp.full((1, LANES), NEG_INF, F32).at[0, :n_exp].set(router_b)
    h, idx, gates = _router(x, gain, mod3, wr_pad, br_pad, n_lat)
    rank, counts = _expert_ranks(idx, n_tok)
    counts = counts[0, :n_exp]
    padded = (counts + MOE_BLK - 1) // MOE_BLK * MOE_BLK
    pad_end = jnp.cumsum(padded)
    pad_start = pad_end - padded
    dest = pad_start[idx[:, :TOP_K]] + rank[:, :TOP_K]
    dest = jnp.where(jnp.arange(np_)[:, None] < n_tok, dest, 0).astype(I32).reshape(-1)
    n_blocks = -(-(n_tok * TOP_K + n_exp * (MOE_BLK - 1)) // MOE_BLK)
    blk_row = jnp.arange(n_blocks, dtype=I32) * MOE_BLK
    blk_expert = jnp.minimum(jnp.sum(pad_end[None, :] <= blk_row[:, None], axis=1), n_exp - 1).astype(I32)
    n_used = (pad_end[-1:] // MOE_BLK).astype(I32)
    xs = _dispatch(dest, h, n_tok, n_blocks * MOE_BLK)
    y = _expert_ffn(blk_expert, n_used, xs, layer, w_gu, bgu, w_dn, bdn)
    return _combine(dest, gates, x, mod3, y, n_lat)


def _attn_layer(x, mod3, norm_gain, w_qkv, w_o, q_gain, k_gain, sinks, cos2, sin2, n_lat, n_ctx, windowed):
    d = x.shape[1]
    qkv = _qkv(x, norm_gain, mod3, w_qkv.astype(BF16), q_gain, k_gain, cos2, sin2, n_lat)
    o = _attention(qkv, sinks, n_lat, n_ctx, windowed)
    return _proj_res(o, w_o.astype(BF16), jnp.zeros((d,), F32), x, mod3, n_lat, 2)


def _linear_norm_kernel(x_ref, gain_ref, sc_ref, sh_ref, w_ref, b_ref, o_ref, h_scr):
    @pl.when(pl.program_id(1) == 0)
    def _():
        h_scr[...] = _norm_mod(x_ref[...], gain_ref[...], sc_ref[...], sh_ref[...]).astype(BF16)

    o_ref[...] = (jnp.dot(h_scr[...], w_ref[...], preferred_element_type=F32) + b_ref[...]).astype(o_ref.dtype)


def _linear_norm(x, gain, mod3, w_bf, bias, n_lat, out_dtype):
    np_, d = x.shape
    width = w_bf.shape[1]
    tn = 1024
    nlt = n_lat // TM
    return pl.pallas_call(
        _linear_norm_kernel,
        out_shape=SDS((np_, width), out_dtype),
        grid=(np_ // TM, width // tn),
        in_specs=[BS((TM, d), lambda i, j: (i, 0)),
                  BS((1, d), lambda i, j: (0, 0)),
                  _mod_spec_full(nlt, d, 1),
                  _mod_spec_full(nlt, d, 0),
                  BS((d, tn), lambda i, j: (0, j)),
                  BS((1, tn), lambda i, j: (0, j))],
        out_specs=BS((TM, tn), lambda i, j: (i, j)),
        scratch_shapes=[pltpu.VMEM((TM, d), BF16)],
        compiler_params=_cp("parallel", "arbitrary"),
        name="linear_norm",
    )(x, gain.reshape(1, d), mod3, mod3, w_bf, bias.reshape(1, width))


HY_NB = 128
HY_SUB = 8
HY_BANDS = 8
HY_MOD_SHIFT = 0.05
HALO = 16


def _short_conv_kernel(z0c, z0p, z0n, z1c, z1p, z1n, z2c, z2p, z2n, w_ref, b_ref, x0_ref, u_ref,
                       *, n_lat_tiles, n_ctx):
    i = pl.program_id(0)
    row = lax.broadcasted_iota(I32, (TM, 1), 0)
    is_ctx = i >= n_lat_tiles
    has_prev = jnp.logical_and(i > 0, i != n_lat_tiles)
    has_next = i < n_lat_tiles - 1
    keep = jnp.logical_or(jnp.logical_not(is_ctx), row < n_ctx)

    def conv(cur, prv, nxt, part):
        zc = jnp.where(keep, cur[...].astype(F32), 0.0)
        first = jnp.where(has_prev, prv[HALO - 1:HALO, :].astype(F32), 0.0)
        last = jnp.where(has_next, nxt[0:1, :].astype(F32), 0.0)
        up = jnp.where(row == 0, first, pltpu.roll(zc, 1, axis=0))
        dn = jnp.where(row == TM - 1, last, pltpu.roll(zc, TM - 1, axis=0))
        w = w_ref[part]
        return up * w[0:1, :] + zc * w[1:2, :] + dn * w[2:3, :] + b_ref[part]

    x0 = conv(z0c, z0p, z0n, 0)
    x1 = conv(z1c, z1p, z1n, 1)
    v = conv(z2c, z2p, z2n, 2)
    x0_ref[...] = x0.astype(BF16)
    u_ref[...] = x1 * v


def _short_conv(z, w_short, b_short, n_lat, n_ctx):
    np_, d3 = z.shape
    d = d3 // 3
    tn = 512
    per = d // tn
    nlt = n_lat // TM
    hb = TM // HALO
    last_halo = np_ // HALO - 1
    specs = []
    for part in range(3):
        specs.append(BS((TM, tn), lambda i, j, part=part: (i, part * per + j)))
        specs.append(BS((HALO, tn), lambda i, j, part=part: (jnp.maximum(i * hb - 1, 0), part * per + j)))
        specs.append(BS((HALO, tn), lambda i, j, part=part: (jnp.minimum((i + 1) * hb, last_halo), part * per + j)))
    w3 = w_short.reshape(3, 3, d).transpose(1, 0, 2)
    b3 = b_short.reshape(3, 1, d)
    return pl.pallas_call(
        functools.partial(_short_conv_kernel, n_lat_tiles=nlt, n_ctx=n_ctx),
        out_shape=(SDS((np_, d), BF16), SDS((np_, d), F32)),
        grid=(np_ // TM, per),
        in_specs=specs + [BS((3, 3, tn), lambda i, j: (0, 0, j)), BS((3, 1, tn), lambda i, j: (0, 0, j))],
        out_specs=(BS((TM, tn), lambda i, j: (i, j)), BS((TM, tn), lambda i, j: (i, j))),
        compiler_params=_cp("parallel", "parallel"),
        name="hyena_short_conv",
    )(z, z, z, z, z, z, z, z, z, w3, b3)


def _filter_kernel(w1_ref, b1_ref, w2_ref, b2_ref, sf_ref, w3f_ref, b3f_ref, w3b_ref, b3b_ref, dec_ref, o_ref,
                   hid_scr, *, n, rows, nsub, s_mul, s_sub, s_blk):
    blk = pl.program_id(0)
    a = lax.broadcasted_iota(I32, (rows, 1), 0)
    lane = lax.broadcasted_iota(I32, (rows, LANES), 1)

    def positions(sub):
        s = s_mul * a + (s_sub * sub + s_blk * blk)
        lag = jnp.where(s < n, s, 2 * n - s)
        return s, lag.astype(F32) / float(n)

    @pl.when(pl.program_id(1) == 0)
    def _():
        band = jnp.where(lane <= HY_BANDS, lane, HY_BANDS - lane).astype(F32)
        phase = jnp.where(lane <= HY_BANDS, 0.5 * jnp.pi, 0.0)
        for sub in range(nsub):
            _, t = positions(sub)
            trig = jnp.sin(2.0 * jnp.pi * t * band + phase)
            feats = jnp.where(lane == 0, t, jnp.where(lane <= 2 * HY_BANDS, trig, 0.0))
            hid = jnp.sin(sf_ref[0:1, :] * (_dot_f32(feats, w1_ref[...]) + b1_ref[...]))
            hid_scr[sub] = jnp.sin(sf_ref[1:2, :] * (_dot_f32(hid, w2_ref[...]) + b2_ref[...]))

    w3f, w3b = w3f_ref[...].astype(BF16), w3b_ref[...].astype(BF16)
    for sub in range(nsub):
        s, t = positions(sub)
        hid = hid_scr[sub].astype(BF16)
        fwd = jnp.dot(hid, w3f, preferred_element_type=F32) + b3f_ref[...]
        bwd = jnp.dot(hid, w3b, preferred_element_type=F32) + b3b_ref[...]
        window = jnp.exp(-t * jnp.abs(dec_ref[...])) + HY_MOD_SHIFT
        g = jnp.where(s < n, fwd, jnp.where(s == n, 0.0, bwd)) * window
        if nsub == 1:
            o_ref[...] = g
        else:
            o_ref[:, sub, :] = g


def _hyena_filter(n, fw1, fb1, fw2, fb2, fw3, fb3, sin_freq, decay, two_d):
    width = fw1.shape[1]
    d = decay.shape[0]
    tn = 512
    per = d // tn
    pad = lambda m, r, c: jnp.zeros((r, c), F32).at[:m.shape[0], :m.shape[1]].set(m)
    w1p = pad(fw1, LANES, LANES)
    b1p = pad(fb1.reshape(1, width), 1, LANES)
    w2p = pad(fw2, LANES, LANES)
    b2p = pad(fb2.reshape(1, width), 1, LANES)
    sfp = pad(sin_freq, 2, LANES)
    w3p = pad(fw3, LANES, 2 * d)
    b3 = fb3.reshape(1, 2 * d)
    if two_d:
        rows, nsub, nblk = 2 * n // HY_NB, HY_SUB, HY_NB // HY_SUB
        s_mul, s_sub, s_blk = HY_NB, 1, HY_SUB
        out_shape = SDS((rows, HY_NB, d), F32)
        out_spec = BS((rows, HY_SUB, tn), lambda b, j: (0, b, j))
    else:
        rows, nsub = min(2 * n, 256), 1
        nblk, s_mul, s_sub, s_blk = 2 * n // rows, 1, 0, rows
        out_shape = SDS((2 * n, d), F32)
        out_spec = BS((rows, tn), lambda b, j: (b, j))
    const = lambda r, c: BS((r, c), lambda b, j: (0, 0))
    return pl.pallas_call(
        functools.partial(_filter_kernel, n=n, rows=rows, nsub=nsub, s_mul=s_mul, s_sub=s_sub, s_blk=s_blk),
        out_shape=out_shape,
        grid=(nblk, per),
        in_specs=[const(LANES, LANES), const(1, LANES), const(LANES, LANES), const(1, LANES), const(2, LANES),
                  BS((LANES, tn), lambda b, j: (0, j)), BS((1, tn), lambda b, j: (0, j)),
                  BS((LANES, tn), lambda b, j: (0, per + j)), BS((1, tn), lambda b, j: (0, per + j)),
                  BS((1, tn), lambda b, j: (0, j))],
        out_specs=out_spec,
        scratch_shapes=[pltpu.VMEM((nsub, rows, LANES), F32)],
        compiler_params=_cp("parallel", "arbitrary"),
        name="hyena_filter",
    )(w1p, b1p, w2p, b2p, sfp, w3p, b3, w3p, b3, decay.reshape(1, d))


def _dft_tables(n_total):
    nb = HY_NB
    na = n_total // nb
    ka = np.arange(na)[:, None]
    a = np.arange(na)[None, :]
    ang_a = 2.0 * np.pi * ((ka * a) % na) / na
    fa = np.concatenate([np.cos(ang_a), -np.sin(ang_a)], axis=0)
    kb = np.arange(nb)[:, None]
    b = np.arange(nb)[None, :]
    ang_b = 2.0 * np.pi * ((kb * b) % nb) / nb
    c, s = np.cos(ang_b), np.sin(ang_b)
    fwd_b = np.block([[c, s], [-s, c]])
    inv_b = np.block([[c, -s], [s, c]]) / float(n_total)
    ang_t = 2.0 * np.pi * ((np.arange(nb)[:, None] * np.arange(na)[None, :]) % n_total) / n_total
    tw = (np.cos(ang_t), -np.sin(ang_t))
    inv_a = np.concatenate([np.cos(ang_a[:na // 2]), -np.sin(ang_a[:na // 2])], axis=1)
    f = lambda m: jnp.asarray(m, F32)
    return dict(na=na, fa=f(fa), fwd_b=f(fwd_b), inv_b=f(inv_b), tw_re=f(tw[0]), tw_im=f(tw[1]), inv_a=f(inv_a))


def _dft_a_kernel(f_ref, x_ref, twr_ref, twi_ref, re_ref, im_ref, *, na):
    f = f_ref[...]
    for sub in range(HY_SUB):
        pq = jnp.dot(f, x_ref[:, sub, :].astype(BF16), preferred_element_type=F32)
        p, q = pq[:na], pq[na:]
        twr, twi = twr_ref[sub], twi_ref[sub]
        re_ref[:, sub, :] = p * twr - q * twi
        im_ref[:, sub, :] = p * twi + q * twr


def _dft_a(x3, f_bf, tw_re, tw_im):
    ka_rows, nb, d = x3.shape
    na = f_bf.shape[0] // 2
    tn = min(d, 512)
    twr = tw_re.reshape(nb, na, 1)
    twi = tw_im.reshape(nb, na, 1)
    tws = BS((HY_SUB, na, 1), lambda b, j: (b, 0, 0))
    out = BS((na, HY_SUB, tn), lambda b, j: (0, b, j))
    return pl.pallas_call(
        functools.partial(_dft_a_kernel, na=na),
        out_shape=(SDS((na, nb, d), F32), SDS((na, nb, d), F32)),
        grid=(nb // HY_SUB, d // tn),
        in_specs=[BS((2 * na, ka_rows), lambda b, j: (0, 0)),
                  BS((ka_rows, HY_SUB, tn), lambda b, j: (0, b, j)),
                  tws, tws],
        out_specs=(out, out),
        compiler_params=_cp("parallel", "parallel"),
        name="dft_stage_a",
    )(f_bf, x3, twr, twi)


def _dft_b_kernel(fwd_ref, inv_ref, ur_ref, ui_ref, gr_ref, gi_ref, twr_ref, twi_ref, br_ref, bi_ref):
    nb = HY_NB
    fwd = fwd_ref[...]
    stack = lambda re, im: jnp.concatenate([re[...], im[...]], axis=0).astype(BF16)
    us = jnp.dot(fwd, stack(ur_ref, ui_ref), preferred_element_type=F32)
    gs = jnp.dot(fwd, stack(gr_ref, gi_ref), preferred_element_type=F32)
    ure, uim, gre, gim = us[:nb], us[nb:], gs[:nb], gs[nb:]
    yre = ure * gre - uim * gim
    yim = ure * gim + uim * gre
    ys = jnp.concatenate([yre, yim], axis=0).astype(BF16)
    bs = jnp.dot(inv_ref[...], ys, preferred_element_type=F32)
    bre, bim = bs[:nb], bs[nb:]
    twr, twi = twr_ref[...], twi_ref[...]
    br_ref[...] = bre * twr + bim * twi
    bi_ref[...] = bim * twr - bre * twi


def _dft_b(ur, ui, gr, gi, fwd_bf, inv_bf, tw_re, tw_im):
    na, nb, d = ur.shape
    tn = min(d, 2048)
    twr = tw_re.T.reshape(na, nb, 1)
    twi = tw_im.T.reshape(na, nb, 1)
    data = BS((None, nb, tn), lambda k, j: (k, 0, j))
    tws = BS((None, nb, 1), lambda k, j: (k, 0, 0))
    return pl.pallas_call(
        _dft_b_kernel,
        out_shape=(SDS((na, nb, d), F32), SDS((na, nb, d), F32)),
        grid=(na, d // tn),
        in_specs=[BS((2 * nb, 2 * nb), lambda k, j: (0, 0)), BS((2 * nb, 2 * nb), lambda k, j: (0, 0)),
                  data, data, data, data, tws, tws],
        out_specs=(data, data),
        compiler_params=_cp("parallel", "parallel"),
        name="dft_stage_b",
    )(fwd_bf, inv_bf, ur, ui, gr, gi, twr, twi)


def _idft_a_kernel(f_ref, br_ref, bi_ref, o_ref):
    f = f_ref[...]
    for sub in range(HY_SUB):
        bs = jnp.concatenate([br_ref[:, sub, :], bi_ref[:, sub, :]], axis=0).astype(BF16)
        o_ref[:, sub, :] = jnp.dot(f, bs, preferred_element_type=F32)


def _idft_a(br, bi, inv_a_bf):
    na, nb, d = br.shape
    tn = min(d, 512)
    data = BS((na, HY_SUB, tn), lambda b, j: (0, b, j))
    return pl.pallas_call(
        _idft_a_kernel,
        out_shape=SDS((na // 2, nb, d), F32),
        grid=(nb // HY_SUB, d // tn),
        in_specs=[BS((na // 2, 2 * na), lambda b, j: (0, 0)), data, data],
        out_specs=BS((na // 2, HY_SUB, tn), lambda b, j: (0, b, j)),
        compiler_params=_cp("parallel", "parallel"),
        name="idft_stage_a",
    )(inv_a_bf, br, bi)


def _ctx_conv_kernel(g_ref, u_ref, o_ref, gg_scr, *, n):
    g = g_ref[...]
    gg_scr[0:2 * n, :] = g
    gg_scr[2 * n:4 * n, :] = g

    def body(s, acc):
        return acc + gg_scr[pl.ds(2 * n - s, n), :] * u_ref[pl.ds(s, 1), :]

    o_ref[...] = lax.fori_loop(0, n, body, jnp.zeros((n, LANES), F32))


def _ctx_conv(g, u):
    n2, d = g.shape
    n = n2 // 2
    return pl.pallas_call(
        functools.partial(_ctx_conv_kernel, n=n),
        out_shape=SDS((n, d), F32),
        grid=(d // LANES,),
        in_specs=[BS((n2, LANES), lambda j: (0, j)), BS((n, LANES), lambda j: (0, j))],
        out_specs=BS((n, LANES), lambda j: (0, j)),
        scratch_shapes=[pltpu.VMEM((2 * n2, LANES), F32)],
        compiler_params=_cp("parallel"),
        name="hyena_ctx_conv",
    )(g, u)


def _gated_proj_res_kernel(a_ref, y_ref, u_ref, skip_ref, w_ref, b_ref, x_ref, gt_ref, o_ref):
    a = (a_ref[...].astype(F32) * (y_ref[...] + u_ref[...] * skip_ref[...])).astype(BF16)
    y = jnp.dot(a, w_ref[...], preferred_element_type=F32) + b_ref[...]
    o_ref[...] = x_ref[...] + gt_ref[...] * y


def _gated_proj_res(a, y, u, skip, w_bf, bias, x, mod3, n_lat, gate_chunk):
    np_, d = x.shape
    kdim = a.shape[1]
    tn = 1024
    nlt = n_lat // TM
    rows = BS((TM, kdim), lambda i, j: (i, 0))
    return pl.pallas_call(
        _gated_proj_res_kernel,
        out_shape=SDS((np_, d), F32),
        grid=(np_ // TM, d // tn),
        in_specs=[rows, rows, rows,
                  BS((1, kdim), lambda i, j: (0, 0)),
                  BS((kdim, tn), lambda i, j: (0, j)),
                  BS((1, tn), lambda i, j: (0, j)),
                  BS((TM, tn), lambda i, j: (i, j)),
                  _mod_spec(nlt, d, tn, gate_chunk)],
        out_specs=BS((TM, tn), lambda i, j: (i, j)),
        compiler_params=_cp("parallel", "parallel"),
        name="gated_proj_residual",
    )(a, y, u, skip.reshape(1, kdim), w_bf, bias.reshape(1, d), x, mod3)


def _hyena_layer(x, mod3, norm_gain, w_in, b_in, w_short, b_short, fw1, fb1, fw2, fb2, fw3, fb3, sin_freq, decay,
                 skip, w_out, b_out, n_lat, n_ctx):
    np_, d = x.shape
    z = _linear_norm(x, norm_gain, mod3, w_in.astype(BF16), b_in, n_lat, BF16)
    x0, u = _short_conv(z, w_short, b_short, n_lat, n_ctx)
    tabs = _dft_tables(2 * n_lat)
    na = tabs["na"]
    g3 = _hyena_filter(n_lat, fw1, fb1, fw2, fb2, fw3, fb3, sin_freq, decay, True)
    u3 = u[:n_lat].reshape(na // 2, HY_NB, d)
    fa_bf = tabs["fa"].astype(BF16)
    ur, ui = _dft_a(u3, fa_bf[:, :na // 2], tabs["tw_re"], tabs["tw_im"])
    gr, gi = _dft_a(g3, fa_bf, tabs["tw_re"], tabs["tw_im"])
    br, bi = _dft_b(ur, ui, gr, gi, tabs["fwd_b"].astype(BF16), tabs["inv_b"].astype(BF16),
                    tabs["tw_re"], tabs["tw_im"])
    y_lat = _idft_a(br, bi, tabs["inv_a"].astype(BF16)).reshape(n_lat, d)
    g_ctx = _hyena_filter(n_ctx, fw1, fb1, fw2, fb2, fw3, fb3, sin_freq, decay, False)
    y_ctx = _ctx_conv(g_ctx, u[n_lat:n_lat + n_ctx])
    y = jnp.concatenate([y_lat, y_ctx, jnp.zeros((np_ - n_lat - n_ctx, d), F32)], axis=0)
    return _gated_proj_res(x0, y, u, skip, w_out.astype(BF16), b_out, x, mod3, n_lat, 2)


S5_T = 16
S5_GS = 16
S5_P = 64
S5_GB = 8


def _s5_discretize(lr_row, li_row, dt):
    mag = jnp.exp(lr_row * dt)
    a_re, a_im = mag * jnp.cos(li_row * dt), mag * jnp.sin(li_row * dt)
    xr, xi = a_re - 1.0, a_im
    den = 1.0 / (lr_row * lr_row + li_row * li_row)
    return (xr * lr_row + xi * li_row) * den, (xi * lr_row - xr * li_row) * den


def _s5_power_c(lr_col, li_col, dt, cre_t, cim_t, tau):
    width = S5_T * S5_GS
    lane = lax.broadcasted_iota(I32, (S5_GS, width), 1)
    expand = jnp.where(lane % S5_GS == lax.broadcasted_iota(I32, (S5_GS, width), 0), 1.0, 0.0)
    c_re = _dot_select(cre_t, expand)
    c_im = _dot_select(cim_t, expand)
    mag = jnp.exp(tau * (lr_col * dt))
    p_re, p_im = mag * jnp.cos(tau * (li_col * dt)), mag * jnp.sin(tau * (li_col * dt))
    z_re = p_re * c_re - p_im * c_im
    z_im = p_re * c_im + p_im * c_re
    return jnp.concatenate([z_re, -z_im], axis=0)


def _fill_group_perm(perm_scr, transposed):
    n = S5_GB * S5_T * S5_GS
    rows_per = 256
    for b in range(n // rows_per):
        r = b * rows_per + lax.broadcasted_iota(I32, (rows_per, n), 0)
        q = lax.broadcasted_iota(I32, (rows_per, n), 1)
        grouped, stream = (q, r) if transposed else (r, q)
        j = grouped // (S5_T * S5_GS)
        s = (grouped // S5_GS) % S5_T
        c = grouped % S5_GS
        hit = stream == s * (S5_GB * S5_GS) + j * S5_GS + c
        perm_scr[b * rows_per:(b + 1) * rows_per, :] = jnp.where(hit, 1.0, 0.0).astype(BF16)


def _s5_chunk_kernel(h_ref, lr_row, li_row, lr_col, li_col, ls_ref, bre_t, bim_t, cre_t, cim_t,
                     y_ref, s_ref, zc_ref, mt_scr, u_scr, perm_scr):
    dd = pl.program_id(1)
    width = S5_T * S5_GS

    @pl.when(jnp.logical_and(pl.program_id(0) == 0, dd == 0))
    def _():
        _fill_group_perm(perm_scr, True)

    @pl.when(dd == 0)
    def _():
        hcat = jnp.concatenate([h_ref[:, s, :].astype(BF16) for s in range(S5_T)], axis=1)
        u_scr[...] = jnp.dot(hcat, perm_scr[...], preferred_element_type=F32).astype(BF16)

    lane = lax.broadcasted_iota(I32, (S5_GS, width), 1)
    t_lane = lax.broadcasted_iota(I32, (1, width), 1) // S5_GS
    t_row = lax.broadcasted_iota(I32, (width, 1), 0) // S5_GS
    row = lax.broadcasted_iota(I32, (width, S5_GS), 0)
    expand_t = jnp.where(row % S5_GS == lax.broadcasted_iota(I32, (width, S5_GS), 1), 1.0, 0.0)
    tau_k = jnp.where(dd == 0, t_lane, S5_T - 1 - t_lane).astype(F32)
    tau_g = jnp.where(dd == 0, S5_T - 1 - t_row, t_row).astype(F32)
    for j in range(S5_GB):
        dt = jnp.exp(ls_ref[j])
        q_re, q_im = _s5_discretize(lr_row[j], li_row[j], dt)
        bb_re = q_re * bre_t[j] - q_im * bim_t[j]
        bb_im = q_re * bim_t[j] + q_im * bre_t[j]
        z = _s5_power_c(lr_col[j], li_col[j], dt, cre_t[j], cim_t[j], tau_k)
        k0 = _dot_f32(jnp.concatenate([bb_re, bb_im], axis=1), z)
        a_mag = jnp.exp(lr_col[j] * dt)
        a_re, a_im = a_mag * jnp.cos(li_col[j] * dt), a_mag * jnp.sin(li_col[j] * dt)
        z_re, z_nim = z[:S5_P], z[S5_P:]
        zc_ref[j] = jnp.concatenate([a_re * z_re + a_im * z_nim, a_re * z_nim - a_im * z_re], axis=0).astype(BF16)

        @pl.when(dd == 0)
        def _():
            for s in range(S5_T):
                blk = k0 if s == 0 else jnp.where(lane >= s * S5_GS, pltpu.roll(k0, s * S5_GS, axis=1), 0.0)
                mt_scr[s * S5_GS:(s + 1) * S5_GS, :] = blk.astype(BF16)

        @pl.when(dd == 1)
        def _():
            for s in range(S5_T):
                sh = (S5_T - 1 - s) * S5_GS
                blk = k0 if sh == 0 else jnp.where(lane < (s + 1) * S5_GS, pltpu.roll(k0, width - sh, axis=1), 0.0)
                mt_scr[s * S5_GS:(s + 1) * S5_GS, :] = blk.astype(BF16)

        u = u_scr[:, j * width:(j + 1) * width]
        y_ref[j] = jnp.dot(u, mt_scr[...], preferred_element_type=F32).astype(BF16)
        be_re = _dot_select(bb_re, expand_t, select_on_left=True)
        be_im = _dot_select(bb_im, expand_t, select_on_left=True)
        mag = jnp.exp(tau_g * (lr_row[j] * dt))
        p_re, p_im = mag * jnp.cos(tau_g * (li_row[j] * dt)), mag * jnp.sin(tau_g * (li_row[j] * dt))
        g = jnp.concatenate([p_re * be_re - p_im * be_im, p_re * be_im + p_im * be_re], axis=1)
        s_ref[:, j, :] = jnp.dot(u, g.astype(BF16), preferred_element_type=F32)


def _s5_chunks(h3, prm):
    nc, _, d = h3.shape
    groups, width = d // S5_GS, S5_T * S5_GS
    gl = S5_GB * S5_GS
    par = lambda r, c: BS((None, S5_GB, r, c), lambda i, dd: (dd, i, 0, 0))
    row, col, one = par(1, S5_P), par(S5_P, 1), par(1, 1)
    bt, ct = par(S5_GS, S5_P), par(S5_P, S5_GS)
    return pl.pallas_call(
        _s5_chunk_kernel,
        out_shape=(SDS((2, groups, nc, width), BF16), SDS((2, nc, groups, 2 * S5_P), F32),
                   SDS((2, groups, 2 * S5_P, width), BF16)),
        grid=(groups // S5_GB, 2),
        in_specs=[BS((nc, S5_T, gl), lambda i, dd: (0, 0, i)), row, row, col, col, one, bt, bt, ct, ct],
        out_specs=(BS((None, S5_GB, nc, width), lambda i, dd: (dd, i, 0, 0)),
                   BS((None, nc, S5_GB, 2 * S5_P), lambda i, dd: (dd, 0, i, 0)),
                   BS((None, S5_GB, 2 * S5_P, width), lambda i, dd: (dd, i, 0, 0))),
        scratch_shapes=[pltpu.VMEM((width, width), BF16), pltpu.VMEM((nc, S5_GB * width), BF16),
                        pltpu.VMEM((S5_GB * width, S5_GB * width), BF16)],
        compiler_params=_cp("arbitrary", "arbitrary"),
        name="s5_chunks",
    )(h3, prm["lr_row"], prm["li_row"], prm["lr_col"], prm["li_col"], prm["ls"], prm["bre_t"], prm["bim_t"],
      prm["cre_t"], prm["cim_t"])


def _s5_state_kernel(s0_ref, s1_ref, lr_ref, li_ref, ls_ref, h0_ref, h1_ref, st_scr, *, steps):
    @pl.when(pl.program_id(0) == 0)
    def _():
        st_scr[...] = jnp.zeros(st_scr.shape, F32)

    def coeffs(dd):
        dt = jnp.exp(ls_ref[dd])
        mag = jnp.exp(float(S5_T) * lr_ref[dd] * dt)
        ang = float(S5_T) * li_ref[dd] * dt
        a_re, a_im = mag * jnp.cos(ang), mag * jnp.sin(ang)
        lane = lax.broadcasted_iota(I32, a_im.shape, 1)
        a_sw = jnp.where(lane < S5_P, -a_im, a_im)
        return a_re, a_sw

    def run(dd, s_ref, h_ref, order):
        a_re, a_sw = coeffs(dd)
        h, hs = st_scr[2 * dd], st_scr[2 * dd + 1]
        for c in order:
            h_ref[c] = h
            s = s_ref[c]
            s_swapped = pltpu.roll(s, S5_P, axis=1)
            h, hs = a_re * h + a_sw * hs + s, a_re * hs - a_sw * h + s_swapped
        st_scr[2 * dd] = h
        st_scr[2 * dd + 1] = hs

    run(0, s0_ref, h0_ref, range(steps))
    run(1, s1_ref, h1_ref, reversed(range(steps)))


def _s5_states(s_t, prm, first_chunk):
    _, nc, groups, w = s_t.shape
    steps = 8
    nblk = nc // steps
    off = first_chunk // steps
    fwd = BS((None, steps, groups, w), lambda c: (0, (c + off) % nblk, 0, 0))
    bwd = BS((None, steps, groups, w), lambda c: (1, nblk - 1 - c, 0, 0))
    fwd_o = BS((steps, groups, w), lambda c: ((c + off) % nblk, 0, 0))
    bwd_o = BS((steps, groups, w), lambda c: (nblk - 1 - c, 0, 0))
    par = lambda last: BS((2, groups, last), lambda c: (0, 0, 0))
    return pl.pallas_call(
        functools.partial(_s5_state_kernel, steps=steps),
        out_shape=(SDS((nc, groups, w), F32), SDS((nc, groups, w), F32)),
        grid=(nblk,),
        in_specs=[fwd, bwd, par(w), par(w), par(1)],
        out_specs=(fwd_o, bwd_o),
        scratch_shapes=[pltpu.VMEM((4, groups, w), F32)],
        compiler_params=_cp("arbitrary"),
        name="s5_state_scan",
    )(s_t, s_t, prm["lr2"], prm["li2"], prm["ls_g"])


def _s5_output_kernel(y1_ref, h0_ref, h1_ref, zc_ref, y_ref, perm_scr):
    @pl.when(jnp.logical_and(pl.program_id(0) == 0, pl.program_id(1) == 0))
    def _():
        _fill_group_perm(perm_scr, False)

    parts = []
    for j in range(S5_GB):
        z = jnp.concatenate([zc_ref[0, j], zc_ref[1, j]], axis=0)
        h = jnp.concatenate([h0_ref[:, j, :], h1_ref[:, j, :]], axis=1).astype(BF16)
        yj = y1_ref[0, j].astype(F32) + y1_ref[1, j].astype(F32) + jnp.dot(h, z, preferred_element_type=F32)
        parts.append(yj.astype(BF16))
    ynat = jnp.dot(jnp.concatenate(parts, axis=1), perm_scr[...], preferred_element_type=F32)
    gl = S5_GB * S5_GS
    for s in range(S5_T):
        y_ref[:, s, :] = ynat[:, s * gl:(s + 1) * gl]


def _s5_outputs(y1, h0, h1, zc):
    _, groups, nc, width = y1.shape
    halves = 2
    nch = nc // halves
    gl = S5_GB * S5_GS
    states = BS((nch, S5_GB, 2 * S5_P), lambda i, c: (c, i, 0))
    return pl.pallas_call(
        _s5_output_kernel,
        out_shape=SDS((nc, S5_T, groups * S5_GS), F32),
        grid=(groups // S5_GB, halves),
        in_specs=[BS((2, S5_GB, nch, width), lambda i, c: (0, i, c, 0)), states, states,
                  BS((2, S5_GB, 2 * S5_P, width), lambda i, c: (0, i, 0, 0))],
        out_specs=BS((nch, S5_T, gl), lambda i, c: (c, 0, i)),
        scratch_shapes=[pltpu.VMEM((S5_GB * width, S5_GB * width), BF16)],
        compiler_params=_cp("arbitrary", "arbitrary"),
        name="s5_outputs",
    )(y1, h0, h1, zc)


def _s5_glu_kernel(x_ref, gain_ref, sc_ref, sh_ref, ys_ref, skip_ref, wv_ref, wg_ref, bv_ref, bg_ref,
                   xo_ref, gt_ref, o_ref, z_scr):
    @pl.when(pl.program_id(1) == 0)
    def _():
        h = _norm_mod(x_ref[...], gain_ref[...], sc_ref[...], sh_ref[...])
        z_scr[...] = jax.nn.gelu(skip_ref[...] * h + ys_ref[...]).astype(BF16)

    z = z_scr[...]
    val = jnp.dot(z, wv_ref[...], preferred_element_type=F32) + bv_ref[...]
    gate = jnp.dot(z, wg_ref[...], preferred_element_type=F32) + bg_ref[...]
    o_ref[...] = xo_ref[...] + gt_ref[...] * (val * (1.0 / (1.0 + jnp.exp(-gate))))


def _s5_glu(x, gain, mod3, ys, skip, w_bf, bias, n_lat):
    np_, d = x.shape
    tn = 512
    per = d // tn
    nlt = n_lat // TM
    b2 = bias.reshape(1, 2 * d)
    return pl.pallas_call(
        _s5_glu_kernel,
        out_shape=SDS((np_, d), F32),
        grid=(np_ // TM, per),
        in_specs=[BS((TM, d), lambda i, j: (i, 0)),
                  BS((1, d), lambda i, j: (0, 0)),
                  _mod_spec_full(nlt, d, 1),
                  _mod_spec_full(nlt, d, 0),
                  BS((TM, d), lambda i, j: (i, 0)),
                  BS((1, d), lambda i, j: (0, 0)),
                  BS((d, tn), lambda i, j: (0, j)),
                  BS((d, tn), lambda i, j: (0, per + j)),
                  BS((1, tn), lambda i, j: (0, j)),
                  BS((1, tn), lambda i, j: (0, per + j)),
                  BS((TM, tn), lambda i, j: (i, j)),
                  _mod_spec(nlt, d, tn, 2)],
        out_specs=BS((TM, tn), lambda i, j: (i, j)),
        scratch_shapes=[pltpu.VMEM((TM, d), BF16)],
        compiler_params=_cp("parallel", "arbitrary"),
        name="s5_glu",
    )(x, gain.reshape(1, d), mod3, mod3, ys, skip.reshape(1, d), w_bf, w_bf, b2, b2, x, mod3)


def _norm_mod_rows_kernel(x_ref, gain_ref, sc_ref, sh_ref, o_ref):
    o_ref[...] = _norm_mod(x_ref[...], gain_ref[...], sc_ref[...], sh_ref[...])


def _norm_mod_rows(x, gain, mod3, n_lat):
    np_, d = x.shape
    nlt = n_lat // TM
    return pl.pallas_call(
        _norm_mod_rows_kernel,
        out_shape=SDS((np_, d), F32),
        grid=(np_ // TM, 1),
        in_specs=[BS((TM, d), lambda i, j: (i, 0)),
                  BS((1, d), lambda i, j: (0, 0)),
                  _mod_spec_full(nlt, d, 1),
                  _mod_spec_full(nlt, d, 0)],
        out_specs=BS((TM, d), lambda i, j: (i, 0)),
        compiler_params=_cp("parallel", "arbitrary"),
        name="norm_mod_rows",
    )(x, gain.reshape(1, d), mod3, mod3)


def _s5_layer(x, mod3, norm_gain, lam_re, lam_im, log_step, b_re, b_im, c_re, c_im, skip, w_glu, b_glu,
              n_lat, n_ctx):
    np_, d = x.shape
    groups = d // S5_GS
    n_tok = n_lat + n_ctx
    nc = n_tok // S5_T
    h = _norm_mod_rows(x, norm_gain, mod3, n_lat)
    h3 = h[:n_tok].reshape(nc, S5_T, d)
    prm = dict(
        lr_row=lam_re.reshape(2, groups, 1, S5_P), li_row=lam_im.reshape(2, groups, 1, S5_P),
        lr_col=lam_re.reshape(2, groups, S5_P, 1), li_col=lam_im.reshape(2, groups, S5_P, 1),
        ls=log_step.reshape(2, groups, 1, 1), ls_g=log_step.reshape(2, groups, 1),
        bre_t=b_re.transpose(0, 1, 3, 2), bim_t=b_im.transpose(0, 1, 3, 2),
        cre_t=c_re.transpose(0, 1, 3, 2), cim_t=c_im.transpose(0, 1, 3, 2),
        lr2=jnp.concatenate([lam_re, lam_re], axis=-1), li2=jnp.concatenate([lam_im, lam_im], axis=-1))
    y1, s, zc = _s5_chunks(h3, prm)
    h0, h1 = _s5_states(s, prm, n_lat // S5_T)
    y = _s5_outputs(y1, h0, h1, zc).reshape(n_tok, d)
    ys = jnp.concatenate([y, jnp.zeros((np_ - n_tok, d), F32)], axis=0)
    return _s5_glu(x, norm_gain, mod3, ys, skip, w_glu.astype(BF16), b_glu, n_lat)


def kernel(x, c, ctx, c_ctx, w_mod, b_mod, norm_mix, norm_ffn, router_w, router_b, w_gate_up, b_gate_up, w_down, b_down, win_w_qkv, win_w_o, win_q_gain, win_k_gain, win_sinks, full_w_qkv, full_w_o, full_q_gain, full_k_gain, hy_w_in, hy_b_in, hy_w_short, hy_b_short, hy_filt_w1, hy_filt_b1, hy_filt_w2, hy_filt_b2, hy_filt_w3, hy_filt_b3, hy_sin_freq, hy_decay, hy_skip, hy_w_out, hy_b_out, s5_lam_re, s5_lam_im, s5_log_step, s5_b_re, s5_b_im, s5_c_re, s5_c_im, s5_skip, s5_w_glu, s5_b_glu):
    batch, n_lat, d = x.shape
    n_ctx = ctx.shape[1]
    assert batch == 1 and n_lat % TM == 0 and n_ctx % TOK_DMA == 0 and (n_lat + n_ctx) % S5_T == 0
    depth = w_mod.shape[0]
    n_tok = n_lat + n_ctx
    pad = (-n_ctx) % TM
    cc = jnp.zeros((8, d), F32).at[0].set(c[0]).at[1].set(c_ctx)
    mod = _modulation(cc, w_mod, b_mod)
    xs = jnp.concatenate([x[0], ctx[0], jnp.zeros((pad, d), F32)], axis=0)
    cos2, sin2 = _rope_tables(n_lat)
    for i in range(depth):
        kind, j = i % 4, i // 4
        mod3 = mod[i, :2].reshape(2, 1, 6 * d)
        if kind == 0:
            xs = _attn_layer(xs, mod3, norm_mix[i], win_w_qkv[j], win_w_o[j], win_q_gain[j], win_k_gain[j],
                             win_sinks[j], cos2, sin2, n_lat, n_ctx, True)
        elif kind == 1:
            xs = _attn_layer(xs, mod3, norm_mix[i], full_w_qkv[j], full_w_o[j], full_q_gain[j], full_k_gain[j],
                             None, cos2, sin2, n_lat, n_ctx, False)
        elif kind == 2:
            xs = _hyena_layer(xs, mod3, norm_mix[i], hy_w_in[j], hy_b_in[j], hy_w_short[j], hy_b_short[j],
                              hy_filt_w1[j], hy_filt_b1[j], hy_filt_w2[j], hy_filt_b2[j], hy_filt_w3[j],
                              hy_filt_b3[j], hy_sin_freq[j], hy_decay[j], hy_skip[j], hy_w_out[j], hy_b_out[j],
                              n_lat, n_ctx)
        else:
            xs = _s5_layer(xs, mod3, norm_mix[i], s5_lam_re[j], s5_lam_im[j], s5_log_step[j], s5_b_re[j],
                           s5_b_im[j], s5_c_re[j], s5_c_im[j], s5_skip[j], s5_w_glu[j], s5_b_glu[j], n_lat, n_ctx)
        xs = _moe(xs, norm_ffn[i], mod3, router_w[i], router_b[i], i, w_gate_up, b_gate_up[i],
                  w_down, b_down[i], n_lat, n_tok)
    return xs[:n_lat].reshape(x.shape)
```
